```python
import math
import jax
import jax.numpy as jnp
from jax import lax
import numpy as np

D_MODEL = 4096
BATCH = 1
SEQ = 16384
DEPTH = 2

GRID_W = 64
CTX_LEN = 256
D_MIX = D_MODEL
SSD_D_INNER = D_MIX // 2
SSD_HEAD_DIM = 64
SSD_HEADS = SSD_D_INNER // SSD_HEAD_DIM
SSD_GROUPS = 4
SSD_HPG = SSD_HEADS // SSD_GROUPS
SSD_STATE = 128
SSD_GN = SSD_GROUPS * SSD_STATE
SSD_XBC = SSD_D_INNER + 2 * SSD_GN
SSD_CONV = 5
SSD_CHUNK = 128
NA_WIDTH = D_MIX // 4
NA_HEAD_DIM = 128
NA_HEADS = NA_WIDTH // NA_HEAD_DIM
NA_WIN_R = 8
NA_WIN_C = 16
ROPE_BASE = 10000.0
POOL_WIDTH = D_MIX - SSD_D_INNER - NA_WIDTH
POOL_WINDOWS = (2, 4, 8, 16)
POOL_GROUP = POOL_WIDTH // len(POOL_WINDOWS)
D_FF = 11008
MLP_CONV = 3
N_MOD = 6
RMS_EPS = 1e-6
OFF_Z = 0
OFF_Q = OFF_Z + SSD_D_INNER
OFF_POOL = OFF_Q + NA_WIDTH
OFF_XBC = OFF_POOL + POOL_WIDTH
OFF_DT = OFF_XBC + SSD_XBC
OFF_K = OFF_DT + 2 * SSD_HEADS
OFF_V = OFF_K + NA_WIDTH
IN_COLS = OFF_V + NA_WIDTH

kernel_name = "hymba_style_ssd_natten_pool_dit"


def rmsnorm(x, w):
    xf = x.astype(jnp.float32)
    y = xf * lax.rsqrt(jnp.mean(xf * xf, axis=-1, keepdims=True) + RMS_EPS)
    return (y * w.astype(jnp.float32)).astype(x.dtype)


def modulate(x, shift, scale):
    return x * (1 + scale) + shift


def dwconv1d(x, w, b):
    K = w.shape[0]
    L = x.shape[1]
    left = (K - 1) // 2
    xp = jnp.pad(x, ((0, 0), (left, K - 1 - left), (0, 0)))
    y = b
    for k in range(K):
        y = y + xp[:, k:k + L] * w[k]
    return y


def axial_rope(L):
    t = jnp.arange(L)
    half = NA_HEAD_DIM // 2
    inv = ROPE_BASE ** (-jnp.arange(0, half, 2, dtype=jnp.float32) / half)
    ar = (t // GRID_W).astype(jnp.float32)[:, None] * inv
    ac = (t % GRID_W).astype(jnp.float32)[:, None] * inv
    ang = jnp.concatenate([ar, ar, ac, ac], axis=-1)[:, None, :]
    return jnp.cos(ang), jnp.sin(ang)


def _rotate_half(v):
    v1, v2 = jnp.split(v, 2, axis=-1)
    return jnp.concatenate([-v2, v1], axis=-1)


def apply_axial_rope(x, cos, sin):
    xf = x.astype(jnp.float32)
    half = x.shape[-1] // 2
    rot = jnp.concatenate([_rotate_half(xf[..., :half]), _rotate_half(xf[..., half:])], axis=-1)
    return (xf * cos + rot * sin).astype(x.dtype)


def _segsum(a):
    T = a.shape[-1]
    cs = jnp.cumsum(a, axis=-1)
    diff = cs[..., :, None] - cs[..., None, :]
    return jnp.where(jnp.tril(jnp.ones((T, T), dtype=bool)), diff, -jnp.inf)


def _chunks(a):
    b, L = a.shape[:2]
    return a.reshape((b, L // SSD_CHUNK, SSD_CHUNK) + a.shape[2:])


def _chunk_end_states(xc, a_cs, bc):
    decay = jnp.exp(a_cs[..., -1:] - a_cs)
    return jnp.einsum('bclgn,bgrcl,bclgrp->bcgrpn', bc, decay, xc)


def ssd_output(xdt, da, bm, cm, h0):
    xc, bc, cc = _chunks(xdt), _chunks(bm), _chunks(cm)
    ac = _chunks(da).transpose(0, 3, 4, 1, 2)
    a_cs = jnp.cumsum(ac, axis=-1)
    cb = jnp.einsum('bclgn,bcsgn->bgcls', cc, bc)
    y_diag = jnp.einsum('bgcls,bgrcls,bcsgrp->bclgrp', cb, jnp.exp(_segsum(ac)), xc)
    s_end = _chunk_end_states(xc[:, :-1], a_cs[..., :-1, :], bc[:, :-1])
    s_cat = jnp.concatenate([h0[:, None], s_end], axis=1)
    tot = a_cs[..., :-1, -1]
    decay_chunk = jnp.exp(_segsum(jnp.pad(tot, ((0, 0), (0, 0), (0, 0), (1, 0)))))
    h_in = jnp.einsum('bgrzc,bcgrpn->bzgrpn', decay_chunk, s_cat)
    y_off = jnp.einsum('bclgn,bcgrpn,bgrcl->bclgrp', cc, h_in, jnp.exp(a_cs))
    return (y_diag + y_off).reshape(xdt.shape)


def ssd_final_state(xdt, da, bm, h0):
    xc, bc = _chunks(xdt), _chunks(bm)
    a_cs = jnp.cumsum(_chunks(da).transpose(0, 3, 4, 1, 2), axis=-1)
    s_end = _chunk_end_states(xc, a_cs, bc)
    cum = jnp.cumsum(a_cs[..., -1], axis=-1)
    total = cum[..., -1:]
    return jnp.exp(total)[..., None] * h0 + jnp.einsum('bgrc,bcgrpn->bgrpn', jnp.exp(total - cum), s_end)


def _ssd_prep(xbc, dt_raw, conv_w, conv_b, dt_bias, a_log):
    b, L, _ = xbc.shape
    xbc = jax.nn.silu(dwconv1d(xbc, conv_w, conv_b)).astype(jnp.float32)
    xs = xbc[..., :SSD_D_INNER].reshape(b, L, SSD_GROUPS, SSD_HPG, SSD_HEAD_DIM)
    bm = xbc[..., SSD_D_INNER:SSD_D_INNER + SSD_GN].reshape(b, L, SSD_GROUPS, SSD_STATE)
    cm = xbc[..., SSD_D_INNER + SSD_GN:].reshape(b, L, SSD_GROUPS, SSD_STATE)
    dt = jax.nn.softplus(dt_raw.astype(jnp.float32).reshape(b, L, 2, SSD_HEADS) + dt_bias.astype(jnp.float32))
    da = dt * -jnp.exp(a_log.astype(jnp.float32))
    shape = (b, L, 2, SSD_GROUPS, SSD_HPG)
    return xs, bm, cm, dt.reshape(shape), da.reshape(shape)


def ssd_mixer(z_l, xbc_l, dt_l, z_c, xbc_c, dt_c, conv_w, conv_b, dt_bias, a_log, d_skip, norm_w):
    b, L, _ = xbc_l.shape
    xl, bl, cl, dtl, dal = _ssd_prep(xbc_l, dt_l, conv_w, conv_b, dt_bias, a_log)
    xc, bc, cc, dtc, dac = _ssd_prep(xbc_c, dt_c, conv_w, conv_b, dt_bias, a_log)
    d_sk = d_skip.astype(jnp.float32).reshape(SSD_GROUPS, SSD_HPG, 1)
    h_zero = jnp.zeros((b, SSD_GROUPS, SSD_HPG, SSD_HEAD_DIM, SSD_STATE), jnp.float32)
    y_l = xl * d_sk
    y_c = xc * d_sk if z_c is not None else None
    for direction in range(2):
        rev = (lambda a: a[:, ::-1]) if direction else (lambda a: a)
        xdt_c = rev(xc * dtc[:, :, direction, :, :, None])
        da_c = rev(dac[:, :, direction])
        b_c = rev(bc)
        h_ctx = ssd_final_state(xdt_c, da_c, b_c, h_zero)
        y_l = y_l + rev(ssd_output(rev(xl * dtl[:, :, direction, :, :, None]), rev(dal[:, :, direction]),
                                   rev(bl), rev(cl), h_ctx))
        if z_c is not None:
            y_c = y_c + rev(ssd_output(xdt_c, da_c, b_c, rev(cc), h_zero))
    out_l = rmsnorm(y_l.reshape(b, L, SSD_D_INNER) * jax.nn.silu(z_l.astype(jnp.float32)), norm_w).astype(z_l.dtype)
    out_c = None
    if z_c is not None:
        out_c = rmsnorm(y_c.reshape(b, -1, SSD_D_INNER) * jax.nn.silu(z_c.astype(jnp.float32)), norm_w).astype(z_c.dtype)
    return out_l, out_c


def neighbourhood_attention(q_l, k_l, v_l, q_c, k_c, v_c, q_norm_w, k_norm_w, rpb):
    b, L, _ = q_l.shape
    lc = k_c.shape[1]
    rows = L // GRID_W
    wr = min(NA_WIN_R, rows)
    hd = (NA_HEADS, NA_HEAD_DIM)
    scale = NA_HEAD_DIM ** -0.5
    cos, sin = axial_rope(L)
    q = apply_axial_rope(rmsnorm(q_l.reshape(b, L, *hd), q_norm_w), cos, sin)
    k = apply_axial_rope(rmsnorm(k_l.reshape(b, L, *hd), k_norm_w), cos, sin)
    v = v_l.reshape(b, L, *hd)
    kc = rmsnorm(k_c.reshape(b, lc, *hd), k_norm_w)
    vc = v_c.reshape(b, lc, *hd)
    r_idx = jnp.arange(rows)
    band = jnp.clip(r_idx - wr // 2, 0, rows - wr)[:, None] + jnp.arange(wr)[None, :]
    qg = q.reshape(b, rows, GRID_W, *hd)
    kg = k.reshape(b, rows, GRID_W, *hd)[:, band]
    vg = v.reshape(b, rows, GRID_W, *hd)[:, band]
    col = jnp.arange(GRID_W)
    c0 = jnp.clip(col - NA_WIN_C // 2, 0, GRID_W - NA_WIN_C)
    col_ok = (col[None, :] >= c0[:, None]) & (col[None, :] < c0[:, None] + NA_WIN_C)
    dr = band - r_idx[:, None] + (NA_WIN_R - 1)
    dc = jnp.clip(col[None, :] - col[:, None] + (NA_WIN_C - 1), 0, 2 * NA_WIN_C - 2)
    bias = rpb.astype(jnp.float32)[:, dr[:, None, :, None], dc[None, :, None, :]]
    s_loc = jnp.einsum('brqhd,brwkhd->bhrqwk', qg, kg).astype(jnp.float32) * scale + bias
    s_loc = jnp.where(col_ok[:, None, :], s_loc, -jnp.inf).reshape(b, NA_HEADS, rows, GRID_W, wr * GRID_W)
    s_ctx = jnp.einsum('brqhd,bjhd->bhrqj', qg, kc).astype(jnp.float32) * scale
    prob = jax.nn.softmax(jnp.concatenate([s_loc, s_ctx], axis=-1), axis=-1).astype(v.dtype)
    p_loc = prob[..., :wr * GRID_W].reshape(b, NA_HEADS, rows, GRID_W, wr, GRID_W)
    p_ctx = prob[..., wr * GRID_W:]
    o = jnp.einsum('bhrqwk,brwkhd->brqhd', p_loc, vg) + jnp.einsum('bhrqj,bjhd->brqhd', p_ctx, vc)
    y_l = o.reshape(b, L, NA_WIDTH)
    y_c = None
    if q_c is not None:
        qc = rmsnorm(q_c.reshape(b, lc, *hd), q_norm_w)
        s = jnp.einsum('bihd,bjhd->bhij', qc, kc).astype(jnp.float32) * scale
        pc = jax.nn.softmax(s, axis=-1).astype(vc.dtype)
        y_c = jnp.einsum('bhij,bjhd->bihd', pc, vc).reshape(b, lc, NA_WIDTH)
    return y_l, y_c


def multiscale_pool(u, pool_w, pool_scale):
    bsz, L, _ = u.shape
    uf = u.astype(jnp.float32)
    cs = jnp.pad(jnp.cumsum(uf, axis=1), ((0, 0), (1, 0), (0, 0)))
    t = jnp.arange(L)
    outs = []
    for g, w in enumerate(POOL_WINDOWS):
        lo = jnp.clip(t - w // 2, 0, L)
        hi = jnp.clip(t + w // 2, 0, L)
        seg = cs[..., g * POOL_GROUP:(g + 1) * POOL_GROUP]
        cnt = (hi - lo).astype(jnp.float32)[None, :, None]
        outs.append((seg[:, hi] - seg[:, lo]) / cnt - uf[..., g * POOL_GROUP:(g + 1) * POOL_GROUP])
    pooled = jnp.stack(outs, axis=2)
    y = jnp.einsum('blgc,gcd->blgd', pooled, pool_w.astype(jnp.float32)).reshape(bsz, L, POOL_WIDTH)
    return (y * pool_scale.astype(jnp.float32)).astype(u.dtype)


def conv_ffn(h, w_up, conv_w, conv_b, w_down):
    u = dwconv1d(h @ w_up, conv_w, conv_b)
    gate, val = jnp.split(u, 2, axis=-1)
    return (jax.nn.silu(gate) * val) @ w_down


def hybrid_layer(x, ctx, c, c_ctx, ada_w, ada_b, norm1_w, w_in, ssd_conv_w, ssd_conv_b, ssd_dt_bias,
                 ssd_a_log, ssd_d, ssd_norm_w, na_q_norm_w, na_k_norm_w, na_rpb, pool_w, pool_scale,
                 w_out, norm2_w, mlp_w_up, mlp_conv_w, mlp_conv_b, mlp_w_down, ctx_out):
    mod_l = (jax.nn.silu(c) @ ada_w + ada_b)[:, None, :]
    mod_c = jax.nn.silu(c_ctx) @ ada_w + ada_b
    sh1_l, sc1_l, g1_l, sh2_l, sc2_l, g2_l = jnp.split(mod_l, N_MOD, axis=-1)
    sh1_c, sc1_c, g1_c, sh2_c, sc2_c, g2_c = jnp.split(mod_c, N_MOD, axis=-1)
    h_l = modulate(rmsnorm(x, norm1_w), sh1_l, sc1_l)
    h_c = modulate(rmsnorm(ctx, norm1_w), sh1_c, sc1_c)
    p_l = h_l @ w_in
    base = 0 if ctx_out else OFF_XBC
    p_c = h_c @ w_in[:, base:]

    def seg_l(off, width):
        return p_l[..., off:off + width]

    def seg_c(off, width):
        return p_c[..., off - base:off - base + width]

    y_ssd_l, y_ssd_c = ssd_mixer(seg_l(OFF_Z, SSD_D_INNER), seg_l(OFF_XBC, SSD_XBC), seg_l(OFF_DT, 2 * SSD_HEADS),
                                 seg_c(OFF_Z, SSD_D_INNER) if ctx_out else None,
                                 seg_c(OFF_XBC, SSD_XBC), seg_c(OFF_DT, 2 * SSD_HEADS),
                                 ssd_conv_w, ssd_conv_b, ssd_dt_bias, ssd_a_log, ssd_d, ssd_norm_w)
    y_na_l, y_na_c = neighbourhood_attention(seg_l(OFF_Q, NA_WIDTH), seg_l(OFF_K, NA_WIDTH), seg_l(OFF_V, NA_WIDTH),
                                             seg_c(OFF_Q, NA_WIDTH) if ctx_out else None,
                                             seg_c(OFF_K, NA_WIDTH), seg_c(OFF_V, NA_WIDTH),
                                             na_q_norm_w, na_k_norm_w, na_rpb)
    y_pool_l = multiscale_pool(seg_l(OFF_POOL, POOL_WIDTH), pool_w, pool_scale)
    mix_l = jnp.concatenate([y_ssd_l, y_na_l, y_pool_l], axis=-1).astype(x.dtype) @ w_out
    x_new = x + g1_l * mix_l
    x_new = x_new + g2_l * conv_ffn(modulate(rmsnorm(x_new, norm2_w), sh2_l, sc2_l),
                                    mlp_w_up, mlp_conv_w, mlp_conv_b, mlp_w_down)
    ctx_new = ctx
    if ctx_out:
        y_pool_c = multiscale_pool(seg_c(OFF_POOL, POOL_WIDTH), pool_w, pool_scale)
        mix_c = jnp.concatenate([y_ssd_c, y_na_c, y_pool_c], axis=-1).astype(ctx.dtype) @ w_out
        ctx_new = ctx + g1_c * mix_c
        ctx_new = ctx_new + g2_c * conv_ffn(modulate(rmsnorm(ctx_new, norm2_w), sh2_c, sc2_c),
                                            mlp_w_up, mlp_conv_w, mlp_conv_b, mlp_w_down)
    return x_new, ctx_new


def setup_inputs(seed: int = 0) -> dict:
    key = jax.random.key(seed)
    ks = jax.random.split(key, 32)
    f32 = jnp.float32

    def nrm(k, shape, scale):
        return jax.random.normal(k, shape, f32) * scale

    dt0 = jnp.exp(jax.random.uniform(ks[10], (DEPTH, 2, SSD_HEADS), f32, math.log(1e-3), math.log(1e-1)))
    return {
        "x": nrm(ks[0], (BATCH, SEQ, D_MODEL), 1.0),
        "c": nrm(ks[1], (BATCH, D_MODEL), 1.0),
        "ctx": nrm(ks[2], (BATCH, CTX_LEN, D_MODEL), 1.0),
        "c_ctx": nrm(ks[3], (D_MODEL,), 1.0),
        "ada_w": nrm(ks[4], (DEPTH, D_MODEL, N_MOD * D_MODEL), 0.5 * D_MODEL ** -0.5),
        "ada_b": nrm(ks[5], (DEPTH, N_MOD * D_MODEL), 0.02),
        "norm1_w": 1.0 + nrm(ks[6], (DEPTH, D_MODEL), 0.02),
        "w_in": nrm(ks[7], (DEPTH, D_MODEL, IN_COLS), D_MODEL ** -0.5),
        "ssd_conv_w": nrm(ks[8], (DEPTH, SSD_CONV, SSD_XBC), SSD_CONV ** -0.5),
        "ssd_conv_b": nrm(ks[9], (DEPTH, SSD_XBC), 0.02),
        "ssd_dt_bias": dt0 + jnp.log(-jnp.expm1(-dt0)),
        "ssd_a_log": jnp.log(jax.random.uniform(ks[11], (DEPTH, 2, SSD_HEADS), f32, 1.0, 16.0)),
        "ssd_d": 1.0 + nrm(ks[12], (DEPTH, SSD_HEADS), 0.1),
        "ssd_norm_w": 1.0 + nrm(ks[13], (DEPTH, SSD_D_INNER), 0.02),
        "na_q_norm_w": 1.0 + nrm(ks[14], (DEPTH, NA_HEAD_DIM), 0.02),
        "na_k_norm_w": 1.0 + nrm(ks[15], (DEPTH, NA_HEAD_DIM), 0.02),
        "na_rpb": nrm(ks[16], (DEPTH, NA_HEADS, 2 * NA_WIN_R - 1, 2 * NA_WIN_C - 1), 0.1),
        "pool_w": nrm(ks[17], (DEPTH, len(POOL_WINDOWS), POOL_GROUP, POOL_GROUP), POOL_GROUP ** -0.5),
        "pool_scale": 1.0 + nrm(ks[18], (DEPTH, POOL_WIDTH), 0.1),
        "w_out": nrm(ks[19], (DEPTH, D_MIX, D_MODEL), D_MIX ** -0.5),
        "norm2_w": 1.0 + nrm(ks[20], (DEPTH, D_MODEL), 0.02),
        "mlp_w_up": nrm(ks[21], (DEPTH, D_MODEL, 2 * D_FF), D_MODEL ** -0.5),
        "mlp_conv_w": nrm(ks[22], (DEPTH, MLP_CONV, 2 * D_FF), MLP_CONV ** -0.5),
        "mlp_conv_b": nrm(ks[23], (DEPTH, 2 * D_FF), 0.02),
        "mlp_w_down": nrm(ks[24], (DEPTH, D_FF, D_MODEL), D_FF ** -0.5),
    }


def reference(x, c, ctx, c_ctx, ada_w, ada_b, norm1_w, w_in, ssd_conv_w, ssd_conv_b, ssd_dt_bias, ssd_a_log,
              ssd_d, ssd_norm_w, na_q_norm_w, na_k_norm_w, na_rpb, pool_w, pool_scale, w_out, norm2_w,
              mlp_w_up, mlp_conv_w, mlp_conv_b, mlp_w_down):
    for i in range(DEPTH):
        x, ctx = hybrid_layer(x, ctx, c, c_ctx, ada_w[i], ada_b[i], norm1_w[i], w_in[i], ssd_conv_w[i],
                              ssd_conv_b[i], ssd_dt_bias[i], ssd_a_log[i], ssd_d[i], ssd_norm_w[i],
                              na_q_norm_w[i], na_k_norm_w[i], na_rpb[i], pool_w[i], pool_scale[i], w_out[i],
                              norm2_w[i], mlp_w_up[i], mlp_conv_w[i], mlp_conv_b[i], mlp_w_down[i],
                              ctx_out=(i < DEPTH - 1))
    return x
```

```python
import functools
import math

import jax
import jax.numpy as jnp
from jax import lax
from jax.experimental import pallas as pl
from jax.experimental.pallas import tpu as pltpu

F32 = jnp.float32
BF16 = jnp.bfloat16

D_MODEL = 4096
DEPTH = 2
GRID_W = 64
D_MIX = D_MODEL
SSD_D_INNER = D_MIX // 2
SSD_HEAD_DIM = 64
SSD_HEADS = SSD_D_INNER // SSD_HEAD_DIM
SSD_GROUPS = 4
SSD_HPG = SSD_HEADS // SSD_GROUPS
SSD_STATE = 128
SSD_GN = SSD_GROUPS * SSD_STATE
SSD_XBC = SSD_D_INNER + 2 * SSD_GN
SSD_CONV = 5
SSD_CHUNK = 128
NA_WIDTH = D_MIX // 4
NA_HEAD_DIM = 128
NA_HEADS = NA_WIDTH // NA_HEAD_DIM
NA_WIN_R = 8
NA_WIN_C = 16
ROPE_BASE = 10000.0
POOL_WIDTH = D_MIX - SSD_D_INNER - NA_WIDTH
POOL_WINDOWS = (2, 4, 8, 16)
POOL_GROUP = POOL_WIDTH // len(POOL_WINDOWS)
D_FF = 11008
N_MOD = 6
RMS_EPS = 1e-6
OFF_Z = 0
OFF_Q = OFF_Z + SSD_D_INNER
OFF_POOL = OFF_Q + NA_WIDTH
OFF_XBC = OFF_POOL + POOL_WIDTH
OFF_DT = OFF_XBC + SSD_XBC
OFF_K = OFF_DT + 2 * SSD_HEADS
OFF_V = OFF_K + NA_WIDTH
IN_COLS = OFF_V + NA_WIDTH

VMEM_LIMIT_BYTES = 58 * 1024 * 1024
LANE = 128
BF16_SUBLANE = 16
MOD_ROWS = 8


def _params(*sem):
    return pltpu.CompilerParams(dimension_semantics=sem, vmem_limit_bytes=VMEM_LIMIT_BYTES)


def _mod_kernel(c_ref, w_ref, b_ref, o_ref):
    c = c_ref[...]
    s = (c * jax.nn.sigmoid(c)).astype(BF16)
    o_ref[0] = jnp.dot(s, w_ref[0].astype(BF16), preferred_element_type=F32) + b_ref[0]


def _modulation(c_rows, ada_w, ada_b, tn=512):
    depth, d, n = ada_w.shape
    return pl.pallas_call(
        _mod_kernel,
        grid=(depth, n // tn),
        in_specs=[pl.BlockSpec((MOD_ROWS, d), lambda l, j: (0, 0)),
                  pl.BlockSpec((1, d, tn), lambda l, j: (l, 0, j)),
                  pl.BlockSpec((1, 1, tn), lambda l, j: (l, 0, j))],
        out_specs=pl.BlockSpec((1, MOD_ROWS, tn), lambda l, j: (l, 0, j)),
        out_shape=jax.ShapeDtypeStruct((depth, MOD_ROWS, n), F32),
        compiler_params=_params("arbitrary", "arbitrary"),
        name="modulation",
    )(c_rows, ada_w, ada_b.reshape(depth, 1, n))


def _norm_kernel(x_ref, w_ref, sc_ref, sh_ref, o_ref):
    x = x_ref[...]
    y = x * lax.rsqrt(jnp.mean(x * x, axis=-1, keepdims=True) + RMS_EPS) * w_ref[...]
    o_ref[...] = (y * (1.0 + sc_ref[...]) + sh_ref[...]).astype(o_ref.dtype)


def _norm_mod(x, w, scale, shift, tm=512):
    m, d = x.shape
    tm = min(tm, m)
    row = pl.BlockSpec((1, d), lambda i: (0, 0))
    return pl.pallas_call(
        _norm_kernel,
        grid=(m // tm,),
        in_specs=[pl.BlockSpec((tm, d), lambda i: (i, 0)), row, row, row],
        out_specs=pl.BlockSpec((tm, d), lambda i: (i, 0)),
        out_shape=jax.ShapeDtypeStruct((m, d), BF16),
        compiler_params=_params("arbitrary"),
        name="norm_mod",
    )(x, w.reshape(1, d), scale.reshape(1, d), shift.reshape(1, d))


def _mm_kernel(a_ref, b_ref, o_ref):
    o_ref[...] = jnp.dot(a_ref[...], b_ref[...], preferred_element_type=F32).astype(o_ref.dtype)


def _matmul(a, b, out_dtype, tm=1024, tn=1024):
    m, k = a.shape
    n = b.shape[1]
    tm, tn = min(tm, m), min(tn, n)
    return pl.pallas_call(
        _mm_kernel,
        grid=(m // tm, n // tn),
        in_specs=[pl.BlockSpec((tm, k), lambda i, j: (i, 0)),
                  pl.BlockSpec((k, tn), lambda i, j: (0, j))],
        out_specs=pl.BlockSpec((tm, tn), lambda i, j: (i, j)),
        out_shape=jax.ShapeDtypeStruct((m, n), out_dtype),
        compiler_params=_params("arbitrary", "arbitrary"),
        name="matmul",
    )(a, b)


def _outproj_kernel(a_ref, w_ref, x_ref, g_ref, o_ref):
    acc = jnp.dot(a_ref[...], w_ref[...], preferred_element_type=F32)
    o_ref[...] = x_ref[...] + g_ref[...] * acc


def _outproj(a, w, x, gate, tm=1024, tn=1024):
    m, k = a.shape
    n = w.shape[1]
    tm = min(tm, m)
    return pl.pallas_call(
        _outproj_kernel,
        grid=(m // tm, n // tn),
        in_specs=[pl.BlockSpec((tm, k), lambda i, j: (i, 0)),
                  pl.BlockSpec((k, tn), lambda i, j: (0, j)),
                  pl.BlockSpec((tm, tn), lambda i, j: (i, j)),
                  pl.BlockSpec((1, tn), lambda i, j: (0, j))],
        out_specs=pl.BlockSpec((tm, tn), lambda i, j: (i, j)),
        out_shape=jax.ShapeDtypeStruct((m, n), F32),
        compiler_params=_params("arbitrary", "arbitrary"),
        name="outproj",
    )(a, w, x, gate.reshape(1, n))


FFN_HALO = BF16_SUBLANE


def _ffn_kernel(h_ref, hp_ref, hn_ref, wg_ref, wv_ref, cg_ref, cv_ref, wd_ref, x_ref, g_ref,
                o_ref, ext_ref, *, tm, n_mtiles, nf, tn_d):
    i = pl.program_id(0)
    f = pl.program_id(1)
    d = o_ref.shape[1]

    @pl.when(f == 0)
    def _():
        ext_ref[0:tm, :] = h_ref[...]
        nxt = hn_ref[...]
        prv = hp_ref[...]
        ext_ref[tm:tm + FFN_HALO, :] = jnp.where(i == n_mtiles - 1, jnp.zeros_like(nxt), nxt)
        ext_ref[tm + FFN_HALO:tm + 2 * FFN_HALO, :] = jnp.where(i == 0, jnp.zeros_like(prv), prv)
        o_ref[...] = jnp.zeros_like(o_ref)

    h = ext_ref[...]

    def conv_branch(w_ref, c_ref):
        u = jnp.dot(h, w_ref[...], preferred_element_type=F32)
        um = u[0:tm]
        row = lax.broadcasted_iota(jnp.int32, um.shape, 0)
        up = jnp.where(row == 0, u[tm + 2 * FFN_HALO - 1:tm + 2 * FFN_HALO], pltpu.roll(um, 1, 0))
        un = jnp.where(row == tm - 1, u[tm:tm + 1], pltpu.roll(um, tm - 1, 0))
        c = c_ref[...]
        return up * c[0:1] + um * c[1:2] + un * c[2:3] + c[3:4]

    gate = conv_branch(wg_ref, cg_ref)
    val = conv_branch(wv_ref, cv_ref)
    act = (gate * jax.nn.sigmoid(gate) * val).astype(BF16)
    for n in range(d // tn_d):
        sl = slice(n * tn_d, (n + 1) * tn_d)
        o_ref[:, sl] += jnp.dot(act, wd_ref[:, sl], preferred_element_type=F32)

    @pl.when(f == nf - 1)
    def _():
        o_ref[...] = x_ref[...] + g_ref[...] * o_ref[...]


def _conv_ffn(h, w_up, conv_tab, w_down, x, gate, tm=512, tf=256, tn_d=1024):
    m, d = h.shape
    ff = w_down.shape[0]
    tm = min(tm, m)
    n_mtiles, nf = m // tm, ff // tf
    hb = tm // FFN_HALO
    last_hb = m // FFN_HALO - 1
    kern = functools.partial(_ffn_kernel, tm=tm, n_mtiles=n_mtiles, nf=nf, tn_d=tn_d)
    return pl.pallas_call(
        kern,
        grid=(n_mtiles, nf),
        in_specs=[
            pl.BlockSpec((tm, d), lambda i, f: (i, 0), pipeline_mode=pl.Buffered(1)),
            pl.BlockSpec((FFN_HALO, d), lambda i, f: (jnp.maximum(i * hb - 1, 0), 0)),
            pl.BlockSpec((FFN_HALO, d), lambda i, f: (jnp.minimum((i + 1) * hb, last_hb), 0)),
            pl.BlockSpec((d, tf), lambda i, f: (0, f)),
            pl.BlockSpec((d, tf), lambda i, f: (0, f + nf)),
            pl.BlockSpec((8, tf), lambda i, f: (0, f)),
            pl.BlockSpec((8, tf), lambda i, f: (0, f + nf)),
            pl.BlockSpec((tf, d), lambda i, f: (f, 0)),
            pl.BlockSpec((tm, d), lambda i, f: (i, 0), pipeline_mode=pl.Buffered(1)),
            pl.BlockSpec((1, d), lambda i, f: (0, 0)),
        ],
        out_specs=pl.BlockSpec((tm, d), lambda i, f: (i, 0)),
        out_shape=jax.ShapeDtypeStruct((m, d), F32),
        scratch_shapes=[pltpu.VMEM((tm + 2 * FFN_HALO, d), BF16)],
        compiler_params=_params("arbitrary", "arbitrary"),
        name="conv_ffn",
    )(h, h, h, w_up, w_up, conv_tab, conv_tab, w_down, x, gate.reshape(1, d))


def _rmsnorm(x, w):
    xf = x.astype(F32)
    return xf * lax.rsqrt(jnp.mean(xf * xf, axis=-1, keepdims=True) + RMS_EPS) * w.astype(F32)


def _dwconv1d(x, w, b):
    k_w = w.shape[0]
    length = x.shape[1]
    left = (k_w - 1) // 2
    xp = jnp.pad(x, ((0, 0), (left, k_w - 1 - left), (0, 0)))
    y = b
    for k in range(k_w):
        y = y + xp[:, k:k + length] * w[k]
    return y


def _axial_rope(length):
    t = jnp.arange(length)
    half = NA_HEAD_DIM // 2
    inv = ROPE_BASE ** (-jnp.arange(0, half, 2, dtype=F32) / half)
    ar = (t // GRID_W).astype(F32)[:, None] * inv
    ac = (t % GRID_W).astype(F32)[:, None] * inv
    ang = jnp.concatenate([ar, ar, ac, ac], axis=-1)[:, None, :]
    return jnp.cos(ang), jnp.sin(ang)


def _rotate_half(v):
    v1, v2 = jnp.split(v, 2, axis=-1)
    return jnp.concatenate([-v2, v1], axis=-1)


def _apply_axial_rope(x, cos, sin):
    half = x.shape[-1] // 2
    rot = jnp.concatenate([_rotate_half(x[..., :half]), _rotate_half(x[..., half:])], axis=-1)
    return x * cos + rot * sin


def _segsum(a):
    t = a.shape[-1]
    cs = jnp.cumsum(a, axis=-1)
    diff = cs[..., :, None] - cs[..., None, :]
    return jnp.where(jnp.tril(jnp.ones((t, t), dtype=bool)), diff, -jnp.inf)


def _chunks(a):
    b, length = a.shape[:2]
    return a.reshape((b, length // SSD_CHUNK, SSD_CHUNK) + a.shape[2:])


def _chunk_end_states(xc, a_cs, bc):
    decay = jnp.exp(a_cs[..., -1:] - a_cs)
    return jnp.einsum('bclgn,bgrcl,bclgrp->bcgrpn', bc, decay, xc)


def _ssd_output(xdt, da, bm, cm, h0):
    xc, bc, cc = _chunks(xdt), _chunks(bm), _chunks(cm)
    ac = _chunks(da).transpose(0, 3, 4, 1, 2)
    a_cs = jnp.cumsum(ac, axis=-1)
    cb = jnp.einsum('bclgn,bcsgn->bgcls', cc, bc)
    y_diag = jnp.einsum('bgcls,bgrcls,bcsgrp->bclgrp', cb, jnp.exp(_segsum(ac)), xc)
    s_end = _chunk_end_states(xc[:, :-1], a_cs[..., :-1, :], bc[:, :-1])
    s_cat = jnp.concatenate([h0[:, None], s_end], axis=1)
    tot = a_cs[..., :-1, -1]
    decay_chunk = jnp.exp(_segsum(jnp.pad(tot, ((0, 0), (0, 0), (0, 0), (1, 0)))))
    h_in = jnp.einsum('bgrzc,bcgrpn->bzgrpn', decay_chunk, s_cat)
    y_off = jnp.einsum('bclgn,bcgrpn,bgrcl->bclgrp', cc, h_in, jnp.exp(a_cs))
    return (y_diag + y_off).reshape(xdt.shape)


def _ssd_final_state(xdt, da, bm, h0):
    xc, bc = _chunks(xdt), _chunks(bm)
    a_cs = jnp.cumsum(_chunks(da).transpose(0, 3, 4, 1, 2), axis=-1)
    s_end = _chunk_end_states(xc, a_cs, bc)
    cum = jnp.cumsum(a_cs[..., -1], axis=-1)
    total = cum[..., -1:]
    return jnp.exp(total)[..., None] * h0 + jnp.einsum('bgrc,bcgrpn->bgrpn', jnp.exp(total - cum), s_end)


def _ssd_prep(xbc, dt_raw, conv_w, conv_b, dt_bias, a_log):
    b, length, _ = xbc.shape
    xbc = jax.nn.silu(_dwconv1d(xbc, conv_w, conv_b))
    xs = xbc[..., :SSD_D_INNER].reshape(b, length, SSD_GROUPS, SSD_HPG, SSD_HEAD_DIM)
    bm = xbc[..., SSD_D_INNER:SSD_D_INNER + SSD_GN].reshape(b, length, SSD_GROUPS, SSD_STATE)
    cm = xbc[..., SSD_D_INNER + SSD_GN:].reshape(b, length, SSD_GROUPS, SSD_STATE)
    dt = jax.nn.softplus(dt_raw.reshape(b, length, 2, SSD_HEADS) + dt_bias)
    da = dt * -jnp.exp(a_log)
    shape = (b, length, 2, SSD_GROUPS, SSD_HPG)
    return xs, bm, cm, dt.reshape(shape), da.reshape(shape)


def _ssd_mixer(z_l, xbc_l, dt_l, z_c, xbc_c, dt_c, conv_w, conv_b, dt_bias, a_log, d_skip, norm_w):
    b, length, _ = xbc_l.shape
    xl, bl, cl, dtl, dal = _ssd_prep(xbc_l, dt_l, conv_w, conv_b, dt_bias, a_log)
    xc, bc, cc, dtc, dac = _ssd_prep(xbc_c, dt_c, conv_w, conv_b, dt_bias, a_log)
    d_sk = d_skip.reshape(SSD_GROUPS, SSD_HPG, 1)
    h_zero = jnp.zeros((b, SSD_GROUPS, SSD_HPG, SSD_HEAD_DIM, SSD_STATE), F32)
    y_l = xl * d_sk
    y_c = xc * d_sk if z_c is not None else None
    for direction in range(2):
        rev = (lambda a: a[:, ::-1]) if direction else (lambda a: a)
        xdt_c = rev(xc * dtc[:, :, direction, :, :, None])
        da_c = rev(dac[:, :, direction])
        b_c = rev(bc)
        h_ctx = _ssd_final_state(xdt_c, da_c, b_c, h_zero)
        y_l = y_l + rev(_ssd_output(rev(xl * dtl[:, :, direction, :, :, None]), rev(dal[:, :, direction]),
                                    rev(bl), rev(cl), h_ctx))
        if z_c is not None:
            y_c = y_c + rev(_ssd_output(xdt_c, da_c, b_c, rev(cc), h_zero))
    out_l = _rmsnorm(y_l.reshape(b, length, SSD_D_INNER) * jax.nn.silu(z_l), norm_w)
    out_c = None
    if z_c is not None:
        out_c = _rmsnorm(y_c.reshape(b, -1, SSD_D_INNER) * jax.nn.silu(z_c), norm_w)
    return out_l, out_c


def _neighbourhood_attention(q_l, k_l, v_l, q_c, k_c, v_c, q_norm_w, k_norm_w, rpb):
    b, length, _ = q_l.shape
    lc = k_c.shape[1]
    rows = length // GRID_W
    wr = min(NA_WIN_R, rows)
    hd = (NA_HEADS, NA_HEAD_DIM)
    scale = NA_HEAD_DIM ** -0.5
    cos, sin = _axial_rope(length)
    q = _apply_axial_rope(_rmsnorm(q_l.reshape(b, length, *hd), q_norm_w), cos, sin)
    k = _apply_axial_rope(_rmsnorm(k_l.reshape(b, length, *hd), k_norm_w), cos, sin)
    v = v_l.reshape(b, length, *hd)
    kc = _rmsnorm(k_c.reshape(b, lc, *hd), k_norm_w)
    vc = v_c.reshape(b, lc, *hd)
    r_idx = jnp.arange(rows)
    band = jnp.clip(r_idx - wr // 2, 0, rows - wr)[:, None] + jnp.arange(wr)[None, :]
    qg = q.reshape(b, rows, GRID_W, *hd)
    kg = k.reshape(b, rows, GRID_W, *hd)[:, band]
    vg = v.reshape(b, rows, GRID_W, *hd)[:, band]
    col = jnp.arange(GRID_W)
    c0 = jnp.clip(col - NA_WIN_C // 2, 0, GRID_W - NA_WIN_C)
    col_ok = (col[None, :] >= c0[:, None]) & (col[None, :] < c0[:, None] + NA_WIN_C)
    dr = band - r_idx[:, None] + (NA_WIN_R - 1)
    dc = jnp.clip(col[None, :] - col[:, None] + (NA_WIN_C - 1), 0, 2 * NA_WIN_C - 2)
    bias = rpb[:, dr[:, None, :, None], dc[None, :, None, :]]
    s_loc = jnp.einsum('brqhd,brwkhd->bhrqwk', qg, kg) * scale + bias
    s_loc = jnp.where(col_ok[:, None, :], s_loc, -jnp.inf).reshape(b, NA_HEADS, rows, GRID_W, wr * GRID_W)
    s_ctx = jnp.einsum('brqhd,bjhd->bhrqj', qg, kc) * scale
    prob = jax.nn.softmax(jnp.concatenate([s_loc, s_ctx], axis=-1), axis=-1)
    p_loc = prob[..., :wr * GRID_W].reshape(b, NA_HEADS, rows, GRID_W, wr, GRID_W)
    p_ctx = prob[..., wr * GRID_W:]
    o = jnp.einsum('bhrqwk,brwkhd->brqhd', p_loc, vg) + jnp.einsum('bhrqj,bjhd->brqhd', p_ctx, vc)
    y_l = o.reshape(b, length, NA_WIDTH)
    y_c = None
    if q_c is not None:
        qc = _rmsnorm(q_c.reshape(b, lc, *hd), q_norm_w)
        s = jnp.einsum('bihd,bjhd->bhij', qc, kc) * scale
        pc = jax.nn.softmax(s, axis=-1)
        y_c = jnp.einsum('bhij,bjhd->bihd', pc, vc).reshape(b, lc, NA_WIDTH)
    return y_l, y_c


def _multiscale_pool(u, pool_w, pool_scale):
    bsz, length, _ = u.shape
    cs = jnp.pad(jnp.cumsum(u, axis=1), ((0, 0), (1, 0), (0, 0)))
    t = jnp.arange(length)
    outs = []
    for g, w in enumerate(POOL_WINDOWS):
        lo = jnp.clip(t - w // 2, 0, length)
        hi = jnp.clip(t + w // 2, 0, length)
        seg = cs[..., g * POOL_GROUP:(g + 1) * POOL_GROUP]
        cnt = (hi - lo).astype(F32)[None, :, None]
        outs.append((seg[:, hi] - seg[:, lo]) / cnt - u[..., g * POOL_GROUP:(g + 1) * POOL_GROUP])
    pooled = jnp.stack(outs, axis=2)
    y = jnp.einsum('blgc,gcd->blgd', pooled, pool_w).reshape(bsz, length, POOL_WIDTH)
    return y * pool_scale


P_Z = 0
P_Q = P_Z + SSD_D_INNER
P_POOL = P_Q + NA_WIDTH
P_XBC = P_POOL + POOL_WIDTH
P_K = P_XBC + SSD_XBC
P_V = P_K + NA_WIDTH
P_COLS = P_V + NA_WIDTH


def _layer(x, ctx, mod_l, mod_c, norm1_w, w_in, ssd_conv_w, ssd_conv_b, ssd_dt_bias, ssd_a_log, ssd_d,
           ssd_norm_w, na_q_norm_w, na_k_norm_w, na_rpb, pool_w, pool_scale, w_out, norm2_w, mlp_w_up,
           mlp_conv_w, mlp_conv_b, mlp_w_down, ctx_out):
    sh1_l, sc1_l, g1_l, sh2_l, sc2_l, g2_l = jnp.split(mod_l, N_MOD)
    sh1_c, sc1_c, g1_c, sh2_c, sc2_c, g2_c = jnp.split(mod_c, N_MOD)

    w_main = jnp.concatenate([w_in[:, :OFF_DT], w_in[:, OFF_K:]], axis=1).astype(BF16)
    w_dt = jnp.pad(w_in[:, OFF_DT:OFF_K], ((0, 0), (0, LANE - 2 * SSD_HEADS))).astype(BF16)
    w_out_b = w_out.astype(BF16)
    w_up_b = mlp_w_up.astype(BF16)
    w_down_b = mlp_w_down.astype(BF16)
    conv_tab = jnp.concatenate([mlp_conv_w, mlp_conv_b[None], jnp.zeros((4, 2 * D_FF), F32)], axis=0)

    h_l = _norm_mod(x, norm1_w, sc1_l, sh1_l)
    h_c = _norm_mod(ctx, norm1_w, sc1_c, sh1_c)
    p_l = _matmul(h_l, w_main, BF16)
    dt_l = _matmul(h_l, w_dt, F32)[:, :2 * SSD_HEADS]
    p_c = _matmul(h_c, w_main, BF16)
    dt_c = _matmul(h_c, w_dt, F32)[:, :2 * SSD_HEADS]

    def seg(p, off, width):
        return p[None, :, off:off + width].astype(F32)

    y_ssd_l, y_ssd_c = _ssd_mixer(seg(p_l, P_Z, SSD_D_INNER), seg(p_l, P_XBC, SSD_XBC), dt_l[None],
                                  seg(p_c, P_Z, SSD_D_INNER) if ctx_out else None,
                                  seg(p_c, P_XBC, SSD_XBC), dt_c[None],
                                  ssd_conv_w, ssd_conv_b, ssd_dt_bias, ssd_a_log, ssd_d, ssd_norm_w)
    y_na_l, y_na_c = _neighbourhood_attention(seg(p_l, P_Q, NA_WIDTH), seg(p_l, P_K, NA_WIDTH),
                                              seg(p_l, P_V, NA_WIDTH),
                                              seg(p_c, P_Q, NA_WIDTH) if ctx_out else None,
                                              seg(p_c, P_K, NA_WIDTH), seg(p_c, P_V, NA_WIDTH),
                                              na_q_norm_w, na_k_norm_w, na_rpb)
    y_pool_l = _multiscale_pool(seg(p_l, P_POOL, POOL_WIDTH), pool_w, pool_scale)
    mix_l = jnp.concatenate([y_ssd_l, y_na_l, y_pool_l], axis=-1)[0].astype(BF16)

    x_mid = _outproj(mix_l, w_out_b, x, g1_l)
    h2_l = _norm_mod(x_mid, norm2_w, sc2_l, sh2_l)
    x_new = _conv_ffn(h2_l, w_up_b, conv_tab, w_down_b, x_mid, g2_l)
    ctx_new = ctx
    if ctx_out:
        y_pool_c = _multiscale_pool(seg(p_c, P_POOL, POOL_WIDTH), pool_w, pool_scale)
        mix_c = jnp.concatenate([y_ssd_c, y_na_c, y_pool_c], axis=-1)[0].astype(BF16)
        c_mid = _outproj(mix_c, w_out_b, ctx, g1_c)
        h2_c = _norm_mod(c_mid, norm2_w, sc2_c, sh2_c)
        ctx_new = _conv_ffn(h2_c, w_up_b, conv_tab, w_down_b, c_mid, g2_c)
    return x_new, ctx_new


def kernel(x, c, ctx, c_ctx, ada_w, ada_b, norm1_w, w_in, ssd_conv_w, ssd_conv_b, ssd_dt_bias, ssd_a_log,
           ssd_d, ssd_norm_w, na_q_norm_w, na_k_norm_w, na_rpb, pool_w, pool_scale, w_out, norm2_w,
           mlp_w_up, mlp_conv_w, mlp_conv_b, mlp_w_down):
    depth = ada_w.shape[0]
    d = x.shape[-1]
    c_rows = jnp.concatenate([c.reshape(1, d), c_ctx.reshape(1, d), jnp.zeros((MOD_ROWS - 2, d), F32)], axis=0)
    mods = _modulation(c_rows, ada_w, ada_b)
    xs, cs = x[0], ctx[0]
    for i in range(depth):
        xs, cs = _layer(xs, cs, mods[i, 0], mods[i, 1], norm1_w[i], w_in[i], ssd_conv_w[i], ssd_conv_b[i],
                        ssd_dt_bias[i], ssd_a_log[i], ssd_d[i], ssd_norm_w[i], na_q_norm_w[i], na_k_norm_w[i],
                        na_rpb[i], pool_w[i], pool_scale[i], w_out[i], norm2_w[i], mlp_w_up[i], mlp_conv_w[i],
                        mlp_conv_b[i], mlp_w_down[i], ctx_out=(i < depth - 1))
    return xs[None]
```

```python
import functools
import math

import jax
import jax.numpy as jnp
from jax import lax
from jax.experimental import pallas as pl
from jax.experimental.pallas import tpu as pltpu

F32 = jnp.float32
BF16 = jnp.bfloat16

D_MODEL = 4096
DEPTH = 2
GRID_W = 64
D_MIX = D_MODEL
SSD_D_INNER = D_MIX // 2
SSD_HEAD_DIM = 64
SSD_HEADS = SSD_D_INNER // SSD_HEAD_DIM
SSD_GROUPS = 4
SSD_HPG = SSD_HEADS // SSD_GROUPS
SSD_STATE = 128
SSD_GN = SSD_GROUPS * SSD_STATE
SSD_XBC = SSD_D_INNER + 2 * SSD_GN
SSD_CONV = 5
SSD_CHUNK = 128
NA_WIDTH = D_MIX // 4
NA_HEAD_DIM = 128
NA_HEADS = NA_WIDTH // NA_HEAD_DIM
NA_WIN_R = 8
NA_WIN_C = 16
ROPE_BASE = 10000.0
POOL_WIDTH = D_MIX - SSD_D_INNER - NA_WIDTH
POOL_WINDOWS = (2, 4, 8, 16)
POOL_GROUP = POOL_WIDTH // len(POOL_WINDOWS)
D_FF = 11008
N_MOD = 6
RMS_EPS = 1e-6
OFF_Z = 0
OFF_Q = OFF_Z + SSD_D_INNER
OFF_POOL = OFF_Q + NA_WIDTH
OFF_XBC = OFF_POOL + POOL_WIDTH
OFF_DT = OFF_XBC + SSD_XBC
OFF_K = OFF_DT + 2 * SSD_HEADS
OFF_V = OFF_K + NA_WIDTH
IN_COLS = OFF_V + NA_WIDTH

VMEM_LIMIT_BYTES = 58 * 1024 * 1024
LANE = 128
BF16_SUBLANE = 16
MOD_ROWS = 8


def _params(*sem):
    return pltpu.CompilerParams(dimension_semantics=sem, vmem_limit_bytes=VMEM_LIMIT_BYTES)


def _mod_kernel(c_ref, w_ref, b_ref, o_ref):
    c = c_ref[...]
    s = (c * jax.nn.sigmoid(c)).astype(BF16)
    o_ref[0] = jnp.dot(s, w_ref[0].astype(BF16), preferred_element_type=F32) + b_ref[0]


def _modulation(c_rows, ada_w, ada_b, tn=512):
    depth, d, n = ada_w.shape
    return pl.pallas_call(
        _mod_kernel,
        grid=(depth, n // tn),
        in_specs=[pl.BlockSpec((MOD_ROWS, d), lambda l, j: (0, 0)),
                  pl.BlockSpec((1, d, tn), lambda l, j: (l, 0, j)),
                  pl.BlockSpec((1, 1, tn), lambda l, j: (l, 0, j))],
        out_specs=pl.BlockSpec((1, MOD_ROWS, tn), lambda l, j: (l, 0, j)),
        out_shape=jax.ShapeDtypeStruct((depth, MOD_ROWS, n), F32),
        compiler_params=_params("arbitrary", "arbitrary"),
        name="modulation",
    )(c_rows, ada_w, ada_b.reshape(depth, 1, n))


def _norm_kernel(x_ref, w_ref, sc_ref, sh_ref, o_ref):
    x = x_ref[...]
    y = x * lax.rsqrt(jnp.mean(x * x, axis=-1, keepdims=True) + RMS_EPS) * w_ref[...]
    o_ref[...] = (y * (1.0 + sc_ref[...]) + sh_ref[...]).astype(o_ref.dtype)


def _norm_mod(x, w, scale, shift, tm=512):
    m, d = x.shape
    tm = min(tm, m)
    row = pl.BlockSpec((1, d), lambda i: (0, 0))
    return pl.pallas_call(
        _norm_kernel,
        grid=(m // tm,),
        in_specs=[pl.BlockSpec((tm, d), lambda i: (i, 0)), row, row, row],
        out_specs=pl.BlockSpec((tm, d), lambda i: (i, 0)),
        out_shape=jax.ShapeDtypeStruct((m, d), BF16),
        compiler_params=_params("arbitrary"),
        name="norm_mod",
    )(x, w.reshape(1, d), scale.reshape(1, d), shift.reshape(1, d))


def _mm_kernel(a_ref, b_ref, o_ref):
    o_ref[...] = jnp.dot(a_ref[...], b_ref[...], preferred_element_type=F32).astype(o_ref.dtype)


def _matmul(a, b, out_dtype, tm=1024, tn=1024):
    m, k = a.shape
    n = b.shape[1]
    tm, tn = min(tm, m), min(tn, n)
    return pl.pallas_call(
        _mm_kernel,
        grid=(m // tm, n // tn),
        in_specs=[pl.BlockSpec((tm, k), lambda i, j: (i, 0)),
                  pl.BlockSpec((k, tn), lambda i, j: (0, j))],
        out_specs=pl.BlockSpec((tm, tn), lambda i, j: (i, j)),
        out_shape=jax.ShapeDtypeStruct((m, n), out_dtype),
        compiler_params=_params("arbitrary", "arbitrary"),
        name="matmul",
    )(a, b)


def _outproj_kernel(*refs, widths):
    a_refs, (w_ref, x_ref, g_ref, o_ref) = refs[:len(widths)], refs[len(widths):]
    acc, off = None, 0
    for a_ref, k in zip(a_refs, widths):
        part = jnp.dot(a_ref[...], w_ref[off:off + k, :], preferred_element_type=F32)
        acc = part if acc is None else acc + part
        off += k
    o_ref[...] = x_ref[...] + g_ref[...] * acc


def _outproj(parts, w, x, gate, tm=1024, tn=1024):
    m = x.shape[0]
    k, n = w.shape
    tm = min(tm, m)
    widths = tuple(a.shape[1] for a in parts)
    assert sum(widths) == k
    return pl.pallas_call(
        functools.partial(_outproj_kernel, widths=widths),
        grid=(m // tm, n // tn),
        in_specs=[pl.BlockSpec((tm, kw), lambda i, j: (i, 0)) for kw in widths] + [
            pl.BlockSpec((k, tn), lambda i, j: (0, j)),
            pl.BlockSpec((tm, tn), lambda i, j: (i, j)),
            pl.BlockSpec((1, tn), lambda i, j: (0, j))],
        out_specs=pl.BlockSpec((tm, tn), lambda i, j: (i, j)),
        out_shape=jax.ShapeDtypeStruct((m, n), F32),
        compiler_params=_params("arbitrary", "arbitrary"),
        name="outproj",
    )(*parts, w, x, gate.reshape(1, n))


FFN_HALO = BF16_SUBLANE


def _ffn_kernel(h_ref, hp_ref, hn_ref, wg_ref, wv_ref, cg_ref, cv_ref, wd_ref, x_ref, g_ref,
                o_ref, ext_ref, *, tm, n_mtiles, nf, tn_d):
    i = pl.program_id(0)
    f = pl.program_id(1)
    d = o_ref.shape[1]

    @pl.when(f == 0)
    def _():
        ext_ref[0:tm, :] = h_ref[...]
        nxt = hn_ref[...]
        prv = hp_ref[...]
        ext_ref[tm:tm + FFN_HALO, :] = jnp.where(i == n_mtiles - 1, jnp.zeros_like(nxt), nxt)
        ext_ref[tm + FFN_HALO:tm + 2 * FFN_HALO, :] = jnp.where(i == 0, jnp.zeros_like(prv), prv)
        o_ref[...] = jnp.zeros_like(o_ref)

    h = ext_ref[...]

    def conv_branch(w_ref, c_ref):
        u = jnp.dot(h, w_ref[...], preferred_element_type=F32)
        um = u[0:tm]
        row = lax.broadcasted_iota(jnp.int32, um.shape, 0)
        up = jnp.where(row == 0, u[tm + 2 * FFN_HALO - 1:tm + 2 * FFN_HALO], pltpu.roll(um, 1, 0))
        un = jnp.where(row == tm - 1, u[tm:tm + 1], pltpu.roll(um, tm - 1, 0))
        c = c_ref[...]
        return up * c[0:1] + um * c[1:2] + un * c[2:3] + c[3:4]

    gate = conv_branch(wg_ref, cg_ref)
    val = conv_branch(wv_ref, cv_ref)
    act = (gate * jax.nn.sigmoid(gate) * val).astype(BF16)
    for n in range(d // tn_d):
        sl = slice(n * tn_d, (n + 1) * tn_d)
        o_ref[:, sl] += jnp.dot(act, wd_ref[:, sl], preferred_element_type=F32)

    @pl.when(f == nf - 1)
    def _():
        o_ref[...] = x_ref[...] + g_ref[...] * o_ref[...]


def _conv_ffn(h, w_up, conv_tab, w_down, x, gate, tm=512, tf=256, tn_d=1024):
    m, d = h.shape
    ff = w_down.shape[0]
    tm = min(tm, m)
    n_mtiles, nf = m // tm, ff // tf
    hb = tm // FFN_HALO
    last_hb = m // FFN_HALO - 1
    kern = functools.partial(_ffn_kernel, tm=tm, n_mtiles=n_mtiles, nf=nf, tn_d=tn_d)
    return pl.pallas_call(
        kern,
        grid=(n_mtiles, nf),
        in_specs=[
            pl.BlockSpec((tm, d), lambda i, f: (i, 0), pipeline_mode=pl.Buffered(1)),
            pl.BlockSpec((FFN_HALO, d), lambda i, f: (jnp.maximum(i * hb - 1, 0), 0)),
            pl.BlockSpec((FFN_HALO, d), lambda i, f: (jnp.minimum((i + 1) * hb, last_hb), 0)),
            pl.BlockSpec((d, tf), lambda i, f: (0, f)),
            pl.BlockSpec((d, tf), lambda i, f: (0, f + nf)),
            pl.BlockSpec((8, tf), lambda i, f: (0, f)),
            pl.BlockSpec((8, tf), lambda i, f: (0, f + nf)),
            pl.BlockSpec((tf, d), lambda i, f: (f, 0)),
            pl.BlockSpec((tm, d), lambda i, f: (i, 0), pipeline_mode=pl.Buffered(1)),
            pl.BlockSpec((1, d), lambda i, f: (0, 0)),
        ],
        out_specs=pl.BlockSpec((tm, d), lambda i, f: (i, 0)),
        out_shape=jax.ShapeDtypeStruct((m, d), F32),
        scratch_shapes=[pltpu.VMEM((tm + 2 * FFN_HALO, d), BF16)],
        compiler_params=_params("arbitrary", "arbitrary"),
        name="conv_ffn",
    )(h, h, h, w_up, w_up, conv_tab, conv_tab, w_down, x, gate.reshape(1, d))


def _rope_tables(length):
    t = jnp.arange(length)
    half = NA_HEAD_DIM // 2
    inv = ROPE_BASE ** (-jnp.arange(0, half, 2, dtype=F32) / half)
    ar = (t // GRID_W).astype(F32)[:, None] * inv
    ac = (t % GRID_W).astype(F32)[:, None] * inv
    ang = jnp.concatenate([ar, ar, ac, ac], axis=-1)
    cos, sin = jnp.cos(ang), jnp.sin(ang)
    first = (jnp.arange(NA_HEAD_DIM) % half) < half // 2
    return cos, jnp.where(first, -sin, 0.0), jnp.where(first, 0.0, sin)


def _qk_prep_kernel(q_ref, k_ref, qw_ref, kw_ref, cos_ref, sa_ref, sb_ref, qo_ref, ko_ref):
    cos, sa, sb = cos_ref[...], sa_ref[...], sb_ref[...]
    quarter = NA_HEAD_DIM // 4
    for h in range(NA_HEADS):
        sl = slice(h * NA_HEAD_DIM, (h + 1) * NA_HEAD_DIM)
        for src, w_ref, dst in ((q_ref, qw_ref, qo_ref), (k_ref, kw_ref, ko_ref)):
            x = src[:, sl].astype(F32)
            y = x * lax.rsqrt(jnp.mean(x * x, axis=-1, keepdims=True) + RMS_EPS) * w_ref[...]
            y = (y * cos + pltpu.roll(y, NA_HEAD_DIM - quarter, 1) * sa + pltpu.roll(y, quarter, 1) * sb)
            dst[:, sl] = y.astype(dst.dtype)


def _qk_prep(p, q_blk, k_blk, qw, kw, tables, tm=512):
    m = p.shape[0]
    tm = min(tm, m)
    tab = pl.BlockSpec((tm, NA_HEAD_DIM), lambda i: (i, 0))
    vec = pl.BlockSpec((1, NA_HEAD_DIM), lambda i: (0, 0))
    out = jax.ShapeDtypeStruct((m, NA_WIDTH), BF16)
    return pl.pallas_call(
        _qk_prep_kernel,
        grid=(m // tm,),
        in_specs=[pl.BlockSpec((tm, NA_WIDTH), lambda i: (i, q_blk)),
                  pl.BlockSpec((tm, NA_WIDTH), lambda i: (i, k_blk)), vec, vec, tab, tab, tab],
        out_specs=[pl.BlockSpec((tm, NA_WIDTH), lambda i: (i, 0))] * 2,
        out_shape=[out, out],
        compiler_params=_params("arbitrary"),
        name="qk_prep",
    )(p, p, qw.reshape(1, -1), kw.reshape(1, -1), *tables)


NA_DR = 2 * NA_WIN_R
NA_BIAS_CHUNK = 1024


def _na_bias_kernel(r0_ref, r1_ref, o_ref):
    def pieces(r):
        out = []
        for _ in range(3):
            piece = r.astype(BF16)
            out.append(piece)
            r = r - piece.astype(F32)
        return out

    p0, p1 = pieces(r0_ref[0]), pieces(r1_ref[0])
    for c in range(GRID_W * LANE // NA_BIAS_CHUNK):
        shape = (LANE, NA_BIAS_CHUNK)
        pos = c * NA_BIAS_CHUNK + lax.broadcasted_iota(jnp.int32, shape, 1)
        j = lax.broadcasted_iota(jnp.int32, shape, 0)
        qc = pos // LANE
        lane = pos % LANE
        kc = lane % GRID_W
        idx = jnp.clip(kc - qc + (NA_WIN_C - 1), 0, 2 * NA_WIN_C - 2)
        hit = idx == j
        second = lane >= GRID_W
        oh0 = jnp.where(hit, jnp.where(second, 0.0, 1.0), 0.0).astype(BF16)
        oh1 = jnp.where(hit, jnp.where(second, 1.0, 0.0), 0.0).astype(BF16)
        acc = jnp.zeros((NA_DR, NA_BIAS_CHUNK), F32)
        for a, b in zip(p0, p1):
            acc = acc + jnp.dot(a, oh0, preferred_element_type=F32) + jnp.dot(b, oh1, preferred_element_type=F32)
        c0 = jnp.clip(qc[0:1] - NA_WIN_C // 2, 0, GRID_W - NA_WIN_C)
        ok = (kc[0:1] >= c0) & (kc[0:1] < c0 + NA_WIN_C)
        o_ref[0, :, c * NA_BIAS_CHUNK:(c + 1) * NA_BIAS_CHUNK] = jnp.where(ok, acc, -jnp.inf)


def _na_bias_table(rpb):
    nh, ndr, ndc = rpb.shape
    r0 = jnp.pad(rpb, ((0, 0), (0, NA_DR - ndr), (0, LANE - ndc)))
    r1 = jnp.pad(rpb[:, 1:], ((0, 0), (0, NA_DR - ndr + 1), (0, LANE - ndc)))
    spec = pl.BlockSpec((1, NA_DR, LANE), lambda h: (h, 0, 0))
    out = pl.pallas_call(
        _na_bias_kernel,
        grid=(nh,),
        in_specs=[spec, spec],
        out_specs=pl.BlockSpec((1, NA_DR, GRID_W * LANE), lambda h: (h, 0, 0)),
        out_shape=jax.ShapeDtypeStruct((nh, NA_DR, GRID_W * LANE), F32),
        compiler_params=_params("arbitrary"),
        name="na_bias",
    )(r0, r1)
    return out.reshape(nh, NA_DR, GRID_W, LANE)


def _softmax_pv(scores, values):
    m = functools.reduce(jnp.maximum, [jnp.max(s, axis=1, keepdims=True) for s in scores])
    ps = [jnp.exp(s - m) for s in scores]
    den = functools.reduce(jnp.add, [jnp.sum(p, axis=1, keepdims=True) for p in ps])
    o = functools.reduce(jnp.add, [jnp.dot(p.astype(BF16), v, preferred_element_type=F32)
                                   for p, v in zip(ps, values)])
    return o / den


_NT = (((1,), (1,)), ((), ()))


def _na_kernel(q_ref, k_ref, v_ref, kc_ref, vc_ref, tz_ref, o_ref, *, rows_per_step, n_rows):
    t = pl.program_id(1)
    scale = NA_HEAD_DIM ** -0.5
    band = NA_WIN_R * GRID_W
    kc, vc = kc_ref[...], vc_ref[...]
    for i in range(rows_per_step):
        qr = t * rows_per_step + i
        bs = jnp.clip(qr - NA_WIN_R // 2, 0, n_rows - NA_WIN_R)
        off = bs - qr + (NA_WIN_R - 1)
        start = pl.multiple_of(bs * GRID_W, GRID_W)
        q = q_ref[i * GRID_W:(i + 1) * GRID_W, :]
        kb = k_ref[pl.ds(start, band), :]
        vb = v_ref[pl.ds(start, band), :]
        bias = jnp.concatenate([tz_ref[0, off + 2 * w] for w in range(NA_WIN_R // 2)], axis=1)
        s_loc = lax.dot_general(q, kb, _NT, preferred_element_type=F32) * scale + bias
        s_ctx = lax.dot_general(q, kc, _NT, preferred_element_type=F32) * scale
        o = _softmax_pv([s_loc, s_ctx], [vb, vc])
        o_ref[i * GRID_W:(i + 1) * GRID_W, :] = o.astype(o_ref.dtype)


def _na_latent(q_r, k_r, p_l, v_blk, kc_r, p_c, vc_blk, tz, rows_per_step=4):
    length = q_r.shape[0]
    lc = kc_r.shape[0]
    n_rows = length // GRID_W
    assert n_rows >= NA_WIN_R and n_rows % rows_per_step == 0
    hd = NA_HEAD_DIM
    kern = functools.partial(_na_kernel, rows_per_step=rows_per_step, n_rows=n_rows)
    return pl.pallas_call(
        kern,
        grid=(NA_HEADS, n_rows // rows_per_step),
        in_specs=[pl.BlockSpec((rows_per_step * GRID_W, hd), lambda h, t: (t, h)),
                  pl.BlockSpec((length, hd), lambda h, t: (0, h)),
                  pl.BlockSpec((length, hd), lambda h, t: (0, v_blk + h)),
                  pl.BlockSpec((lc, hd), lambda h, t: (0, h)),
                  pl.BlockSpec((lc, hd), lambda h, t: (0, vc_blk + h)),
                  pl.BlockSpec((1, NA_DR, GRID_W, LANE), lambda h, t: (h, 0, 0, 0))],
        out_specs=pl.BlockSpec((rows_per_step * GRID_W, hd), lambda h, t: (t, h)),
        out_shape=jax.ShapeDtypeStruct((length, NA_WIDTH), BF16),
        compiler_params=_params("arbitrary", "arbitrary"),
        name="na_latent",
    )(q_r, k_r, p_l, kc_r, p_c, tz)


def _ctx_attn_kernel(q_ref, k_ref, v_ref, o_ref):
    s = lax.dot_general(q_ref[...], k_ref[...], _NT, preferred_element_type=F32) * (NA_HEAD_DIM ** -0.5)
    o_ref[...] = _softmax_pv([s], [v_ref[...]]).astype(o_ref.dtype)


def _na_context(qc_r, kc_r, p_c, vc_blk):
    lc = qc_r.shape[0]
    hd = NA_HEAD_DIM
    spec = pl.BlockSpec((lc, hd), lambda h: (0, h))
    return pl.pallas_call(
        _ctx_attn_kernel,
        grid=(NA_HEADS,),
        in_specs=[spec, spec, pl.BlockSpec((lc, hd), lambda h: (0, vc_blk + h))],
        out_specs=spec,
        out_shape=jax.ShapeDtypeStruct((lc, NA_WIDTH), BF16),
        compiler_params=_params("arbitrary"),
        name="na_context",
    )(qc_r, kc_r, p_c)


POOL_HALO = BF16_SUBLANE


def _pool_kernel(u_ref, up_ref, un_ref, w_ref, sc_ref, o_ref, ext_ref, *, tm, n_tiles, length):
    i = pl.program_id(0)
    prv, nxt = up_ref[...], un_ref[...]
    ext_ref[0:POOL_HALO, :] = jnp.where(i == 0, jnp.zeros_like(prv), prv).astype(F32)
    ext_ref[POOL_HALO:POOL_HALO + tm, :] = u_ref[...].astype(F32)
    ext_ref[POOL_HALO + tm:POOL_HALO + tm + POOL_HALO, :] = jnp.where(i == n_tiles - 1, jnp.zeros_like(nxt),
                                                                     nxt).astype(F32)
    t = i * tm + lax.broadcasted_iota(jnp.int32, (tm, 1), 0)
    for g, w in enumerate(POOL_WINDOWS):
        cs = slice(g * POOL_GROUP, (g + 1) * POOL_GROUP)
        acc = ext_ref[pl.ds(POOL_HALO - w // 2, tm), cs]
        for k in range(1, w):
            acc = acc + ext_ref[pl.ds(POOL_HALO - w // 2 + k, tm), cs]
        cnt = (jnp.minimum(t + w // 2, length) - jnp.maximum(t - w // 2, 0)).astype(F32)
        pooled = acc / cnt - ext_ref[pl.ds(POOL_HALO, tm), cs]
        y = jnp.dot(pooled.astype(BF16), w_ref[g], preferred_element_type=F32) * sc_ref[:, cs]
        o_ref[:, cs] = y.astype(o_ref.dtype)


def _pool(p, u_blk, pool_w, pool_scale, tm=512):
    length = p.shape[0]
    tm = min(tm, length)
    n_tiles = length // tm
    hb = tm // POOL_HALO
    last_hb = length // POOL_HALO - 1
    kern = functools.partial(_pool_kernel, tm=tm, n_tiles=n_tiles, length=length)
    return pl.pallas_call(
        kern,
        grid=(n_tiles,),
        in_specs=[pl.BlockSpec((tm, POOL_WIDTH), lambda i: (i, u_blk)),
                  pl.BlockSpec((POOL_HALO, POOL_WIDTH), lambda i: (jnp.maximum(i * hb - 1, 0), u_blk)),
                  pl.BlockSpec((POOL_HALO, POOL_WIDTH), lambda i: (jnp.minimum((i + 1) * hb, last_hb), u_blk)),
                  pl.BlockSpec((len(POOL_WINDOWS), POOL_GROUP, POOL_GROUP), lambda i: (0, 0, 0)),
                  pl.BlockSpec((1, POOL_WIDTH), lambda i: (0, 0))],
        out_specs=pl.BlockSpec((tm, POOL_WIDTH), lambda i: (i, 0)),
        out_shape=jax.ShapeDtypeStruct((length, POOL_WIDTH), BF16),
        scratch_shapes=[pltpu.VMEM((tm + 2 * POOL_HALO, POOL_WIDTH), F32)],
        compiler_params=_params("arbitrary"),
        name="pool",
    )(p, p, p, pool_w.astype(BF16), pool_scale.reshape(1, POOL_WIDTH))


def _rmsnorm(x, w):
    xf = x.astype(F32)
    return xf * lax.rsqrt(jnp.mean(xf * xf, axis=-1, keepdims=True) + RMS_EPS) * w.astype(F32)


def _dwconv1d(x, w, b):
    k_w = w.shape[0]
    length = x.shape[1]
    left = (k_w - 1) // 2
    xp = jnp.pad(x, ((0, 0), (left, k_w - 1 - left), (0, 0)))
    y = b
    for k in range(k_w):
        y = y + xp[:, k:k + length] * w[k]
    return y


def _axial_rope(length):
    t = jnp.arange(length)
    half = NA_HEAD_DIM // 2
    inv = ROPE_BASE ** (-jnp.arange(0, half, 2, dtype=F32) / half)
    ar = (t // GRID_W).astype(F32)[:, None] * inv
    ac = (t % GRID_W).astype(F32)[:, None] * inv
    ang = jnp.concatenate([ar, ar, ac, ac], axis=-1)[:, None, :]
    return jnp.cos(ang), jnp.sin(ang)


def _rotate_half(v):
    v1, v2 = jnp.split(v, 2, axis=-1)
    return jnp.concatenate([-v2, v1], axis=-1)


def _apply_axial_rope(x, cos, sin):
    half = x.shape[-1] // 2
    rot = jnp.concatenate([_rotate_half(x[..., :half]), _rotate_half(x[..., half:])], axis=-1)
    return x * cos + rot * sin


def _segsum(a):
    t = a.shape[-1]
    cs = jnp.cumsum(a, axis=-1)
    diff = cs[..., :, None] - cs[..., None, :]
    return jnp.where(jnp.tril(jnp.ones((t, t), dtype=bool)), diff, -jnp.inf)


def _chunks(a):
    b, length = a.shape[:2]
    return a.reshape((b, length // SSD_CHUNK, SSD_CHUNK) + a.shape[2:])


def _chunk_end_states(xc, a_cs, bc):
    decay = jnp.exp(a_cs[..., -1:] - a_cs)
    return jnp.einsum('bclgn,bgrcl,bclgrp->bcgrpn', bc, decay, xc)


def _ssd_output(xdt, da, bm, cm, h0):
    xc, bc, cc = _chunks(xdt), _chunks(bm), _chunks(cm)
    ac = _chunks(da).transpose(0, 3, 4, 1, 2)
    a_cs = jnp.cumsum(ac, axis=-1)
    cb = jnp.einsum('bclgn,bcsgn->bgcls', cc, bc)
    y_diag = jnp.einsum('bgcls,bgrcls,bcsgrp->bclgrp', cb, jnp.exp(_segsum(ac)), xc)
    s_end = _chunk_end_states(xc[:, :-1], a_cs[..., :-1, :], bc[:, :-1])
    s_cat = jnp.concatenate([h0[:, None], s_end], axis=1)
    tot = a_cs[..., :-1, -1]
    decay_chunk = jnp.exp(_segsum(jnp.pad(tot, ((0, 0), (0, 0), (0, 0), (1, 0)))))
    h_in = jnp.einsum('bgrzc,bcgrpn->bzgrpn', decay_chunk, s_cat)
    y_off = jnp.einsum('bclgn,bcgrpn,bgrcl->bclgrp', cc, h_in, jnp.exp(a_cs))
    return (y_diag + y_off).reshape(xdt.shape)


def _ssd_final_state(xdt, da, bm, h0):
    xc, bc = _chunks(xdt), _chunks(bm)
    a_cs = jnp.cumsum(_chunks(da).transpose(0, 3, 4, 1, 2), axis=-1)
    s_end = _chunk_end_states(xc, a_cs, bc)
    cum = jnp.cumsum(a_cs[..., -1], axis=-1)
    total = cum[..., -1:]
    return jnp.exp(total)[..., None] * h0 + jnp.einsum('bgrc,bcgrpn->bgrpn', jnp.exp(total - cum), s_end)


def _ssd_prep(xbc, dt_raw, conv_w, conv_b, dt_bias, a_log):
    b, length, _ = xbc.shape
    xbc = jax.nn.silu(_dwconv1d(xbc, conv_w, conv_b))
    xs = xbc[..., :SSD_D_INNER].reshape(b, length, SSD_GROUPS, SSD_HPG, SSD_HEAD_DIM)
    bm = xbc[..., SSD_D_INNER:SSD_D_INNER + SSD_GN].reshape(b, length, SSD_GROUPS, SSD_STATE)
    cm = xbc[..., SSD_D_INNER + SSD_GN:].reshape(b, length, SSD_GROUPS, SSD_STATE)
    dt = jax.nn.softplus(dt_raw.reshape(b, length, 2, SSD_HEADS) + dt_bias)
    da = dt * -jnp.exp(a_log)
    shape = (b, length, 2, SSD_GROUPS, SSD_HPG)
    return xs, bm, cm, dt.reshape(shape), da.reshape(shape)


def _ssd_mixer(z_l, xbc_l, dt_l, z_c, xbc_c, dt_c, conv_w, conv_b, dt_bias, a_log, d_skip, norm_w):
    b, length, _ = xbc_l.shape
    xl, bl, cl, dtl, dal = _ssd_prep(xbc_l, dt_l, conv_w, conv_b, dt_bias, a_log)
    xc, bc, cc, dtc, dac = _ssd_prep(xbc_c, dt_c, conv_w, conv_b, dt_bias, a_log)
    d_sk = d_skip.reshape(SSD_GROUPS, SSD_HPG, 1)
    h_zero = jnp.zeros((b, SSD_GROUPS, SSD_HPG, SSD_HEAD_DIM, SSD_STATE), F32)
    y_l = xl * d_sk
    y_c = xc * d_sk if z_c is not None else None
    for direction in range(2):
        rev = (lambda a: a[:, ::-1]) if direction else (lambda a: a)
        xdt_c = rev(xc * dtc[:, :, direction, :, :, None])
        da_c = rev(dac[:, :, direction])
        b_c = rev(bc)
        h_ctx = _ssd_final_state(xdt_c, da_c, b_c, h_zero)
        y_l = y_l + rev(_ssd_output(rev(xl * dtl[:, :, direction, :, :, None]), rev(dal[:, :, direction]),
                                    rev(bl), rev(cl), h_ctx))
        if z_c is not None:
            y_c = y_c + rev(_ssd_output(xdt_c, da_c, b_c, rev(cc), h_zero))
    out_l = _rmsnorm(y_l.reshape(b, length, SSD_D_INNER) * jax.nn.silu(z_l), norm_w)
    out_c = None
    if z_c is not None:
        out_c = _rmsnorm(y_c.reshape(b, -1, SSD_D_INNER) * jax.nn.silu(z_c), norm_w)
    return out_l, out_c


def _neighbourhood_attention(q_l, k_l, v_l, q_c, k_c, v_c, q_norm_w, k_norm_w, rpb):
    b, length, _ = q_l.shape
    lc = k_c.shape[1]
    rows = length // GRID_W
    wr = min(NA_WIN_R, rows)
    hd = (NA_HEADS, NA_HEAD_DIM)
    scale = NA_HEAD_DIM ** -0.5
    cos, sin = _axial_rope(length)
    q = _apply_axial_rope(_rmsnorm(q_l.reshape(b, length, *hd), q_norm_w), cos, sin)
    k = _apply_axial_rope(_rmsnorm(k_l.reshape(b, length, *hd), k_norm_w), cos, sin)
    v = v_l.reshape(b, length, *hd)
    kc = _rmsnorm(k_c.reshape(b, lc, *hd), k_norm_w)
    vc = v_c.reshape(b, lc, *hd)
    r_idx = jnp.arange(rows)
    band = jnp.clip(r_idx - wr // 2, 0, rows - wr)[:, None] + jnp.arange(wr)[None, :]
    qg = q.reshape(b, rows, GRID_W, *hd)
    kg = k.reshape(b, rows, GRID_W, *hd)[:, band]
    vg = v.reshape(b, rows, GRID_W, *hd)[:, band]
    col = jnp.arange(GRID_W)
    c0 = jnp.clip(col - NA_WIN_C // 2, 0, GRID_W - NA_WIN_C)
    col_ok = (col[None, :] >= c0[:, None]) & (col[None, :] < c0[:, None] + NA_WIN_C)
    dr = band - r_idx[:, None] + (NA_WIN_R - 1)
    dc = jnp.clip(col[None, :] - col[:, None] + (NA_WIN_C - 1), 0, 2 * NA_WIN_C - 2)
    bias = rpb[:, dr[:, None, :, None], dc[None, :, None, :]]
    s_loc = jnp.einsum('brqhd,brwkhd->bhrqwk', qg, kg) * scale + bias
    s_loc = jnp.where(col_ok[:, None, :], s_loc, -jnp.inf).reshape(b, NA_HEADS, rows, GRID_W, wr * GRID_W)
    s_ctx = jnp.einsum('brqhd,bjhd->bhrqj', qg, kc) * scale
    prob = jax.nn.softmax(jnp.concatenate([s_loc, s_ctx], axis=-1), axis=-1)
    p_loc = prob[..., :wr * GRID_W].reshape(b, NA_HEADS, rows, GRID_W, wr, GRID_W)
    p_ctx = prob[..., wr * GRID_W:]
    o = jnp.einsum('bhrqwk,brwkhd->brqhd', p_loc, vg) + jnp.einsum('bhrqj,bjhd->brqhd', p_ctx, vc)
    y_l = o.reshape(b, length, NA_WIDTH)
    y_c = None
    if q_c is not None:
        qc = _rmsnorm(q_c.reshape(b, lc, *hd), q_norm_w)
        s = jnp.einsum('bihd,bjhd->bhij', qc, kc) * scale
        pc = jax.nn.softmax(s, axis=-1)
        y_c = jnp.einsum('bhij,bjhd->bihd', pc, vc).reshape(b, lc, NA_WIDTH)
    return y_l, y_c


def _multiscale_pool(u, pool_w, pool_scale):
    bsz, length, _ = u.shape
    cs = jnp.pad(jnp.cumsum(u, axis=1), ((0, 0), (1, 0), (0, 0)))
    t = jnp.arange(length)
    outs = []
    for g, w in enumerate(POOL_WINDOWS):
        lo = jnp.clip(t - w // 2, 0, length)
        hi = jnp.clip(t + w // 2, 0, length)
        seg = cs[..., g * POOL_GROUP:(g + 1) * POOL_GROUP]
        cnt = (hi - lo).astype(F32)[None, :, None]
        outs.append((seg[:, hi] - seg[:, lo]) / cnt - u[..., g * POOL_GROUP:(g + 1) * POOL_GROUP])
    pooled = jnp.stack(outs, axis=2)
    y = jnp.einsum('blgc,gcd->blgd', pooled, pool_w).reshape(bsz, length, POOL_WIDTH)
    return y * pool_scale


P_Z = 0
P_Q = P_Z + SSD_D_INNER
P_POOL = P_Q + NA_WIDTH
P_XBC = P_POOL + POOL_WIDTH
P_K = P_XBC + SSD_XBC
P_V = P_K + NA_WIDTH
P_COLS = P_V + NA_WIDTH


def _layer(x, ctx, mod_l, mod_c, norm1_w, w_in, ssd_conv_w, ssd_conv_b, ssd_dt_bias, ssd_a_log, ssd_d,
           ssd_norm_w, na_q_norm_w, na_k_norm_w, na_rpb, pool_w, pool_scale, w_out, norm2_w, mlp_w_up,
           mlp_conv_w, mlp_conv_b, mlp_w_down, rope_l, rope_c, ctx_out):
    sh1_l, sc1_l, g1_l, sh2_l, sc2_l, g2_l = jnp.split(mod_l, N_MOD)
    sh1_c, sc1_c, g1_c, sh2_c, sc2_c, g2_c = jnp.split(mod_c, N_MOD)

    w_main = jnp.concatenate([w_in[:, :OFF_DT], w_in[:, OFF_K:]], axis=1).astype(BF16)
    w_dt = jnp.pad(w_in[:, OFF_DT:OFF_K], ((0, 0), (0, LANE - 2 * SSD_HEADS))).astype(BF16)
    w_out_b = w_out.astype(BF16)
    w_up_b = mlp_w_up.astype(BF16)
    w_down_b = mlp_w_down.astype(BF16)
    conv_tab = jnp.concatenate([mlp_conv_w, mlp_conv_b[None], jnp.zeros((4, 2 * D_FF), F32)], axis=0)

    h_l = _norm_mod(x, norm1_w, sc1_l, sh1_l)
    h_c = _norm_mod(ctx, norm1_w, sc1_c, sh1_c)
    p_l = _matmul(h_l, w_main, BF16)
    dt_l = _matmul(h_l, w_dt, F32)[:, :2 * SSD_HEADS]
    p_c = _matmul(h_c, w_main, BF16)
    dt_c = _matmul(h_c, w_dt, F32)[:, :2 * SSD_HEADS]

    def seg(p, off, width):
        return p[None, :, off:off + width].astype(F32)

    y_ssd_l, y_ssd_c = _ssd_mixer(seg(p_l, P_Z, SSD_D_INNER), seg(p_l, P_XBC, SSD_XBC), dt_l[None],
                                  seg(p_c, P_Z, SSD_D_INNER) if ctx_out else None,
                                  seg(p_c, P_XBC, SSD_XBC), dt_c[None],
                                  ssd_conv_w, ssd_conv_b, ssd_dt_bias, ssd_a_log, ssd_d, ssd_norm_w)
    q_blk, k_blk, v_blk = P_Q // NA_WIDTH, P_K // NA_WIDTH, P_V // NA_HEAD_DIM
    q_r, k_r = _qk_prep(p_l, q_blk, k_blk, na_q_norm_w, na_k_norm_w, rope_l)
    qc_r, kc_r = _qk_prep(p_c, q_blk, k_blk, na_q_norm_w, na_k_norm_w, rope_c)
    y_na_l = _na_latent(q_r, k_r, p_l, v_blk, kc_r, p_c, v_blk, _na_bias_table(na_rpb))
    y_pool_l = _pool(p_l, P_POOL // POOL_WIDTH, pool_w, pool_scale)

    x_mid = _outproj([y_ssd_l[0].astype(BF16), y_na_l, y_pool_l], w_out_b, x, g1_l)
    h2_l = _norm_mod(x_mid, norm2_w, sc2_l, sh2_l)
    x_new = _conv_ffn(h2_l, w_up_b, conv_tab, w_down_b, x_mid, g2_l)
    ctx_new = ctx
    if ctx_out:
        y_na_c = _na_context(qc_r, kc_r, p_c, v_blk)
        y_pool_c = _pool(p_c, P_POOL // POOL_WIDTH, pool_w, pool_scale)
        c_mid = _outproj([y_ssd_c[0].astype(BF16), y_na_c, y_pool_c], w_out_b, ctx, g1_c)
        h2_c = _norm_mod(c_mid, norm2_w, sc2_c, sh2_c)
        ctx_new = _conv_ffn(h2_c, w_up_b, conv_tab, w_down_b, c_mid, g2_c)
    return x_new, ctx_new


def kernel(x, c, ctx, c_ctx, ada_w, ada_b, norm1_w, w_in, ssd_conv_w, ssd_conv_b, ssd_dt_bias, ssd_a_log,
           ssd_d, ssd_norm_w, na_q_norm_w, na_k_norm_w, na_rpb, pool_w, pool_scale, w_out, norm2_w,
           mlp_w_up, mlp_conv_w, mlp_conv_b, mlp_w_down):
    depth = ada_w.shape[0]
    d = x.shape[-1]
    c_rows = jnp.concatenate([c.reshape(1, d), c_ctx.reshape(1, d), jnp.zeros((MOD_ROWS - 2, d), F32)], axis=0)
    mods = _modulation(c_rows, ada_w, ada_b)
    xs, cs = x[0], ctx[0]
    rope_l = _rope_tables(xs.shape[0])
    ctx_tab = (cs.shape[0], NA_HEAD_DIM)
    rope_c = (jnp.ones(ctx_tab, F32), jnp.zeros(ctx_tab, F32), jnp.zeros(ctx_tab, F32))
    for i in range(depth):
        xs, cs = _layer(xs, cs, mods[i, 0], mods[i, 1], norm1_w[i], w_in[i], ssd_conv_w[i], ssd_conv_b[i],
                        ssd_dt_bias[i], ssd_a_log[i], ssd_d[i], ssd_norm_w[i], na_q_norm_w[i], na_k_norm_w[i],
                        na_rpb[i], pool_w[i], pool_scale[i], w_out[i], norm2_w[i], mlp_w_up[i], mlp_conv_w[i],
                        mlp_conv_b[i], mlp_w_down[i], rope_l, rope_c, ctx_out=(i < depth - 1))
    return xs[None]
```

```python
import functools
import math

import jax
import jax.numpy as jnp
from jax import lax
from jax.experimental import pallas as pl
from jax.experimental.pallas import tpu as pltpu

F32 = jnp.float32
BF16 = jnp.bfloat16

D_MODEL = 4096
DEPTH = 2
GRID_W = 64
D_MIX = D_MODEL
SSD_D_INNER = D_MIX // 2
SSD_HEAD_DIM = 64
SSD_HEADS = SSD_D_INNER // SSD_HEAD_DIM
SSD_GROUPS = 4
SSD_HPG = SSD_HEADS // SSD_GROUPS
SSD_STATE = 128
SSD_GN = SSD_GROUPS * SSD_STATE
SSD_XBC = SSD_D_INNER + 2 * SSD_GN
SSD_CONV = 5
SSD_CHUNK = 128
NA_WIDTH = D_MIX // 4
NA_HEAD_DIM = 128
NA_HEADS = NA_WIDTH // NA_HEAD_DIM
NA_WIN_R = 8
NA_WIN_C = 16
ROPE_BASE = 10000.0
POOL_WIDTH = D_MIX - SSD_D_INNER - NA_WIDTH
POOL_WINDOWS = (2, 4, 8, 16)
POOL_GROUP = POOL_WIDTH // len(POOL_WINDOWS)
D_FF = 11008
N_MOD = 6
RMS_EPS = 1e-6
OFF_Z = 0
OFF_Q = OFF_Z + SSD_D_INNER
OFF_POOL = OFF_Q + NA_WIDTH
OFF_XBC = OFF_POOL + POOL_WIDTH
OFF_DT = OFF_XBC + SSD_XBC
OFF_K = OFF_DT + 2 * SSD_HEADS
OFF_V = OFF_K + NA_WIDTH
IN_COLS = OFF_V + NA_WIDTH

VMEM_LIMIT_BYTES = 58 * 1024 * 1024
LANE = 128
BF16_SUBLANE = 16
MOD_ROWS = 8


def _params(*sem):
    return pltpu.CompilerParams(dimension_semantics=sem, vmem_limit_bytes=VMEM_LIMIT_BYTES)


def _mod_kernel(c_ref, w_ref, b_ref, o_ref):
    c = c_ref[...]
    s = (c * jax.nn.sigmoid(c)).astype(BF16)
    o_ref[0] = jnp.dot(s, w_ref[0].astype(BF16), preferred_element_type=F32) + b_ref[0]


def _modulation(c_rows, ada_w, ada_b, tn=512):
    depth, d, n = ada_w.shape
    return pl.pallas_call(
        _mod_kernel,
        grid=(depth, n // tn),
        in_specs=[pl.BlockSpec((MOD_ROWS, d), lambda l, j: (0, 0)),
                  pl.BlockSpec((1, d, tn), lambda l, j: (l, 0, j)),
                  pl.BlockSpec((1, 1, tn), lambda l, j: (l, 0, j))],
        out_specs=pl.BlockSpec((1, MOD_ROWS, tn), lambda l, j: (l, 0, j)),
        out_shape=jax.ShapeDtypeStruct((depth, MOD_ROWS, n), F32),
        compiler_params=_params("arbitrary", "arbitrary"),
        name="modulation",
    )(c_rows, ada_w, ada_b.reshape(depth, 1, n))


def _norm_kernel(x_ref, w_ref, sc_ref, sh_ref, o_ref):
    x = x_ref[...]
    y = x * lax.rsqrt(jnp.mean(x * x, axis=-1, keepdims=True) + RMS_EPS) * w_ref[...]
    o_ref[...] = (y * (1.0 + sc_ref[...]) + sh_ref[...]).astype(o_ref.dtype)


def _norm_mod(x, w, scale, shift, tm=512):
    m, d = x.shape
    tm = min(tm, m)
    row = pl.BlockSpec((1, d), lambda i: (0, 0))
    return pl.pallas_call(
        _norm_kernel,
        grid=(m // tm,),
        in_specs=[pl.BlockSpec((tm, d), lambda i: (i, 0)), row, row, row],
        out_specs=pl.BlockSpec((tm, d), lambda i: (i, 0)),
        out_shape=jax.ShapeDtypeStruct((m, d), BF16),
        compiler_params=_params("arbitrary"),
        name="norm_mod",
    )(x, w.reshape(1, d), scale.reshape(1, d), shift.reshape(1, d))


def _mm_kernel(a_ref, b_ref, o_ref):
    o_ref[...] = jnp.dot(a_ref[...], b_ref[...], preferred_element_type=F32).astype(o_ref.dtype)


def _matmul(a, b, out_dtype, tm=1024, tn=1024):
    m, k = a.shape
    n = b.shape[1]
    tm, tn = min(tm, m), min(tn, n)
    return pl.pallas_call(
        _mm_kernel,
        grid=(m // tm, n // tn),
        in_specs=[pl.BlockSpec((tm, k), lambda i, j: (i, 0)),
                  pl.BlockSpec((k, tn), lambda i, j: (0, j))],
        out_specs=pl.BlockSpec((tm, tn), lambda i, j: (i, j)),
        out_shape=jax.ShapeDtypeStruct((m, n), out_dtype),
        compiler_params=_params("arbitrary", "arbitrary"),
        name="matmul",
    )(a, b)


def _outproj_kernel(*refs, widths):
    a_refs, (w_ref, x_ref, g_ref, o_ref) = refs[:len(widths)], refs[len(widths):]
    acc, off = None, 0
    for a_ref, k in zip(a_refs, widths):
        part = jnp.dot(a_ref[...], w_ref[off:off + k, :], preferred_element_type=F32)
        acc = part if acc is None else acc + part
        off += k
    o_ref[...] = x_ref[...] + g_ref[...] * acc


def _outproj(parts, w, x, gate, tm=1024, tn=1024):
    m = x.shape[0]
    k, n = w.shape
    tm = min(tm, m)
    widths = tuple(a.shape[1] for a in parts)
    assert sum(widths) == k
    return pl.pallas_call(
        functools.partial(_outproj_kernel, widths=widths),
        grid=(m // tm, n // tn),
        in_specs=[pl.BlockSpec((tm, kw), lambda i, j: (i, 0)) for kw in widths] + [
            pl.BlockSpec((k, tn), lambda i, j: (0, j)),
            pl.BlockSpec((tm, tn), lambda i, j: (i, j)),
            pl.BlockSpec((1, tn), lambda i, j: (0, j))],
        out_specs=pl.BlockSpec((tm, tn), lambda i, j: (i, j)),
        out_shape=jax.ShapeDtypeStruct((m, n), F32),
        compiler_params=_params("arbitrary", "arbitrary"),
        name="outproj",
    )(*parts, w, x, gate.reshape(1, n))


FFN_HALO = BF16_SUBLANE


def _ffn_kernel(h_ref, hp_ref, hn_ref, wg_ref, wv_ref, cg_ref, cv_ref, wd_ref, x_ref, g_ref,
                o_ref, ext_ref, u_ref, *, tm, n_mtiles, nf, tn_d):
    i = pl.program_id(0)
    f = pl.program_id(1)
    d = o_ref.shape[1]

    @pl.when(f == 0)
    def _():
        ext_ref[0:tm, :] = h_ref[...]
        nxt = hn_ref[...]
        prv = hp_ref[...]
        ext_ref[tm:tm + FFN_HALO, :] = jnp.where(i == n_mtiles - 1, jnp.zeros_like(nxt), nxt)
        ext_ref[tm + FFN_HALO:tm + 2 * FFN_HALO, :] = jnp.where(i == 0, jnp.zeros_like(prv), prv)
        o_ref[...] = jnp.zeros_like(o_ref)
        u_ref[1] = jnp.zeros_like(u_ref[1])

    def conv_branch(u, c_ref):
        um = u[0:tm]
        row = lax.broadcasted_iota(jnp.int32, um.shape, 0)
        up = jnp.where(row == 0, u[tm + 2 * FFN_HALO - 1:tm + 2 * FFN_HALO], pltpu.roll(um, 1, 0))
        un = jnp.where(row == tm - 1, u[tm:tm + 1], pltpu.roll(um, tm - 1, 0))
        c = c_ref[...]
        return up * c[0:1] + um * c[1:2] + un * c[2:3] + c[3:4]

    def step(produce, consume):
        gate = conv_branch(u_ref[consume, 0], cg_ref)
        val = conv_branch(u_ref[consume, 1], cv_ref)
        act = gate * jax.nn.sigmoid(gate) * val
        act = jnp.where(f > 0, act, jnp.zeros_like(act)).astype(BF16)
        for n in range(d // tn_d):
            sl = slice(n * tn_d, (n + 1) * tn_d)
            o_ref[:, sl] += jnp.dot(act, wd_ref[:, sl], preferred_element_type=F32)
        h = ext_ref[...]
        u_ref[produce, 0] = jnp.dot(h, wg_ref[0], preferred_element_type=F32)
        u_ref[produce, 1] = jnp.dot(h, wv_ref[0], preferred_element_type=F32)

    @pl.when(f % 2 == 0)
    def _():
        step(0, 1)

    @pl.when(f % 2 == 1)
    def _():
        step(1, 0)

    @pl.when(f == nf)
    def _():
        o_ref[...] = x_ref[...] + g_ref[...] * o_ref[...]


def _conv_ffn(h, w_up_tiles, conv_tab, w_down, x, gate, tm=512, tn_d=1024):
    m, d = h.shape
    ff = w_down.shape[0]
    tf = w_up_tiles.shape[2]
    tm = min(tm, m)
    tn_d = min(tn_d, d)
    n_mtiles, nf = m // tm, ff // tf
    hb = tm // FFN_HALO
    last_hb = m // FFN_HALO - 1
    kern = functools.partial(_ffn_kernel, tm=tm, n_mtiles=n_mtiles, nf=nf, tn_d=tn_d)

    def prod(f):
        return jnp.minimum(f, nf - 1)

    def cons(f):
        return jnp.maximum(f - 1, 0)

    return pl.pallas_call(
        kern,
        grid=(n_mtiles, nf + 1),
        in_specs=[
            pl.BlockSpec((tm, d), lambda i, f: (i, 0), pipeline_mode=pl.Buffered(1)),
            pl.BlockSpec((FFN_HALO, d), lambda i, f: (jnp.maximum(i * hb - 1, 0), 0)),
            pl.BlockSpec((FFN_HALO, d), lambda i, f: (jnp.minimum((i + 1) * hb, last_hb), 0)),
            pl.BlockSpec((1, d, tf), lambda i, f: (prod(f), 0, 0)),
            pl.BlockSpec((1, d, tf), lambda i, f: (prod(f) + nf, 0, 0)),
            pl.BlockSpec((8, tf), lambda i, f: (0, cons(f))),
            pl.BlockSpec((8, tf), lambda i, f: (0, cons(f) + nf)),
            pl.BlockSpec((tf, d), lambda i, f: (cons(f), 0)),
            pl.BlockSpec((tm, d), lambda i, f: (i, 0), pipeline_mode=pl.Buffered(1)),
            pl.BlockSpec((1, d), lambda i, f: (0, 0)),
        ],
        out_specs=pl.BlockSpec((tm, d), lambda i, f: (i, 0)),
        out_shape=jax.ShapeDtypeStruct((m, d), F32),
        scratch_shapes=[pltpu.VMEM((tm + 2 * FFN_HALO, d), BF16),
                        pltpu.VMEM((2, 2, tm + 2 * FFN_HALO, tf), F32)],
        compiler_params=_params("arbitrary", "arbitrary"),
        name="conv_ffn",
    )(h, h, h, w_up_tiles, w_up_tiles, conv_tab, conv_tab, w_down, x, gate.reshape(1, d))


FFN_TF = 256


def _ffn_up_tiles(w_up, tf=FFN_TF):
    d, n = w_up.shape
    return w_up.astype(BF16).reshape(d, n // tf, tf).transpose(1, 0, 2)


def _rope_tables(length):
    t = jnp.arange(length)
    half = NA_HEAD_DIM // 2
    inv = ROPE_BASE ** (-jnp.arange(0, half, 2, dtype=F32) / half)
    ar = (t // GRID_W).astype(F32)[:, None] * inv
    ac = (t % GRID_W).astype(F32)[:, None] * inv
    ang = jnp.concatenate([ar, ar, ac, ac], axis=-1)
    cos, sin = jnp.cos(ang), jnp.sin(ang)
    first = (jnp.arange(NA_HEAD_DIM) % half) < half // 2
    return cos, jnp.where(first, -sin, 0.0), jnp.where(first, 0.0, sin)


def _qk_prep_kernel(q_ref, k_ref, qw_ref, kw_ref, cos_ref, sa_ref, sb_ref, qo_ref, ko_ref):
    cos, sa, sb = cos_ref[...], sa_ref[...], sb_ref[...]
    quarter = NA_HEAD_DIM // 4
    for h in range(NA_HEADS):
        sl = slice(h * NA_HEAD_DIM, (h + 1) * NA_HEAD_DIM)
        for src, w_ref, dst in ((q_ref, qw_ref, qo_ref), (k_ref, kw_ref, ko_ref)):
            x = src[:, sl].astype(F32)
            y = x * lax.rsqrt(jnp.mean(x * x, axis=-1, keepdims=True) + RMS_EPS) * w_ref[...]
            y = (y * cos + pltpu.roll(y, NA_HEAD_DIM - quarter, 1) * sa + pltpu.roll(y, quarter, 1) * sb)
            dst[:, sl] = y.astype(dst.dtype)


def _qk_prep(p, q_blk, k_blk, qw, kw, tables, tm=512):
    m = p.shape[0]
    tm = min(tm, m)
    tab = pl.BlockSpec((tm, NA_HEAD_DIM), lambda i: (i, 0))
    vec = pl.BlockSpec((1, NA_HEAD_DIM), lambda i: (0, 0))
    out = jax.ShapeDtypeStruct((m, NA_WIDTH), BF16)
    return pl.pallas_call(
        _qk_prep_kernel,
        grid=(m // tm,),
        in_specs=[pl.BlockSpec((tm, NA_WIDTH), lambda i: (i, q_blk)),
                  pl.BlockSpec((tm, NA_WIDTH), lambda i: (i, k_blk)), vec, vec, tab, tab, tab],
        out_specs=[pl.BlockSpec((tm, NA_WIDTH), lambda i: (i, 0))] * 2,
        out_shape=[out, out],
        compiler_params=_params("arbitrary"),
        name="qk_prep",
    )(p, p, qw.reshape(1, -1), kw.reshape(1, -1), *tables)


NA_DR = 2 * NA_WIN_R
NA_BIAS_CHUNK = 1024


def _na_bias_kernel(r0_ref, r1_ref, o_ref):
    def pieces(r):
        out = []
        for _ in range(3):
            piece = r.astype(BF16)
            out.append(piece)
            r = r - piece.astype(F32)
        return out

    p0, p1 = pieces(r0_ref[0]), pieces(r1_ref[0])
    for c in range(GRID_W * LANE // NA_BIAS_CHUNK):
        shape = (LANE, NA_BIAS_CHUNK)
        pos = c * NA_BIAS_CHUNK + lax.broadcasted_iota(jnp.int32, shape, 1)
        j = lax.broadcasted_iota(jnp.int32, shape, 0)
        qc = pos // LANE
        lane = pos % LANE
        kc = lane % GRID_W
        idx = jnp.clip(kc - qc + (NA_WIN_C - 1), 0, 2 * NA_WIN_C - 2)
        hit = idx == j
        second = lane >= GRID_W
        oh0 = jnp.where(hit, jnp.where(second, 0.0, 1.0), 0.0).astype(BF16)
        oh1 = jnp.where(hit, jnp.where(second, 1.0, 0.0), 0.0).astype(BF16)
        acc = jnp.zeros((NA_DR, NA_BIAS_CHUNK), F32)
        for a, b in zip(p0, p1):
            acc = acc + jnp.dot(a, oh0, preferred_element_type=F32) + jnp.dot(b, oh1, preferred_element_type=F32)
        c0 = jnp.clip(qc[0:1] - NA_WIN_C // 2, 0, GRID_W - NA_WIN_C)
        ok = (kc[0:1] >= c0) & (kc[0:1] < c0 + NA_WIN_C)
        o_ref[0, :, c * NA_BIAS_CHUNK:(c + 1) * NA_BIAS_CHUNK] = jnp.where(ok, acc, -jnp.inf)


def _na_bias_table(rpb):
    nh, ndr, ndc = rpb.shape
    r0 = jnp.pad(rpb, ((0, 0), (0, NA_DR - ndr), (0, LANE - ndc)))
    r1 = jnp.pad(rpb[:, 1:], ((0, 0), (0, NA_DR - ndr + 1), (0, LANE - ndc)))
    spec = pl.BlockSpec((1, NA_DR, LANE), lambda h: (h, 0, 0))
    out = pl.pallas_call(
        _na_bias_kernel,
        grid=(nh,),
        in_specs=[spec, spec],
        out_specs=pl.BlockSpec((1, NA_DR, GRID_W * LANE), lambda h: (h, 0, 0)),
        out_shape=jax.ShapeDtypeStruct((nh, NA_DR, GRID_W * LANE), F32),
        compiler_params=_params("arbitrary"),
        name="na_bias",
    )(r0, r1)
    return out.reshape(nh, NA_DR, GRID_W, LANE)


def _softmax_pv(scores, values):
    m = functools.reduce(jnp.maximum, [jnp.max(s, axis=1, keepdims=True) for s in scores])
    ps = [jnp.exp(s - m) for s in scores]
    den = functools.reduce(jnp.add, [jnp.sum(p, axis=1, keepdims=True) for p in ps])
    o = functools.reduce(jnp.add, [jnp.dot(p.astype(BF16), v, preferred_element_type=F32)
                                   for p, v in zip(ps, values)])
    return o / den


_NT = (((1,), (1,)), ((), ()))


def _na_kernel(q_ref, k_ref, v_ref, kc_ref, vc_ref, tz_ref, o_ref, *, rows_per_step, n_rows):
    t = pl.program_id(1)
    scale = NA_HEAD_DIM ** -0.5
    band = NA_WIN_R * GRID_W
    kc, vc = kc_ref[...], vc_ref[...]
    for i in range(rows_per_step):
        qr = t * rows_per_step + i
        bs = jnp.clip(qr - NA_WIN_R // 2, 0, n_rows - NA_WIN_R)
        off = bs - qr + (NA_WIN_R - 1)
        start = pl.multiple_of(bs * GRID_W, GRID_W)
        q = q_ref[i * GRID_W:(i + 1) * GRID_W, :]
        kb = k_ref[pl.ds(start, band), :]
        vb = v_ref[pl.ds(start, band), :]
        bias = jnp.concatenate([tz_ref[0, off + 2 * w] for w in range(NA_WIN_R // 2)], axis=1)
        s_loc = lax.dot_general(q, kb, _NT, preferred_element_type=F32) * scale + bias
        s_ctx = lax.dot_general(q, kc, _NT, preferred_element_type=F32) * scale
        o = _softmax_pv([s_loc, s_ctx], [vb, vc])
        o_ref[i * GRID_W:(i + 1) * GRID_W, :] = o.astype(o_ref.dtype)


def _na_latent(q_r, k_r, p_l, v_blk, kc_r, p_c, vc_blk, tz, rows_per_step=4):
    length = q_r.shape[0]
    lc = kc_r.shape[0]
    n_rows = length // GRID_W
    assert n_rows >= NA_WIN_R and n_rows % rows_per_step == 0
    hd = NA_HEAD_DIM
    kern = functools.partial(_na_kernel, rows_per_step=rows_per_step, n_rows=n_rows)
    return pl.pallas_call(
        kern,
        grid=(NA_HEADS, n_rows // rows_per_step),
        in_specs=[pl.BlockSpec((rows_per_step * GRID_W, hd), lambda h, t: (t, h)),
                  pl.BlockSpec((length, hd), lambda h, t: (0, h)),
                  pl.BlockSpec((length, hd), lambda h, t: (0, v_blk + h)),
                  pl.BlockSpec((lc, hd), lambda h, t: (0, h)),
                  pl.BlockSpec((lc, hd), lambda h, t: (0, vc_blk + h)),
                  pl.BlockSpec((1, NA_DR, GRID_W, LANE), lambda h, t: (h, 0, 0, 0))],
        out_specs=pl.BlockSpec((rows_per_step * GRID_W, hd), lambda h, t: (t, h)),
        out_shape=jax.ShapeDtypeStruct((length, NA_WIDTH), BF16),
        compiler_params=_params("arbitrary", "arbitrary"),
        name="na_latent",
    )(q_r, k_r, p_l, kc_r, p_c, tz)


def _ctx_attn_kernel(q_ref, k_ref, v_ref, o_ref):
    s = lax.dot_general(q_ref[...], k_ref[...], _NT, preferred_element_type=F32) * (NA_HEAD_DIM ** -0.5)
    o_ref[...] = _softmax_pv([s], [v_ref[...]]).astype(o_ref.dtype)


def _na_context(qc_r, kc_r, p_c, vc_blk):
    lc = qc_r.shape[0]
    hd = NA_HEAD_DIM
    spec = pl.BlockSpec((lc, hd), lambda h: (0, h))
    return pl.pallas_call(
        _ctx_attn_kernel,
        grid=(NA_HEADS,),
        in_specs=[spec, spec, pl.BlockSpec((lc, hd), lambda h: (0, vc_blk + h))],
        out_specs=spec,
        out_shape=jax.ShapeDtypeStruct((lc, NA_WIDTH), BF16),
        compiler_params=_params("arbitrary"),
        name="na_context",
    )(qc_r, kc_r, p_c)


POOL_HALO = BF16_SUBLANE


def _pool_kernel(u_ref, up_ref, un_ref, w_ref, sc_ref, o_ref, ext_ref, *, tm, n_tiles, length):
    i = pl.program_id(0)
    prv, nxt = up_ref[...], un_ref[...]
    ext_ref[0:POOL_HALO, :] = jnp.where(i == 0, jnp.zeros_like(prv), prv).astype(F32)
    ext_ref[POOL_HALO:POOL_HALO + tm, :] = u_ref[...].astype(F32)
    ext_ref[POOL_HALO + tm:POOL_HALO + tm + POOL_HALO, :] = jnp.where(i == n_tiles - 1, jnp.zeros_like(nxt),
                                                                     nxt).astype(F32)
    t = i * tm + lax.broadcasted_iota(jnp.int32, (tm, 1), 0)
    for g, w in enumerate(POOL_WINDOWS):
        cs = slice(g * POOL_GROUP, (g + 1) * POOL_GROUP)
        acc = ext_ref[pl.ds(POOL_HALO - w // 2, tm), cs]
        for k in range(1, w):
            acc = acc + ext_ref[pl.ds(POOL_HALO - w // 2 + k, tm), cs]
        cnt = (jnp.minimum(t + w // 2, length) - jnp.maximum(t - w // 2, 0)).astype(F32)
        pooled = acc / cnt - ext_ref[pl.ds(POOL_HALO, tm), cs]
        y = jnp.dot(pooled.astype(BF16), w_ref[g], preferred_element_type=F32) * sc_ref[:, cs]
        o_ref[:, cs] = y.astype(o_ref.dtype)


def _pool(p, u_blk, pool_w, pool_scale, tm=512):
    length = p.shape[0]
    tm = min(tm, length)
    n_tiles = length // tm
    hb = tm // POOL_HALO
    last_hb = length // POOL_HALO - 1
    kern = functools.partial(_pool_kernel, tm=tm, n_tiles=n_tiles, length=length)
    return pl.pallas_call(
        kern,
        grid=(n_tiles,),
        in_specs=[pl.BlockSpec((tm, POOL_WIDTH), lambda i: (i, u_blk)),
                  pl.BlockSpec((POOL_HALO, POOL_WIDTH), lambda i: (jnp.maximum(i * hb - 1, 0), u_blk)),
                  pl.BlockSpec((POOL_HALO, POOL_WIDTH), lambda i: (jnp.minimum((i + 1) * hb, last_hb), u_blk)),
                  pl.BlockSpec((len(POOL_WINDOWS), POOL_GROUP, POOL_GROUP), lambda i: (0, 0, 0)),
                  pl.BlockSpec((1, POOL_WIDTH), lambda i: (0, 0))],
        out_specs=pl.BlockSpec((tm, POOL_WIDTH), lambda i: (i, 0)),
        out_shape=jax.ShapeDtypeStruct((length, POOL_WIDTH), BF16),
        scratch_shapes=[pltpu.VMEM((tm + 2 * POOL_HALO, POOL_WIDTH), F32)],
        compiler_params=_params("arbitrary"),
        name="pool",
    )(p, p, p, pool_w.astype(BF16), pool_scale.reshape(1, POOL_WIDTH))


SSD_HALO = BF16_SUBLANE
SSD_CONV_BLK = 1024
SSD_PAIRS = SSD_HEADS // 2


def _ssd_conv_kernel(u_ref, up_ref, un_ref, tab_ref, o_ref, ext_ref, *, tm, n_tiles):
    i = pl.program_id(0)
    prv, nxt = up_ref[...], un_ref[...]
    ext_ref[0:SSD_HALO, :] = jnp.where(i == 0, jnp.zeros_like(prv), prv).astype(F32)
    ext_ref[SSD_HALO:SSD_HALO + tm, :] = u_ref[...].astype(F32)
    ext_ref[SSD_HALO + tm:SSD_HALO + tm + SSD_HALO, :] = jnp.where(i == n_tiles - 1, jnp.zeros_like(nxt),
                                                                   nxt).astype(F32)
    tab = tab_ref[...]
    left = (SSD_CONV - 1) // 2
    acc = tab[SSD_CONV:SSD_CONV + 1]
    for k in range(SSD_CONV):
        acc = acc + ext_ref[pl.ds(SSD_HALO - left + k, tm), :] * tab[k:k + 1]
    o_ref[...] = (acc * jax.nn.sigmoid(acc)).astype(o_ref.dtype)


def _ssd_conv(p, blk0, conv_w, conv_b, tm=512):
    length = p.shape[0]
    tm = min(tm, length)
    n_tiles = length // tm
    hb = tm // SSD_HALO
    last_hb = length // SSD_HALO - 1
    cb = SSD_CONV_BLK
    tab = jnp.concatenate([conv_w, conv_b[None], jnp.zeros((8 - SSD_CONV - 1, SSD_XBC), F32)], axis=0)
    kern = functools.partial(_ssd_conv_kernel, tm=tm, n_tiles=n_tiles)
    return pl.pallas_call(
        kern,
        grid=(n_tiles, SSD_XBC // cb),
        in_specs=[pl.BlockSpec((tm, cb), lambda i, j: (i, blk0 + j)),
                  pl.BlockSpec((SSD_HALO, cb), lambda i, j: (jnp.maximum(i * hb - 1, 0), blk0 + j)),
                  pl.BlockSpec((SSD_HALO, cb), lambda i, j: (jnp.minimum((i + 1) * hb, last_hb), blk0 + j)),
                  pl.BlockSpec((8, cb), lambda i, j: (0, j))],
        out_specs=pl.BlockSpec((tm, cb), lambda i, j: (i, j)),
        out_shape=jax.ShapeDtypeStruct((length, SSD_XBC), BF16),
        scratch_shapes=[pltpu.VMEM((tm + 2 * SSD_HALO, cb), F32)],
        compiler_params=_params("arbitrary", "arbitrary"),
        name="ssd_conv",
    )(p, p, p, tab)


def _ssd_dt_kernel(raw_ref, bias_ref, alog_ref, dt_ref, da_ref):
    v = raw_ref[...] + bias_ref[...]
    dt = jnp.maximum(v, 0.0) + jnp.log1p(jnp.exp(-jnp.abs(v)))
    dt_ref[...] = dt
    da_ref[...] = dt * -jnp.exp(alog_ref[...])


def _ssd_dt(raw, dt_bias, a_log, tm=2048):
    length = raw.shape[0]
    tm = min(tm, length)
    pad = LANE - 2 * SSD_HEADS
    spec = pl.BlockSpec((tm, LANE), lambda i: (i, 0))
    vec = pl.BlockSpec((1, LANE), lambda i: (0, 0))
    out = jax.ShapeDtypeStruct((length, LANE), F32)
    return pl.pallas_call(
        _ssd_dt_kernel,
        grid=(length // tm,),
        in_specs=[spec, vec, vec],
        out_specs=[spec, spec],
        out_shape=[out, out],
        compiler_params=_params("arbitrary"),
        name="ssd_dt",
    )(raw, jnp.pad(dt_bias.reshape(1, -1), ((0, 0), (0, pad))), jnp.pad(a_log.reshape(1, -1), ((0, 0), (0, pad))))


def _pieces(a):
    out = []
    for _ in range(3):
        piece = a.astype(BF16)
        out.append(piece)
        a = a - piece.astype(F32)
    return out


def _ssd_scan_kernel(*refs, direction, reverse, finalize):
    if finalize:
        (x_ref, b_ref, c_ref, dac_ref, dtr_ref, dar_ref, h0_ref, yp_ref, z_ref, dsk_ref, nw_ref,
         y_ref, hn_ref, s_ref, yacc_ref) = refs
    else:
        x_ref, b_ref, c_ref, dac_ref, dtr_ref, dar_ref, h0_ref, y_ref, hn_ref, s_ref = refs
        yacc_ref = y_ref
    step = pl.program_id(0)
    nsteps = pl.num_programs(0)
    t_len = SSD_CHUNK

    @pl.when(step == 0)
    def _():
        s_ref[...] = h0_ref[...]

    row = lax.broadcasted_iota(jnp.int32, (t_len, t_len), 0)
    col = lax.broadcasted_iota(jnp.int32, (t_len, t_len), 1)
    mask = (col >= row) if reverse else (col <= row)
    mask_t = (row >= col) if reverse else (row <= col)
    ones = jnp.ones((t_len, t_len), BF16)
    tri = jnp.where(mask, 1.0, 0.0).astype(BF16)
    tri_t = jnp.where(mask_t, 1.0, 0.0).astype(BF16)
    hsl = slice(direction * SSD_HEADS, (direction + 1) * SSD_HEADS)

    g_col = sum(jnp.dot(tri, piece, preferred_element_type=F32) for piece in _pieces(dac_ref[...]))
    da_row = dar_ref[hsl, :]
    da_row_p = _pieces(da_row)
    g_row = sum(jnp.dot(piece, tri_t, preferred_element_type=F32) for piece in da_row_p)
    tot_row = sum(jnp.dot(piece, ones, preferred_element_type=F32) for piece in da_row_p)
    dt_row = dtr_ref[hsl, :]
    wdt_row = jnp.exp(tot_row - g_row) * dt_row
    lane = lax.broadcasted_iota(jnp.int32, (t_len, LANE), 1)
    first = lane < SSD_HEAD_DIM
    edge = slice(0, 1) if reverse else slice(t_len - 1, t_len)

    for grp in range(SSD_GROUPS):
        nsl = slice(grp * SSD_STATE, (grp + 1) * SSD_STATE)
        b_g, c_g = b_ref[:, nsl], c_ref[:, nsl]
        cb = lax.dot_general(c_g, b_g, _NT, preferred_element_type=F32)
        bt_g = b_g.astype(F32).T
        for k in range(grp * SSD_HPG // 2, (grp + 1) * SSD_HPG // 2):
            ms, es, bws = [], [], []
            for h in (2 * k, 2 * k + 1):
                j = direction * SSD_HEADS + h
                g_bc = jnp.broadcast_to(g_col[:, j:j + 1], (t_len, t_len))
                decay = jnp.where(mask, jnp.exp(g_bc - g_row[h:h + 1, :]), 0.0)
                ms.append((decay * cb * dt_row[h:h + 1, :]).astype(BF16))
                es.append(jnp.exp(g_bc))
                bws.append((bt_g * wdt_row[h:h + 1, :]).astype(BF16))
            csl = slice(k * LANE, (k + 1) * LANE)
            xp = x_ref[:, csl]
            xa = jnp.where(first, xp, jnp.zeros_like(xp))
            xb = jnp.where(first, jnp.zeros_like(xp), xp)
            e_pair = jnp.where(first, es[0], es[1])
            s_old = s_ref[k]
            y = (jnp.dot(ms[0], xa, preferred_element_type=F32) + jnp.dot(ms[1], xb, preferred_element_type=F32)
                 + jnp.dot(c_g, s_old.astype(BF16), preferred_element_type=F32) * e_pair)
            s_ref[k] = (s_old * e_pair[edge] + jnp.dot(bws[0], xa, preferred_element_type=F32)
                        + jnp.dot(bws[1], xb, preferred_element_type=F32))
            yacc_ref[:, csl] = y

    @pl.when(step == nsteps - 1)
    def _():
        hn_ref[...] = s_ref[...]

    if finalize:
        y = yacc_ref[...] + yp_ref[...] + x_ref[...].astype(F32) * dsk_ref[...]
        zz = z_ref[...].astype(F32)
        gated = y * (zz * jax.nn.sigmoid(zz))
        out = gated * lax.rsqrt(jnp.mean(gated * gated, axis=-1, keepdims=True) + RMS_EPS) * nw_ref[...]
        y_ref[...] = out.astype(y_ref.dtype)


def _ssd_scan(xbc, da_col, dt_row, da_row, h0, direction, final=None):
    length = xbc.shape[0]
    t_len = SSD_CHUNK
    n_chunks = length // t_len
    reverse = direction == 1
    pos = (lambda c: n_chunks - 1 - c) if reverse else (lambda c: c)
    di, gn = SSD_D_INNER, SSD_GN
    state = jax.ShapeDtypeStruct((SSD_PAIRS, SSD_STATE, LANE), F32)
    state_spec = pl.BlockSpec((SSD_PAIRS, SSD_STATE, LANE), lambda c: (0, 0, 0))
    in_specs = [pl.BlockSpec((t_len, di), lambda c: (pos(c), 0)),
                pl.BlockSpec((t_len, gn), lambda c: (pos(c), di // gn)),
                pl.BlockSpec((t_len, gn), lambda c: (pos(c), di // gn + 1)),
                pl.BlockSpec((t_len, LANE), lambda c: (pos(c), 0)),
                pl.BlockSpec((2 * SSD_HEADS, t_len), lambda c: (0, pos(c))),
                pl.BlockSpec((2 * SSD_HEADS, t_len), lambda c: (0, pos(c))),
                state_spec]
    args = [xbc, xbc, xbc, da_col, dt_row, da_row, h0]
    scratch = [pltpu.VMEM((SSD_PAIRS, SSD_STATE, LANE), F32)]
    y_spec = pl.BlockSpec((t_len, di), lambda c: (pos(c), 0))
    row_spec = pl.BlockSpec((1, di), lambda c: (0, 0))
    if final is not None:
        y_other, p, d_skip_row, norm_w_row = final
        in_specs += [y_spec, y_spec, row_spec, row_spec]
        args += [y_other, p, d_skip_row, norm_w_row]
        scratch.append(pltpu.VMEM((t_len, di), F32))
    y_dtype = BF16 if final is not None else F32
    kern = functools.partial(_ssd_scan_kernel, direction=direction, reverse=reverse, finalize=final is not None)
    return pl.pallas_call(
        kern,
        grid=(n_chunks,),
        in_specs=in_specs,
        out_specs=[y_spec, state_spec],
        out_shape=[jax.ShapeDtypeStruct((length, di), y_dtype), state],
        scratch_shapes=scratch,
        compiler_params=_params("arbitrary"),
        name="ssd_scan",
    )(*args)


def _ssd_inputs(p, dt_raw, conv_w, conv_b, dt_bias, a_log):
    xbc = _ssd_conv(p, P_XBC // SSD_CONV_BLK, conv_w, conv_b)
    dt, da = _ssd_dt(dt_raw, dt_bias, a_log)
    nh = 2 * SSD_HEADS
    return xbc, da, dt[:, :nh].T, da[:, :nh].T


def _ssd_pallas(p_l, dt_raw_l, p_c, dt_raw_c, conv_w, conv_b, dt_bias, a_log, d_skip, norm_w, ctx_out):
    xbc_l, dac_l, dtr_l, dar_l = _ssd_inputs(p_l, dt_raw_l, conv_w, conv_b, dt_bias, a_log)
    xbc_c, dac_c, dtr_c, dar_c = _ssd_inputs(p_c, dt_raw_c, conv_w, conv_b, dt_bias, a_log)
    zero = jnp.zeros((SSD_PAIRS, SSD_STATE, LANE), F32)
    fin = (jnp.repeat(d_skip, SSD_HEAD_DIM).reshape(1, -1), norm_w.reshape(1, -1))
    yc0, h_fwd = _ssd_scan(xbc_c, dac_c, dtr_c, dar_c, zero, 0)
    if ctx_out:
        out_c, h_bwd = _ssd_scan(xbc_c, dac_c, dtr_c, dar_c, zero, 1, final=(yc0, p_c) + fin)
    else:
        out_c = None
        _, h_bwd = _ssd_scan(xbc_c, dac_c, dtr_c, dar_c, zero, 1)
    yl0, _ = _ssd_scan(xbc_l, dac_l, dtr_l, dar_l, h_fwd, 0)
    out_l, _ = _ssd_scan(xbc_l, dac_l, dtr_l, dar_l, h_bwd, 1, final=(yl0, p_l) + fin)
    return out_l, out_c


P_Z = 0
P_Q = P_Z + SSD_D_INNER
P_POOL = P_Q + NA_WIDTH
P_XBC = P_POOL + POOL_WIDTH
P_K = P_XBC + SSD_XBC
P_V = P_K + NA_WIDTH
P_COLS = P_V + NA_WIDTH


def _layer(x, ctx, mod_l, mod_c, norm1_w, w_in, ssd_conv_w, ssd_conv_b, ssd_dt_bias, ssd_a_log, ssd_d,
           ssd_norm_w, na_q_norm_w, na_k_norm_w, na_rpb, pool_w, pool_scale, w_out, norm2_w, mlp_w_up,
           mlp_conv_w, mlp_conv_b, mlp_w_down, rope_l, rope_c, ctx_out):
    sh1_l, sc1_l, g1_l, sh2_l, sc2_l, g2_l = jnp.split(mod_l, N_MOD)
    sh1_c, sc1_c, g1_c, sh2_c, sc2_c, g2_c = jnp.split(mod_c, N_MOD)

    w_main = jnp.concatenate([w_in[:, :OFF_DT], w_in[:, OFF_K:]], axis=1).astype(BF16)
    w_dt = jnp.pad(w_in[:, OFF_DT:OFF_K], ((0, 0), (0, LANE - 2 * SSD_HEADS))).astype(BF16)
    w_out_b = w_out.astype(BF16)
    w_up_b = _ffn_up_tiles(mlp_w_up)
    w_down_b = mlp_w_down.astype(BF16)
    conv_tab = jnp.concatenate([mlp_conv_w, mlp_conv_b[None], jnp.zeros((4, 2 * D_FF), F32)], axis=0)

    h_l = _norm_mod(x, norm1_w, sc1_l, sh1_l)
    h_c = _norm_mod(ctx, norm1_w, sc1_c, sh1_c)
    p_l = _matmul(h_l, w_main, BF16)
    dt_l = _matmul(h_l, w_dt, F32)
    p_c = _matmul(h_c, w_main, BF16)
    dt_c = _matmul(h_c, w_dt, F32)

    y_ssd_l, y_ssd_c = _ssd_pallas(p_l, dt_l, p_c, dt_c, ssd_conv_w, ssd_conv_b, ssd_dt_bias, ssd_a_log, ssd_d,
                                   ssd_norm_w, ctx_out)
    q_blk, k_blk, v_blk = P_Q // NA_WIDTH, P_K // NA_WIDTH, P_V // NA_HEAD_DIM
    q_r, k_r = _qk_prep(p_l, q_blk, k_blk, na_q_norm_w, na_k_norm_w, rope_l)
    qc_r, kc_r = _qk_prep(p_c, q_blk, k_blk, na_q_norm_w, na_k_norm_w, rope_c)
    y_na_l = _na_latent(q_r, k_r, p_l, v_blk, kc_r, p_c, v_blk, _na_bias_table(na_rpb))
    y_pool_l = _pool(p_l, P_POOL // POOL_WIDTH, pool_w, pool_scale)

    x_mid = _outproj([y_ssd_l, y_na_l, y_pool_l], w_out_b, x, g1_l)
    h2_l = _norm_mod(x_mid, norm2_w, sc2_l, sh2_l)
    x_new = _conv_ffn(h2_l, w_up_b, conv_tab, w_down_b, x_mid, g2_l)
    ctx_new = ctx
    if ctx_out:
        y_na_c = _na_context(qc_r, kc_r, p_c, v_blk)
        y_pool_c = _pool(p_c, P_POOL // POOL_WIDTH, pool_w, pool_scale)
        c_mid = _outproj([y_ssd_c, y_na_c, y_pool_c], w_out_b, ctx, g1_c)
        h2_c = _norm_mod(c_mid, norm2_w, sc2_c, sh2_c)
        ctx_new = _conv_ffn(h2_c, w_up_b, conv_tab, w_down_b, c_mid, g2_c)
    return x_new, ctx_new


def kernel(x, c, ctx, c_ctx, ada_w, ada_b, norm1_w, w_in, ssd_conv_w, ssd_conv_b, ssd_dt_bias, ssd_a_log,
           ssd_d, ssd_norm_w, na_q_norm_w, na_k_norm_w, na_rpb, pool_w, pool_scale, w_out, norm2_w,
           mlp_w_up, mlp_conv_w, mlp_conv_b, mlp_w_down):
    depth = ada_w.shape[0]
    d = x.shape[-1]
    c_rows = jnp.concatenate([c.reshape(1, d), c_ctx.reshape(1, d), jnp.zeros((MOD_ROWS - 2, d), F32)], axis=0)
    mods = _modulation(c_rows, ada_w, ada_b)
    xs, cs = x[0], ctx[0]
    rope_l = _rope_tables(xs.shape[0])
    ctx_tab = (cs.shape[0], NA_HEAD_DIM)
    rope_c = (jnp.ones(ctx_tab, F32), jnp.zeros(ctx_tab, F32), jnp.zeros(ctx_tab, F32))
    for i in range(depth):
        xs, cs = _layer(xs, cs, mods[i, 0], mods[i, 1], norm1_w[i], w_in[i], ssd_conv_w[i], ssd_conv_b[i],
                        ssd_dt_bias[i], ssd_a_log[i], ssd_d[i], ssd_norm_w[i], na_q_norm_w[i], na_k_norm_w[i],
                        na_rpb[i], pool_w[i], pool_scale[i], w_out[i], norm2_w[i], mlp_w_up[i], mlp_conv_w[i],
                        mlp_conv_b[i], mlp_w_down[i], rope_l, rope_c, ctx_out=(i < depth - 1))
    return xs[None]
```

```python
import functools
import math

import jax
import jax.numpy as jnp
from jax import lax
from jax.experimental import pallas as pl
from jax.experimental.pallas import tpu as pltpu

F32 = jnp.float32
BF16 = jnp.bfloat16

D_MODEL = 4096
DEPTH = 2
GRID_W = 64
D_MIX = D_MODEL
SSD_D_INNER = D_MIX // 2
SSD_HEAD_DIM = 64
SSD_HEADS = SSD_D_INNER // SSD_HEAD_DIM
SSD_GROUPS = 4
SSD_HPG = SSD_HEADS // SSD_GROUPS
SSD_STATE = 128
SSD_GN = SSD_GROUPS * SSD_STATE
SSD_XBC = SSD_D_INNER + 2 * SSD_GN
SSD_CONV = 5
SSD_CHUNK = 128
NA_WIDTH = D_MIX // 4
NA_HEAD_DIM = 128
NA_HEADS = NA_WIDTH // NA_HEAD_DIM
NA_WIN_R = 8
NA_WIN_C = 16
ROPE_BASE = 10000.0
POOL_WIDTH = D_MIX - SSD_D_INNER - NA_WIDTH
POOL_WINDOWS = (2, 4, 8, 16)
POOL_GROUP = POOL_WIDTH // len(POOL_WINDOWS)
D_FF = 11008
N_MOD = 6
RMS_EPS = 1e-6
OFF_Z = 0
OFF_Q = OFF_Z + SSD_D_INNER
OFF_POOL = OFF_Q + NA_WIDTH
OFF_XBC = OFF_POOL + POOL_WIDTH
OFF_DT = OFF_XBC + SSD_XBC
OFF_K = OFF_DT + 2 * SSD_HEADS
OFF_V = OFF_K + NA_WIDTH
IN_COLS = OFF_V + NA_WIDTH

VMEM_LIMIT_BYTES = 58 * 1024 * 1024
LANE = 128
BF16_SUBLANE = 16
MOD_ROWS = 8


def _params(*sem):
    return pltpu.CompilerParams(dimension_semantics=sem, vmem_limit_bytes=VMEM_LIMIT_BYTES)


def _mod_kernel(c_ref, w_ref, b_ref, o_ref):
    c = c_ref[...]
    s = (c * jax.nn.sigmoid(c)).astype(BF16)
    o_ref[0] = jnp.dot(s, w_ref[0].astype(BF16), preferred_element_type=F32) + b_ref[0]


def _modulation(c_rows, ada_w, ada_b, tn=512):
    depth, d, n = ada_w.shape
    return pl.pallas_call(
        _mod_kernel,
        grid=(depth, n // tn),
        in_specs=[pl.BlockSpec((MOD_ROWS, d), lambda l, j: (0, 0)),
                  pl.BlockSpec((1, d, tn), lambda l, j: (l, 0, j)),
                  pl.BlockSpec((1, 1, tn), lambda l, j: (l, 0, j))],
        out_specs=pl.BlockSpec((1, MOD_ROWS, tn), lambda l, j: (l, 0, j)),
        out_shape=jax.ShapeDtypeStruct((depth, MOD_ROWS, n), F32),
        compiler_params=_params("arbitrary", "arbitrary"),
        name="modulation",
    )(c_rows, ada_w, ada_b.reshape(depth, 1, n))


def _norm_kernel(x_ref, w_ref, sc_ref, sh_ref, o_ref):
    x = x_ref[...]
    y = x * lax.rsqrt(jnp.mean(x * x, axis=-1, keepdims=True) + RMS_EPS) * w_ref[...]
    o_ref[...] = (y * (1.0 + sc_ref[...]) + sh_ref[...]).astype(o_ref.dtype)


def _norm_mod(x, w, scale, shift, tm=512):
    m, d = x.shape
    tm = min(tm, m)
    row = pl.BlockSpec((1, d), lambda i: (0, 0))
    return pl.pallas_call(
        _norm_kernel,
        grid=(m // tm,),
        in_specs=[pl.BlockSpec((tm, d), lambda i: (i, 0)), row, row, row],
        out_specs=pl.BlockSpec((tm, d), lambda i: (i, 0)),
        out_shape=jax.ShapeDtypeStruct((m, d), BF16),
        compiler_params=_params("arbitrary"),
        name="norm_mod",
    )(x, w.reshape(1, d), scale.reshape(1, d), shift.reshape(1, d))


def _mm_kernel(a_ref, b_ref, o_ref):
    o_ref[...] = jnp.dot(a_ref[...], b_ref[...], preferred_element_type=F32).astype(o_ref.dtype)


def _matmul(a, b, out_dtype, tm=1024, tn=1024):
    m, k = a.shape
    n = b.shape[1]
    tm, tn = min(tm, m), min(tn, n)
    return pl.pallas_call(
        _mm_kernel,
        grid=(m // tm, n // tn),
        in_specs=[pl.BlockSpec((tm, k), lambda i, j: (i, 0)),
                  pl.BlockSpec((k, tn), lambda i, j: (0, j))],
        out_specs=pl.BlockSpec((tm, tn), lambda i, j: (i, j)),
        out_shape=jax.ShapeDtypeStruct((m, n), out_dtype),
        compiler_params=_params("arbitrary", "arbitrary"),
        name="matmul",
    )(a, b)


def _outproj_kernel(*refs, widths):
    a_refs, (w_ref, x_ref, g_ref, o_ref) = refs[:len(widths)], refs[len(widths):]
    acc, off = None, 0
    for a_ref, k in zip(a_refs, widths):
        part = jnp.dot(a_ref[...], w_ref[off:off + k, :], preferred_element_type=F32)
        acc = part if acc is None else acc + part
        off += k
    o_ref[...] = x_ref[...] + g_ref[...] * acc


def _outproj(parts, w, x, gate, tm=1024, tn=1024):
    m = x.shape[0]
    k, n = w.shape
    tm = min(tm, m)
    widths = tuple(a.shape[1] for a in parts)
    assert sum(widths) == k
    return pl.pallas_call(
        functools.partial(_outproj_kernel, widths=widths),
        grid=(m // tm, n // tn),
        in_specs=[pl.BlockSpec((tm, kw), lambda i, j: (i, 0)) for kw in widths] + [
            pl.BlockSpec((k, tn), lambda i, j: (0, j)),
            pl.BlockSpec((tm, tn), lambda i, j: (i, j)),
            pl.BlockSpec((1, tn), lambda i, j: (0, j))],
        out_specs=pl.BlockSpec((tm, tn), lambda i, j: (i, j)),
        out_shape=jax.ShapeDtypeStruct((m, n), F32),
        compiler_params=_params("arbitrary", "arbitrary"),
        name="outproj",
    )(*parts, w, x, gate.reshape(1, n))


FFN_HALO = BF16_SUBLANE


def _ffn_kernel(h_ref, hp_ref, hn_ref, wg_ref, wv_ref, cg_ref, cv_ref, wd_ref, x_ref, g_ref,
                o_ref, ext_ref, ua_ref, ub_ref, *, tm, n_mtiles, nf, tn_d, row_blocks):
    i = pl.program_id(0)
    f = pl.program_id(1)
    d = o_ref.shape[1]
    u_refs = (ua_ref, ub_ref)

    @pl.when(f == 0)
    def _():
        ext_ref[0:tm, :] = h_ref[...]
        nxt = hn_ref[...]
        prv = hp_ref[...]
        ext_ref[tm:tm + FFN_HALO, :] = jnp.where(i == n_mtiles - 1, jnp.zeros_like(nxt), nxt)
        ext_ref[tm + FFN_HALO:tm + 2 * FFN_HALO, :] = jnp.where(i == 0, jnp.zeros_like(prv), prv)
        o_ref[...] = jnp.zeros_like(o_ref)
        ub_ref[...] = jnp.zeros_like(ub_ref)

    rb = tm // row_blocks

    def conv_rows(u_ref, br, r0, c_ref):
        um = u_ref[br, r0:r0 + rb]
        row = lax.broadcasted_iota(jnp.int32, um.shape, 0)
        before = tm + 2 * FFN_HALO - 1 if r0 == 0 else r0 - 1
        up = jnp.where(row == 0, u_ref[br, before:before + 1], pltpu.roll(um, 1, 0))
        un = jnp.where(row == rb - 1, u_ref[br, r0 + rb:r0 + rb + 1], pltpu.roll(um, rb - 1, 0))
        c = c_ref[...]
        return up * c[0:1] + um * c[1:2] + un * c[2:3] + c[3:4]

    def step(produce, consume):
        h = ext_ref[...]
        p_ref, c_ref = u_refs[produce], u_refs[consume]
        p_ref[0] = jnp.dot(h, wg_ref[...], preferred_element_type=F32)
        p_ref[1] = jnp.dot(h, wv_ref[...], preferred_element_type=F32)
        for r0 in range(0, tm, rb):
            gate = conv_rows(c_ref, 0, r0, cg_ref)
            val = conv_rows(c_ref, 1, r0, cv_ref)
            act = gate * jax.nn.sigmoid(gate) * val
            act = jnp.where(f > 0, act, jnp.zeros_like(act)).astype(BF16)
            for n in range(d // tn_d):
                sl = slice(n * tn_d, (n + 1) * tn_d)
                o_ref[r0:r0 + rb, sl] += jnp.dot(act, wd_ref[:, sl], preferred_element_type=F32)

    @pl.when(f % 2 == 0)
    def _():
        step(0, 1)

    @pl.when(f % 2 == 1)
    def _():
        step(1, 0)

    @pl.when(f == nf)
    def _():
        o_ref[...] = x_ref[...] + g_ref[...] * o_ref[...]


def _conv_ffn(h, w_up, conv_tab, w_down, x, gate, tm=512, tf=256, tn_d=1024, row_blocks=1):
    m, d = h.shape
    ff = w_down.shape[0]
    tm = min(tm, m)
    tn_d = min(tn_d, d)
    n_mtiles, nf = m // tm, ff // tf
    hb = tm // FFN_HALO
    last_hb = m // FFN_HALO - 1
    kern = functools.partial(_ffn_kernel, tm=tm, n_mtiles=n_mtiles, nf=nf, tn_d=tn_d, row_blocks=row_blocks)

    def prod(f):
        return jnp.minimum(f, nf - 1)

    def cons(f):
        return jnp.maximum(f - 1, 0)

    return pl.pallas_call(
        kern,
        grid=(n_mtiles, nf + 1),
        in_specs=[
            pl.BlockSpec((tm, d), lambda i, f: (i, 0), pipeline_mode=pl.Buffered(1)),
            pl.BlockSpec((FFN_HALO, d), lambda i, f: (jnp.maximum(i * hb - 1, 0), 0)),
            pl.BlockSpec((FFN_HALO, d), lambda i, f: (jnp.minimum((i + 1) * hb, last_hb), 0)),
            pl.BlockSpec((d, tf), lambda i, f: (0, prod(f))),
            pl.BlockSpec((d, tf), lambda i, f: (0, prod(f) + nf)),
            pl.BlockSpec((8, tf), lambda i, f: (0, cons(f))),
            pl.BlockSpec((8, tf), lambda i, f: (0, cons(f) + nf)),
            pl.BlockSpec((tf, d), lambda i, f: (cons(f), 0)),
            pl.BlockSpec((tm, d), lambda i, f: (i, 0), pipeline_mode=pl.Buffered(1)),
            pl.BlockSpec((1, d), lambda i, f: (0, 0)),
        ],
        out_specs=pl.BlockSpec((tm, d), lambda i, f: (i, 0)),
        out_shape=jax.ShapeDtypeStruct((m, d), F32),
        scratch_shapes=[pltpu.VMEM((tm + 2 * FFN_HALO, d), BF16),
                        pltpu.VMEM((2, tm + 2 * FFN_HALO, tf), F32),
                        pltpu.VMEM((2, tm + 2 * FFN_HALO, tf), F32)],
        compiler_params=_params("arbitrary", "arbitrary"),
        name="conv_ffn",
    )(h, h, h, w_up, w_up, conv_tab, conv_tab, w_down, x, gate.reshape(1, d))


def _rope_tables(length):
    t = jnp.arange(length)
    half = NA_HEAD_DIM // 2
    inv = ROPE_BASE ** (-jnp.arange(0, half, 2, dtype=F32) / half)
    ar = (t // GRID_W).astype(F32)[:, None] * inv
    ac = (t % GRID_W).astype(F32)[:, None] * inv
    ang = jnp.concatenate([ar, ar, ac, ac], axis=-1)
    cos, sin = jnp.cos(ang), jnp.sin(ang)
    first = (jnp.arange(NA_HEAD_DIM) % half) < half // 2
    return cos, jnp.where(first, -sin, 0.0), jnp.where(first, 0.0, sin)


def _qk_prep_kernel(q_ref, k_ref, qw_ref, kw_ref, cos_ref, sa_ref, sb_ref, qo_ref, ko_ref):
    cos, sa, sb = cos_ref[...], sa_ref[...], sb_ref[...]
    quarter = NA_HEAD_DIM // 4
    for h in range(NA_HEADS):
        sl = slice(h * NA_HEAD_DIM, (h + 1) * NA_HEAD_DIM)
        for src, w_ref, dst in ((q_ref, qw_ref, qo_ref), (k_ref, kw_ref, ko_ref)):
            x = src[:, sl].astype(F32)
            y = x * lax.rsqrt(jnp.mean(x * x, axis=-1, keepdims=True) + RMS_EPS) * w_ref[...]
            y = (y * cos + pltpu.roll(y, NA_HEAD_DIM - quarter, 1) * sa + pltpu.roll(y, quarter, 1) * sb)
            dst[:, sl] = y.astype(dst.dtype)


def _qk_prep(p, q_blk, k_blk, qw, kw, tables, tm=512):
    m = p.shape[0]
    tm = min(tm, m)
    tab = pl.BlockSpec((tm, NA_HEAD_DIM), lambda i: (i, 0))
    vec = pl.BlockSpec((1, NA_HEAD_DIM), lambda i: (0, 0))
    out = jax.ShapeDtypeStruct((m, NA_WIDTH), BF16)
    return pl.pallas_call(
        _qk_prep_kernel,
        grid=(m // tm,),
        in_specs=[pl.BlockSpec((tm, NA_WIDTH), lambda i: (i, q_blk)),
                  pl.BlockSpec((tm, NA_WIDTH), lambda i: (i, k_blk)), vec, vec, tab, tab, tab],
        out_specs=[pl.BlockSpec((tm, NA_WIDTH), lambda i: (i, 0))] * 2,
        out_shape=[out, out],
        compiler_params=_params("arbitrary"),
        name="qk_prep",
    )(p, p, qw.reshape(1, -1), kw.reshape(1, -1), *tables)


NA_DR = 2 * NA_WIN_R
NA_BIAS_CHUNK = 1024


def _na_bias_kernel(r0_ref, r1_ref, o_ref):
    def pieces(r):
        out = []
        for _ in range(3):
            piece = r.astype(BF16)
            out.append(piece)
            r = r - piece.astype(F32)
        return out

    p0, p1 = pieces(r0_ref[0]), pieces(r1_ref[0])
    for c in range(GRID_W * LANE // NA_BIAS_CHUNK):
        shape = (LANE, NA_BIAS_CHUNK)
        pos = c * NA_BIAS_CHUNK + lax.broadcasted_iota(jnp.int32, shape, 1)
        j = lax.broadcasted_iota(jnp.int32, shape, 0)
        qc = pos // LANE
        lane = pos % LANE
        kc = lane % GRID_W
        idx = jnp.clip(kc - qc + (NA_WIN_C - 1), 0, 2 * NA_WIN_C - 2)
        hit = idx == j
        second = lane >= GRID_W
        oh0 = jnp.where(hit, jnp.where(second, 0.0, 1.0), 0.0).astype(BF16)
        oh1 = jnp.where(hit, jnp.where(second, 1.0, 0.0), 0.0).astype(BF16)
        acc = jnp.zeros((NA_DR, NA_BIAS_CHUNK), F32)
        for a, b in zip(p0, p1):
            acc = acc + jnp.dot(a, oh0, preferred_element_type=F32) + jnp.dot(b, oh1, preferred_element_type=F32)
        c0 = jnp.clip(qc[0:1] - NA_WIN_C // 2, 0, GRID_W - NA_WIN_C)
        ok = (kc[0:1] >= c0) & (kc[0:1] < c0 + NA_WIN_C)
        o_ref[0, :, c * NA_BIAS_CHUNK:(c + 1) * NA_BIAS_CHUNK] = jnp.where(ok, acc, -jnp.inf)


def _na_bias_table(rpb):
    nh, ndr, ndc = rpb.shape
    r0 = jnp.pad(rpb, ((0, 0), (0, NA_DR - ndr), (0, LANE - ndc)))
    r1 = jnp.pad(rpb[:, 1:], ((0, 0), (0, NA_DR - ndr + 1), (0, LANE - ndc)))
    spec = pl.BlockSpec((1, NA_DR, LANE), lambda h: (h, 0, 0))
    out = pl.pallas_call(
        _na_bias_kernel,
        grid=(nh,),
        in_specs=[spec, spec],
        out_specs=pl.BlockSpec((1, NA_DR, GRID_W * LANE), lambda h: (h, 0, 0)),
        out_shape=jax.ShapeDtypeStruct((nh, NA_DR, GRID_W * LANE), F32),
        compiler_params=_params("arbitrary"),
        name="na_bias",
    )(r0, r1)
    return out.reshape(nh, NA_DR, GRID_W, LANE)


def _softmax_parts(scores):
    m = functools.reduce(jnp.maximum, [jnp.max(s, axis=1, keepdims=True) for s in scores])
    ps = [jnp.exp(s - m) for s in scores]
    den = functools.reduce(jnp.add, [jnp.sum(p, axis=1, keepdims=True) for p in ps])
    return [p.astype(BF16) for p in ps], den


def _pv(parts, values):
    ps, den = parts
    o = functools.reduce(jnp.add, [jnp.dot(p, v, preferred_element_type=F32) for p, v in zip(ps, values)])
    return o / den


def _softmax_pv(scores, values):
    return _pv(_softmax_parts(scores), values)


_NT = (((1,), (1,)), ((), ()))


def _na_kernel(q_ref, k_ref, v_ref, kc_ref, vc_ref, tz_ref, o_ref, *, rows_per_step, n_rows):
    t = pl.program_id(1)
    scale = NA_HEAD_DIM ** -0.5
    band = NA_WIN_R * GRID_W
    kc, vc = kc_ref[...], vc_ref[...]
    starts, scores = [], []
    for i in range(rows_per_step):
        qr = t * rows_per_step + i
        bs = jnp.clip(qr - NA_WIN_R // 2, 0, n_rows - NA_WIN_R)
        off = bs - qr + (NA_WIN_R - 1)
        start = pl.multiple_of(bs * GRID_W, GRID_W)
        q = q_ref[i * GRID_W:(i + 1) * GRID_W, :]
        kb = k_ref[pl.ds(start, band), :]
        bias = jnp.concatenate([tz_ref[0, off + 2 * w] for w in range(NA_WIN_R // 2)], axis=1)
        s_loc = lax.dot_general(q, kb, _NT, preferred_element_type=F32) * scale + bias
        s_ctx = lax.dot_general(q, kc, _NT, preferred_element_type=F32) * scale
        starts.append(start)
        scores.append((s_loc, s_ctx))
    probs = [_softmax_parts(list(s)) for s in scores]
    for i in range(rows_per_step):
        vb = v_ref[pl.ds(starts[i], band), :]
        o = _pv(probs[i], [vb, vc])
        o_ref[i * GRID_W:(i + 1) * GRID_W, :] = o.astype(o_ref.dtype)


def _na_latent(q_r, k_r, p_l, v_blk, kc_r, p_c, vc_blk, tz, rows_per_step=8):
    length = q_r.shape[0]
    lc = kc_r.shape[0]
    n_rows = length // GRID_W
    assert n_rows >= NA_WIN_R and n_rows % rows_per_step == 0
    hd = NA_HEAD_DIM
    kern = functools.partial(_na_kernel, rows_per_step=rows_per_step, n_rows=n_rows)
    return pl.pallas_call(
        kern,
        grid=(NA_HEADS, n_rows // rows_per_step),
        in_specs=[pl.BlockSpec((rows_per_step * GRID_W, hd), lambda h, t: (t, h)),
                  pl.BlockSpec((length, hd), lambda h, t: (0, h)),
                  pl.BlockSpec((length, hd), lambda h, t: (0, v_blk + h)),
                  pl.BlockSpec((lc, hd), lambda h, t: (0, h)),
                  pl.BlockSpec((lc, hd), lambda h, t: (0, vc_blk + h)),
                  pl.BlockSpec((1, NA_DR, GRID_W, LANE), lambda h, t: (h, 0, 0, 0))],
        out_specs=pl.BlockSpec((rows_per_step * GRID_W, hd), lambda h, t: (t, h)),
        out_shape=jax.ShapeDtypeStruct((length, NA_WIDTH), BF16),
        compiler_params=_params("arbitrary", "arbitrary"),
        name="na_latent",
    )(q_r, k_r, p_l, kc_r, p_c, tz)


def _ctx_attn_kernel(q_ref, k_ref, v_ref, o_ref):
    s = lax.dot_general(q_ref[...], k_ref[...], _NT, preferred_element_type=F32) * (NA_HEAD_DIM ** -0.5)
    o_ref[...] = _softmax_pv([s], [v_ref[...]]).astype(o_ref.dtype)


def _na_context(qc_r, kc_r, p_c, vc_blk):
    lc = qc_r.shape[0]
    hd = NA_HEAD_DIM
    spec = pl.BlockSpec((lc, hd), lambda h: (0, h))
    return pl.pallas_call(
        _ctx_attn_kernel,
        grid=(NA_HEADS,),
        in_specs=[spec, spec, pl.BlockSpec((lc, hd), lambda h: (0, vc_blk + h))],
        out_specs=spec,
        out_shape=jax.ShapeDtypeStruct((lc, NA_WIDTH), BF16),
        compiler_params=_params("arbitrary"),
        name="na_context",
    )(qc_r, kc_r, p_c)


POOL_HALO = BF16_SUBLANE


def _pool_kernel(u_ref, up_ref, un_ref, w_ref, sc_ref, o_ref, ext_ref, *, tm, n_tiles, length):
    i = pl.program_id(0)
    prv, nxt = up_ref[...], un_ref[...]
    ext_ref[0:POOL_HALO, :] = jnp.where(i == 0, jnp.zeros_like(prv), prv).astype(F32)
    ext_ref[POOL_HALO:POOL_HALO + tm, :] = u_ref[...].astype(F32)
    ext_ref[POOL_HALO + tm:POOL_HALO + tm + POOL_HALO, :] = jnp.where(i == n_tiles - 1, jnp.zeros_like(nxt),
                                                                     nxt).astype(F32)
    t = i * tm + lax.broadcasted_iota(jnp.int32, (tm, 1), 0)
    for g, w in enumerate(POOL_WINDOWS):
        cs = slice(g * POOL_GROUP, (g + 1) * POOL_GROUP)
        acc = ext_ref[pl.ds(POOL_HALO - w // 2, tm), cs]
        for k in range(1, w):
            acc = acc + ext_ref[pl.ds(POOL_HALO - w // 2 + k, tm), cs]
        cnt = (jnp.minimum(t + w // 2, length) - jnp.maximum(t - w // 2, 0)).astype(F32)
        pooled = acc / cnt - ext_ref[pl.ds(POOL_HALO, tm), cs]
        y = jnp.dot(pooled.astype(BF16), w_ref[g], preferred_element_type=F32) * sc_ref[:, cs]
        o_ref[:, cs] = y.astype(o_ref.dtype)


def _pool(p, u_blk, pool_w, pool_scale, tm=512):
    length = p.shape[0]
    tm = min(tm, length)
    n_tiles = length // tm
    hb = tm // POOL_HALO
    last_hb = length // POOL_HALO - 1
    kern = functools.partial(_pool_kernel, tm=tm, n_tiles=n_tiles, length=length)
    return pl.pallas_call(
        kern,
        grid=(n_tiles,),
        in_specs=[pl.BlockSpec((tm, POOL_WIDTH), lambda i: (i, u_blk)),
                  pl.BlockSpec((POOL_HALO, POOL_WIDTH), lambda i: (jnp.maximum(i * hb - 1, 0), u_blk)),
                  pl.BlockSpec((POOL_HALO, POOL_WIDTH), lambda i: (jnp.minimum((i + 1) * hb, last_hb), u_blk)),
                  pl.BlockSpec((len(POOL_WINDOWS), POOL_GROUP, POOL_GROUP), lambda i: (0, 0, 0)),
                  pl.BlockSpec((1, POOL_WIDTH), lambda i: (0, 0))],
        out_specs=pl.BlockSpec((tm, POOL_WIDTH), lambda i: (i, 0)),
        out_shape=jax.ShapeDtypeStruct((length, POOL_WIDTH), BF16),
        scratch_shapes=[pltpu.VMEM((tm + 2 * POOL_HALO, POOL_WIDTH), F32)],
        compiler_params=_params("arbitrary"),
        name="pool",
    )(p, p, p, pool_w.astype(BF16), pool_scale.reshape(1, POOL_WIDTH))


SSD_HALO = BF16_SUBLANE
SSD_CONV_BLK = 1024
SSD_PAIRS = SSD_HEADS // 2


def _ssd_conv_kernel(u_ref, up_ref, un_ref, tab_ref, o_ref, ext_ref, *, tm, n_tiles):
    i = pl.program_id(0)
    prv, nxt = up_ref[...], un_ref[...]
    ext_ref[0:SSD_HALO, :] = jnp.where(i == 0, jnp.zeros_like(prv), prv).astype(F32)
    ext_ref[SSD_HALO:SSD_HALO + tm, :] = u_ref[...].astype(F32)
    ext_ref[SSD_HALO + tm:SSD_HALO + tm + SSD_HALO, :] = jnp.where(i == n_tiles - 1, jnp.zeros_like(nxt),
                                                                   nxt).astype(F32)
    tab = tab_ref[...]
    left = (SSD_CONV - 1) // 2
    acc = tab[SSD_CONV:SSD_CONV + 1]
    for k in range(SSD_CONV):
        acc = acc + ext_ref[pl.ds(SSD_HALO - left + k, tm), :] * tab[k:k + 1]
    o_ref[...] = (acc * jax.nn.sigmoid(acc)).astype(o_ref.dtype)


def _ssd_conv(p, blk0, conv_w, conv_b, tm=512):
    length = p.shape[0]
    tm = min(tm, length)
    n_tiles = length // tm
    hb = tm // SSD_HALO
    last_hb = length // SSD_HALO - 1
    cb = SSD_CONV_BLK
    tab = jnp.concatenate([conv_w, conv_b[None], jnp.zeros((8 - SSD_CONV - 1, SSD_XBC), F32)], axis=0)
    kern = functools.partial(_ssd_conv_kernel, tm=tm, n_tiles=n_tiles)
    return pl.pallas_call(
        kern,
        grid=(n_tiles, SSD_XBC // cb),
        in_specs=[pl.BlockSpec((tm, cb), lambda i, j: (i, blk0 + j)),
                  pl.BlockSpec((SSD_HALO, cb), lambda i, j: (jnp.maximum(i * hb - 1, 0), blk0 + j)),
                  pl.BlockSpec((SSD_HALO, cb), lambda i, j: (jnp.minimum((i + 1) * hb, last_hb), blk0 + j)),
                  pl.BlockSpec((8, cb), lambda i, j: (0, j))],
        out_specs=pl.BlockSpec((tm, cb), lambda i, j: (i, j)),
        out_shape=jax.ShapeDtypeStruct((length, SSD_XBC), BF16),
        scratch_shapes=[pltpu.VMEM((tm + 2 * SSD_HALO, cb), F32)],
        compiler_params=_params("arbitrary", "arbitrary"),
        name="ssd_conv",
    )(p, p, p, tab)


def _ssd_dt_kernel(raw_ref, bias_ref, alog_ref, dt_ref, da_ref):
    v = raw_ref[...] + bias_ref[...]
    dt = jnp.maximum(v, 0.0) + jnp.log1p(jnp.exp(-jnp.abs(v)))
    dt_ref[...] = dt
    da_ref[...] = dt * -jnp.exp(alog_ref[...])


def _ssd_dt(raw, dt_bias, a_log, tm=2048):
    length = raw.shape[0]
    tm = min(tm, length)
    pad = LANE - 2 * SSD_HEADS
    spec = pl.BlockSpec((tm, LANE), lambda i: (i, 0))
    vec = pl.BlockSpec((1, LANE), lambda i: (0, 0))
    out = jax.ShapeDtypeStruct((length, LANE), F32)
    return pl.pallas_call(
        _ssd_dt_kernel,
        grid=(length // tm,),
        in_specs=[spec, vec, vec],
        out_specs=[spec, spec],
        out_shape=[out, out],
        compiler_params=_params("arbitrary"),
        name="ssd_dt",
    )(raw, jnp.pad(dt_bias.reshape(1, -1), ((0, 0), (0, pad))), jnp.pad(a_log.reshape(1, -1), ((0, 0), (0, pad))))


def _pieces(a):
    out = []
    for _ in range(3):
        piece = a.astype(BF16)
        out.append(piece)
        a = a - piece.astype(F32)
    return out


def _ssd_scan_kernel(*refs, direction, reverse, finalize):
    if finalize:
        (x_ref, b_ref, c_ref, dac_ref, dtr_ref, dar_ref, h0_ref, yp_ref, z_ref, dsk_ref, nw_ref,
         y_ref, hn_ref, s_ref, yacc_ref) = refs
    else:
        x_ref, b_ref, c_ref, dac_ref, dtr_ref, dar_ref, h0_ref, y_ref, hn_ref, s_ref = refs
        yacc_ref = y_ref
    step = pl.program_id(0)
    nsteps = pl.num_programs(0)
    t_len = SSD_CHUNK

    @pl.when(step == 0)
    def _():
        s_ref[...] = h0_ref[...]

    row = lax.broadcasted_iota(jnp.int32, (t_len, t_len), 0)
    col = lax.broadcasted_iota(jnp.int32, (t_len, t_len), 1)
    mask = (col >= row) if reverse else (col <= row)
    mask_t = (row >= col) if reverse else (row <= col)
    ones = jnp.ones((t_len, t_len), BF16)
    tri = jnp.where(mask, 1.0, 0.0).astype(BF16)
    tri_t = jnp.where(mask_t, 1.0, 0.0).astype(BF16)
    hsl = slice(direction * SSD_HEADS, (direction + 1) * SSD_HEADS)

    g_col = sum(jnp.dot(tri, piece, preferred_element_type=F32) for piece in _pieces(dac_ref[...]))
    da_row = dar_ref[hsl, :]
    da_row_p = _pieces(da_row)
    g_row = sum(jnp.dot(piece, tri_t, preferred_element_type=F32) for piece in da_row_p)
    tot_row = sum(jnp.dot(piece, ones, preferred_element_type=F32) for piece in da_row_p)
    dt_row = dtr_ref[hsl, :]
    wdt_row = jnp.exp(tot_row - g_row) * dt_row
    lane = lax.broadcasted_iota(jnp.int32, (t_len, LANE), 1)
    first = lane < SSD_HEAD_DIM
    edge = slice(0, 1) if reverse else slice(t_len - 1, t_len)

    for grp in range(SSD_GROUPS):
        nsl = slice(grp * SSD_STATE, (grp + 1) * SSD_STATE)
        b_g, c_g = b_ref[:, nsl], c_ref[:, nsl]
        cb = lax.dot_general(c_g, b_g, _NT, preferred_element_type=F32)
        bt_g = b_g.astype(F32).T
        for k in range(grp * SSD_HPG // 2, (grp + 1) * SSD_HPG // 2):
            ms, es, bws = [], [], []
            for h in (2 * k, 2 * k + 1):
                j = direction * SSD_HEADS + h
                g_bc = jnp.broadcast_to(g_col[:, j:j + 1], (t_len, t_len))
                decay = jnp.where(mask, jnp.exp(g_bc - g_row[h:h + 1, :]), 0.0)
                ms.append((decay * cb * dt_row[h:h + 1, :]).astype(BF16))
                es.append(jnp.exp(g_bc))
                bws.append((bt_g * wdt_row[h:h + 1, :]).astype(BF16))
            csl = slice(k * LANE, (k + 1) * LANE)
            xp = x_ref[:, csl]
            xa = jnp.where(first, xp, jnp.zeros_like(xp))
            xb = jnp.where(first, jnp.zeros_like(xp), xp)
            e_pair = jnp.where(first, es[0], es[1])
            s_old = s_ref[k]
            y = (jnp.dot(ms[0], xa, preferred_element_type=F32) + jnp.dot(ms[1], xb, preferred_element_type=F32)
                 + jnp.dot(c_g, s_old.astype(BF16), preferred_element_type=F32) * e_pair)
            s_ref[k] = (s_old * e_pair[edge] + jnp.dot(bws[0], xa, preferred_element_type=F32)
                        + jnp.dot(bws[1], xb, preferred_element_type=F32))
            yacc_ref[:, csl] = y

    @pl.when(step == nsteps - 1)
    def _():
        hn_ref[...] = s_ref[...]

    if finalize:
        y = yacc_ref[...] + yp_ref[...] + x_ref[...].astype(F32) * dsk_ref[...]
        zz = z_ref[...].astype(F32)
        gated = y * (zz * jax.nn.sigmoid(zz))
        out = gated * lax.rsqrt(jnp.mean(gated * gated, axis=-1, keepdims=True) + RMS_EPS) * nw_ref[...]
        y_ref[...] = out.astype(y_ref.dtype)


def _ssd_scan(xbc, da_col, dt_row, da_row, h0, direction, final=None):
    length = xbc.shape[0]
    t_len = SSD_CHUNK
    n_chunks = length // t_len
    reverse = direction == 1
    pos = (lambda c: n_chunks - 1 - c) if reverse else (lambda c: c)
    di, gn = SSD_D_INNER, SSD_GN
    state = jax.ShapeDtypeStruct((SSD_PAIRS, SSD_STATE, LANE), F32)
    state_spec = pl.BlockSpec((SSD_PAIRS, SSD_STATE, LANE), lambda c: (0, 0, 0))
    in_specs = [pl.BlockSpec((t_len, di), lambda c: (pos(c), 0)),
                pl.BlockSpec((t_len, gn), lambda c: (pos(c), di // gn)),
                pl.BlockSpec((t_len, gn), lambda c: (pos(c), di // gn + 1)),
                pl.BlockSpec((t_len, LANE), lambda c: (pos(c), 0)),
                pl.BlockSpec((2 * SSD_HEADS, t_len), lambda c: (0, pos(c))),
                pl.BlockSpec((2 * SSD_HEADS, t_len), lambda c: (0, pos(c))),
                state_spec]
    args = [xbc, xbc, xbc, da_col, dt_row, da_row, h0]
    scratch = [pltpu.VMEM((SSD_PAIRS, SSD_STATE, LANE), F32)]
    y_spec = pl.BlockSpec((t_len, di), lambda c: (pos(c), 0))
    row_spec = pl.BlockSpec((1, di), lambda c: (0, 0))
    if final is not None:
        y_other, p, d_skip_row, norm_w_row = final
        in_specs += [y_spec, y_spec, row_spec, row_spec]
        args += [y_other, p, d_skip_row, norm_w_row]
        scratch.append(pltpu.VMEM((t_len, di), F32))
    y_dtype = BF16 if final is not None else F32
    kern = functools.partial(_ssd_scan_kernel, direction=direction, reverse=reverse, finalize=final is not None)
    return pl.pallas_call(
        kern,
        grid=(n_chunks,),
        in_specs=in_specs,
        out_specs=[y_spec, state_spec],
        out_shape=[jax.ShapeDtypeStruct((length, di), y_dtype), state],
        scratch_shapes=scratch,
        compiler_params=_params("arbitrary"),
        name="ssd_scan",
    )(*args)


def _ssd_inputs(p, dt_raw, conv_w, conv_b, dt_bias, a_log):
    xbc = _ssd_conv(p, P_XBC // SSD_CONV_BLK, conv_w, conv_b)
    dt, da = _ssd_dt(dt_raw, dt_bias, a_log)
    nh = 2 * SSD_HEADS
    return xbc, da, dt[:, :nh].T, da[:, :nh].T


def _ssd_pallas(p_l, dt_raw_l, p_c, dt_raw_c, conv_w, conv_b, dt_bias, a_log, d_skip, norm_w, ctx_out):
    xbc_l, dac_l, dtr_l, dar_l = _ssd_inputs(p_l, dt_raw_l, conv_w, conv_b, dt_bias, a_log)
    xbc_c, dac_c, dtr_c, dar_c = _ssd_inputs(p_c, dt_raw_c, conv_w, conv_b, dt_bias, a_log)
    zero = jnp.zeros((SSD_PAIRS, SSD_STATE, LANE), F32)
    fin = (jnp.repeat(d_skip, SSD_HEAD_DIM).reshape(1, -1), norm_w.reshape(1, -1))
    yc0, h_fwd = _ssd_scan(xbc_c, dac_c, dtr_c, dar_c, zero, 0)
    if ctx_out:
        out_c, h_bwd = _ssd_scan(xbc_c, dac_c, dtr_c, dar_c, zero, 1, final=(yc0, p_c) + fin)
    else:
        out_c = None
        _, h_bwd = _ssd_scan(xbc_c, dac_c, dtr_c, dar_c, zero, 1)
    yl0, _ = _ssd_scan(xbc_l, dac_l, dtr_l, dar_l, h_fwd, 0)
    out_l, _ = _ssd_scan(xbc_l, dac_l, dtr_l, dar_l, h_bwd, 1, final=(yl0, p_l) + fin)
    return out_l, out_c


P_Z = 0
P_Q = P_Z + SSD_D_INNER
P_POOL = P_Q + NA_WIDTH
P_XBC = P_POOL + POOL_WIDTH
P_K = P_XBC + SSD_XBC
P_V = P_K + NA_WIDTH
P_COLS = P_V + NA_WIDTH


def _layer(x, ctx, mod_l, mod_c, norm1_w, w_in, ssd_conv_w, ssd_conv_b, ssd_dt_bias, ssd_a_log, ssd_d,
           ssd_norm_w, na_q_norm_w, na_k_norm_w, na_rpb, pool_w, pool_scale, w_out, norm2_w, mlp_w_up,
           mlp_conv_w, mlp_conv_b, mlp_w_down, rope_l, rope_c, ctx_out):
    sh1_l, sc1_l, g1_l, sh2_l, sc2_l, g2_l = jnp.split(mod_l, N_MOD)
    sh1_c, sc1_c, g1_c, sh2_c, sc2_c, g2_c = jnp.split(mod_c, N_MOD)

    w_main = jnp.concatenate([w_in[:, :OFF_DT], w_in[:, OFF_K:]], axis=1).astype(BF16)
    w_dt = jnp.pad(w_in[:, OFF_DT:OFF_K], ((0, 0), (0, LANE - 2 * SSD_HEADS))).astype(BF16)
    w_out_b = w_out.astype(BF16)
    w_up_b = mlp_w_up.astype(BF16)
    w_down_b = mlp_w_down.astype(BF16)
    conv_tab = jnp.concatenate([mlp_conv_w, mlp_conv_b[None], jnp.zeros((4, 2 * D_FF), F32)], axis=0)

    h_l = _norm_mod(x, norm1_w, sc1_l, sh1_l)
    h_c = _norm_mod(ctx, norm1_w, sc1_c, sh1_c)
    p_l = _matmul(h_l, w_main, BF16)
    dt_l = _matmul(h_l, w_dt, F32)
    p_c = _matmul(h_c, w_main, BF16)
    dt_c = _matmul(h_c, w_dt, F32)

    y_ssd_l, y_ssd_c = _ssd_pallas(p_l, dt_l, p_c, dt_c, ssd_conv_w, ssd_conv_b, ssd_dt_bias, ssd_a_log, ssd_d,
                                   ssd_norm_w, ctx_out)
    q_blk, k_blk, v_blk = P_Q // NA_WIDTH, P_K // NA_WIDTH, P_V // NA_HEAD_DIM
    q_r, k_r = _qk_prep(p_l, q_blk, k_blk, na_q_norm_w, na_k_norm_w, rope_l)
    qc_r, kc_r = _qk_prep(p_c, q_blk, k_blk, na_q_norm_w, na_k_norm_w, rope_c)
    y_na_l = _na_latent(q_r, k_r, p_l, v_blk, kc_r, p_c, v_blk, _na_bias_table(na_rpb))
    y_pool_l = _pool(p_l, P_POOL // POOL_WIDTH, pool_w, pool_scale)

    x_mid = _outproj([y_ssd_l, y_na_l, y_pool_l], w_out_b, x, g1_l)
    h2_l = _norm_mod(x_mid, norm2_w, sc2_l, sh2_l)
    x_new = _conv_ffn(h2_l, w_up_b, conv_tab, w_down_b, x_mid, g2_l)
    ctx_new = ctx
    if ctx_out:
        y_na_c = _na_context(qc_r, kc_r, p_c, v_blk)
        y_pool_c = _pool(p_c, P_POOL // POOL_WIDTH, pool_w, pool_scale)
        c_mid = _outproj([y_ssd_c, y_na_c, y_pool_c], w_out_b, ctx, g1_c)
        h2_c = _norm_mod(c_mid, norm2_w, sc2_c, sh2_c)
        ctx_new = _conv_ffn(h2_c, w_up_b, conv_tab, w_down_b, c_mid, g2_c)
    return x_new, ctx_new


def kernel(x, c, ctx, c_ctx, ada_w, ada_b, norm1_w, w_in, ssd_conv_w, ssd_conv_b, ssd_dt_bias, ssd_a_log,
           ssd_d, ssd_norm_w, na_q_norm_w, na_k_norm_w, na_rpb, pool_w, pool_scale, w_out, norm2_w,
           mlp_w_up, mlp_conv_w, mlp_conv_b, mlp_w_down):
    depth = ada_w.shape[0]
    d = x.shape[-1]
    c_rows = jnp.concatenate([c.reshape(1, d), c_ctx.reshape(1, d), jnp.zeros((MOD_ROWS - 2, d), F32)], axis=0)
    mods = _modulation(c_rows, ada_w, ada_b)
    xs, cs = x[0], ctx[0]
    rope_l = _rope_tables(xs.shape[0])
    ctx_tab = (cs.shape[0], NA_HEAD_DIM)
    rope_c = (jnp.ones(ctx_tab, F32), jnp.zeros(ctx_tab, F32), jnp.zeros(ctx_tab, F32))
    for i in range(depth):
        xs, cs = _layer(xs, cs, mods[i, 0], mods[i, 1], norm1_w[i], w_in[i], ssd_conv_w[i], ssd_conv_b[i],
                        ssd_dt_bias[i], ssd_a_log[i], ssd_d[i], ssd_norm_w[i], na_q_norm_w[i], na_k_norm_w[i],
                        na_rpb[i], pool_w[i], pool_scale[i], w_out[i], norm2_w[i], mlp_w_up[i], mlp_conv_w[i],
                        mlp_conv_b[i], mlp_w_down[i], rope_l, rope_c, ctx_out=(i < depth - 1))
    return xs[None]
```

```python
import functools
import math

import jax
import jax.numpy as jnp
from jax import lax
from jax.experimental import pallas as pl
from jax.experimental.pallas import tpu as pltpu

F32 = jnp.float32
BF16 = jnp.bfloat16

D_MODEL = 4096
DEPTH = 2
GRID_W = 64
D_MIX = D_MODEL
SSD_D_INNER = D_MIX // 2
SSD_HEAD_DIM = 64
SSD_HEADS = SSD_D_INNER // SSD_HEAD_DIM
SSD_GROUPS = 4
SSD_HPG = SSD_HEADS // SSD_GROUPS
SSD_STATE = 128
SSD_GN = SSD_GROUPS * SSD_STATE
SSD_XBC = SSD_D_INNER + 2 * SSD_GN
SSD_CONV = 5
SSD_CHUNK = 128
NA_WIDTH = D_MIX // 4
NA_HEAD_DIM = 128
NA_HEADS = NA_WIDTH // NA_HEAD_DIM
NA_WIN_R = 8
NA_WIN_C = 16
ROPE_BASE = 10000.0
POOL_WIDTH = D_MIX - SSD_D_INNER - NA_WIDTH
POOL_WINDOWS = (2, 4, 8, 16)
POOL_GROUP = POOL_WIDTH // len(POOL_WINDOWS)
D_FF = 11008
N_MOD = 6
RMS_EPS = 1e-6
OFF_Z = 0
OFF_Q = OFF_Z + SSD_D_INNER
OFF_POOL = OFF_Q + NA_WIDTH
OFF_XBC = OFF_POOL + POOL_WIDTH
OFF_DT = OFF_XBC + SSD_XBC
OFF_K = OFF_DT + 2 * SSD_HEADS
OFF_V = OFF_K + NA_WIDTH
IN_COLS = OFF_V + NA_WIDTH

VMEM_LIMIT_BYTES = 58 * 1024 * 1024
LANE = 128
BF16_SUBLANE = 16
MOD_ROWS = 8


def _params(*sem):
    return pltpu.CompilerParams(dimension_semantics=sem, vmem_limit_bytes=VMEM_LIMIT_BYTES)


def _mod_kernel(c_ref, w_ref, b_ref, o_ref):
    c = c_ref[...]
    s = (c * jax.nn.sigmoid(c)).astype(BF16)
    o_ref[0] = jnp.dot(s, w_ref[0].astype(BF16), preferred_element_type=F32) + b_ref[0]


def _modulation(c_rows, ada_w, ada_b, tn=1024):
    depth, d, n = ada_w.shape
    return pl.pallas_call(
        _mod_kernel,
        grid=(depth, n // tn),
        in_specs=[pl.BlockSpec((MOD_ROWS, d), lambda l, j: (0, 0)),
                  pl.BlockSpec((1, d, tn), lambda l, j: (l, 0, j)),
                  pl.BlockSpec((1, 1, tn), lambda l, j: (l, 0, j))],
        out_specs=pl.BlockSpec((1, MOD_ROWS, tn), lambda l, j: (l, 0, j)),
        out_shape=jax.ShapeDtypeStruct((depth, MOD_ROWS, n), F32),
        compiler_params=_params("arbitrary", "arbitrary"),
        name="modulation",
    )(c_rows, ada_w, ada_b.reshape(depth, 1, n))


def _norm_kernel(x_ref, w_ref, sc_ref, sh_ref, o_ref):
    x = x_ref[...]
    y = x * lax.rsqrt(jnp.mean(x * x, axis=-1, keepdims=True) + RMS_EPS) * w_ref[...]
    o_ref[...] = (y * (1.0 + sc_ref[...]) + sh_ref[...]).astype(o_ref.dtype)


def _norm_mod(x, w, scale, shift, tm=512):
    m, d = x.shape
    tm = min(tm, m)
    row = pl.BlockSpec((1, d), lambda i: (0, 0))
    return pl.pallas_call(
        _norm_kernel,
        grid=(m // tm,),
        in_specs=[pl.BlockSpec((tm, d), lambda i: (i, 0)), row, row, row],
        out_specs=pl.BlockSpec((tm, d), lambda i: (i, 0)),
        out_shape=jax.ShapeDtypeStruct((m, d), BF16),
        compiler_params=_params("arbitrary"),
        name="norm_mod",
    )(x, w.reshape(1, d), scale.reshape(1, d), shift.reshape(1, d))


def _mm_kernel(a_ref, b_ref, o_ref):
    o_ref[...] = jnp.dot(a_ref[...], b_ref[...], preferred_element_type=F32).astype(o_ref.dtype)


def _matmul(a, b, out_dtype, tm=1024, tn=1024):
    m, k = a.shape
    n = b.shape[1]
    tm, tn = min(tm, m), min(tn, n)
    return pl.pallas_call(
        _mm_kernel,
        grid=(m // tm, n // tn),
        in_specs=[pl.BlockSpec((tm, k), lambda i, j: (i, 0)),
                  pl.BlockSpec((k, tn), lambda i, j: (0, j))],
        out_specs=pl.BlockSpec((tm, tn), lambda i, j: (i, j)),
        out_shape=jax.ShapeDtypeStruct((m, n), out_dtype),
        compiler_params=_params("arbitrary", "arbitrary"),
        name="matmul",
    )(a, b)


def _outproj_kernel(*refs, widths):
    a_refs, (w_ref, x_ref, g_ref, o_ref) = refs[:len(widths)], refs[len(widths):]
    acc, off = None, 0
    for a_ref, k in zip(a_refs, widths):
        part = jnp.dot(a_ref[...], w_ref[off:off + k, :], preferred_element_type=F32)
        acc = part if acc is None else acc + part
        off += k
    o_ref[...] = x_ref[...] + g_ref[...] * acc


def _outproj(parts, w, x, gate, tm=1024, tn=1024):
    m = x.shape[0]
    k, n = w.shape
    tm = min(tm, m)
    widths = tuple(a.shape[1] for a in parts)
    assert sum(widths) == k
    return pl.pallas_call(
        functools.partial(_outproj_kernel, widths=widths),
        grid=(m // tm, n // tn),
        in_specs=[pl.BlockSpec((tm, kw), lambda i, j: (i, 0)) for kw in widths] + [
            pl.BlockSpec((k, tn), lambda i, j: (0, j)),
            pl.BlockSpec((tm, tn), lambda i, j: (i, j)),
            pl.BlockSpec((1, tn), lambda i, j: (0, j))],
        out_specs=pl.BlockSpec((tm, tn), lambda i, j: (i, j)),
        out_shape=jax.ShapeDtypeStruct((m, n), F32),
        compiler_params=_params("arbitrary", "arbitrary"),
        name="outproj",
    )(*parts, w, x, gate.reshape(1, n))


FFN_HALO = BF16_SUBLANE


def _ffn_kernel(h_ref, hp_ref, hn_ref, wg_ref, wv_ref, tab_ref, wd_ref, x_ref, g_ref,
                o_ref, ext_ref, ua_ref, ub_ref, *, tm, n_mtiles, nf, tn_d):
    i = pl.program_id(0)
    f = pl.program_id(1)
    d = o_ref.shape[1]
    u_refs = (ua_ref, ub_ref)

    @pl.when(f == 0)
    def _():
        ext_ref[0:tm, :] = h_ref[...]
        nxt = hn_ref[...]
        prv = hp_ref[...]
        ext_ref[tm:tm + FFN_HALO, :] = jnp.where(i == n_mtiles - 1, jnp.zeros_like(nxt), nxt)
        ext_ref[tm + FFN_HALO:tm + 2 * FFN_HALO, :] = jnp.where(i == 0, jnp.zeros_like(prv), prv)
        o_ref[...] = jnp.zeros_like(o_ref)
        ub_ref[...] = jnp.zeros_like(ub_ref)

    def conv(u_ref, br, c):
        um = u_ref[br, 0:tm]
        row = lax.broadcasted_iota(jnp.int32, um.shape, 0)
        before = tm + 2 * FFN_HALO - 1
        up = jnp.where(row == 0, u_ref[br, before:before + 1], pltpu.roll(um, 1, 0))
        un = jnp.where(row == tm - 1, u_ref[br, tm:tm + 1], pltpu.roll(um, tm - 1, 0))
        return up * c[0:1] + um * c[1:2] + un * c[2:3] + c[3:4]

    def step(produce, consume):
        h = ext_ref[...]
        p_ref, c_ref = u_refs[produce], u_refs[consume]
        p_ref[0] = jnp.dot(h, wg_ref[...], preferred_element_type=F32)
        p_ref[1] = jnp.dot(h, wv_ref[...], preferred_element_type=F32)
        tile = jnp.maximum(f - 1, 0)
        gate = conv(c_ref, 0, tab_ref[tile])
        val = conv(c_ref, 1, tab_ref[tile + nf])
        act = gate * jax.nn.sigmoid(gate) * val
        act = jnp.where(f > 0, act, jnp.zeros_like(act)).astype(BF16)
        for n in range(d // tn_d):
            sl = slice(n * tn_d, (n + 1) * tn_d)
            o_ref[:, sl] += jnp.dot(act, wd_ref[:, sl].astype(BF16), preferred_element_type=F32)

    @pl.when(f % 2 == 0)
    def _():
        step(0, 1)

    @pl.when(f % 2 == 1)
    def _():
        step(1, 0)

    @pl.when(f == nf)
    def _():
        o_ref[...] = x_ref[...] + g_ref[...] * o_ref[...]


def _conv_ffn(h, w_up, conv_tab, w_down, x, gate, tm=512, tf=256, tn_d=1024):
    m, d = h.shape
    ff = w_down.shape[0]
    tm = min(tm, m)
    tn_d = min(tn_d, d)
    n_mtiles, nf = m // tm, ff // tf
    hb = tm // FFN_HALO
    last_hb = m // FFN_HALO - 1
    kern = functools.partial(_ffn_kernel, tm=tm, n_mtiles=n_mtiles, nf=nf, tn_d=tn_d)
    tab = conv_tab.reshape(8, 2 * nf, tf).transpose(1, 0, 2)

    def prod(f):
        return jnp.minimum(f, nf - 1)

    def cons(f):
        return jnp.maximum(f - 1, 0)

    return pl.pallas_call(
        kern,
        grid=(n_mtiles, nf + 1),
        in_specs=[
            pl.BlockSpec((tm, d), lambda i, f: (i, 0), pipeline_mode=pl.Buffered(1)),
            pl.BlockSpec((FFN_HALO, d), lambda i, f: (jnp.maximum(i * hb - 1, 0), 0)),
            pl.BlockSpec((FFN_HALO, d), lambda i, f: (jnp.minimum((i + 1) * hb, last_hb), 0)),
            pl.BlockSpec((d, tf), lambda i, f: (0, prod(f))),
            pl.BlockSpec((d, tf), lambda i, f: (0, prod(f) + nf)),
            pl.BlockSpec((2 * nf, 8, tf), lambda i, f: (0, 0, 0)),
            pl.BlockSpec((tf, d), lambda i, f: (cons(f), 0)),
            pl.BlockSpec((tm, d), lambda i, f: (i, 0), pipeline_mode=pl.Buffered(1)),
            pl.BlockSpec((1, d), lambda i, f: (0, 0)),
        ],
        out_specs=pl.BlockSpec((tm, d), lambda i, f: (i, 0)),
        out_shape=jax.ShapeDtypeStruct((m, d), F32),
        scratch_shapes=[pltpu.VMEM((tm + 2 * FFN_HALO, d), BF16),
                        pltpu.VMEM((2, tm + 2 * FFN_HALO, tf), F32),
                        pltpu.VMEM((2, tm + 2 * FFN_HALO, tf), F32)],
        compiler_params=_params("arbitrary", "arbitrary"),
        name="conv_ffn",
    )(h, h, h, w_up, w_up, tab, w_down, x, gate.reshape(1, d))


def _rope_tables(length):
    t = jnp.arange(length)
    half = NA_HEAD_DIM // 2
    inv = ROPE_BASE ** (-jnp.arange(0, half, 2, dtype=F32) / half)
    ar = (t // GRID_W).astype(F32)[:, None] * inv
    ac = (t % GRID_W).astype(F32)[:, None] * inv
    ang = jnp.concatenate([ar, ar, ac, ac], axis=-1)
    cos, sin = jnp.cos(ang), jnp.sin(ang)
    first = (jnp.arange(NA_HEAD_DIM) % half) < half // 2
    return cos, jnp.where(first, -sin, 0.0), jnp.where(first, 0.0, sin)


def _qk_prep_kernel(q_ref, k_ref, qw_ref, kw_ref, cos_ref, sa_ref, sb_ref, qo_ref, ko_ref):
    cos, sa, sb = cos_ref[...], sa_ref[...], sb_ref[...]
    quarter = NA_HEAD_DIM // 4
    for h in range(NA_HEADS):
        sl = slice(h * NA_HEAD_DIM, (h + 1) * NA_HEAD_DIM)
        for src, w_ref, dst in ((q_ref, qw_ref, qo_ref), (k_ref, kw_ref, ko_ref)):
            x = src[:, sl].astype(F32)
            y = x * lax.rsqrt(jnp.mean(x * x, axis=-1, keepdims=True) + RMS_EPS) * w_ref[...]
            y = (y * cos + pltpu.roll(y, NA_HEAD_DIM - quarter, 1) * sa + pltpu.roll(y, quarter, 1) * sb)
            dst[:, sl] = y.astype(dst.dtype)


def _qk_prep(p, q_blk, k_blk, qw, kw, tables, tm=512):
    m = p.shape[0]
    tm = min(tm, m)
    tab = pl.BlockSpec((tm, NA_HEAD_DIM), lambda i: (i, 0))
    vec = pl.BlockSpec((1, NA_HEAD_DIM), lambda i: (0, 0))
    out = jax.ShapeDtypeStruct((m, NA_WIDTH), BF16)
    return pl.pallas_call(
        _qk_prep_kernel,
        grid=(m // tm,),
        in_specs=[pl.BlockSpec((tm, NA_WIDTH), lambda i: (i, q_blk)),
                  pl.BlockSpec((tm, NA_WIDTH), lambda i: (i, k_blk)), vec, vec, tab, tab, tab],
        out_specs=[pl.BlockSpec((tm, NA_WIDTH), lambda i: (i, 0))] * 2,
        out_shape=[out, out],
        compiler_params=_params("arbitrary"),
        name="qk_prep",
    )(p, p, qw.reshape(1, -1), kw.reshape(1, -1), *tables)


NA_DR = 2 * NA_WIN_R
NA_BIAS_CHUNK = 1024


def _na_bias_kernel(r0_ref, r1_ref, o_ref):
    def pieces(r):
        out = []
        for _ in range(3):
            piece = r.astype(BF16)
            out.append(piece)
            r = r - piece.astype(F32)
        return out

    p0, p1 = pieces(r0_ref[0]), pieces(r1_ref[0])
    for c in range(GRID_W * LANE // NA_BIAS_CHUNK):
        shape = (LANE, NA_BIAS_CHUNK)
        pos = c * NA_BIAS_CHUNK + lax.broadcasted_iota(jnp.int32, shape, 1)
        j = lax.broadcasted_iota(jnp.int32, shape, 0)
        qc = pos // LANE
        lane = pos % LANE
        kc = lane % GRID_W
        idx = jnp.clip(kc - qc + (NA_WIN_C - 1), 0, 2 * NA_WIN_C - 2)
        hit = idx == j
        second = lane >= GRID_W
        oh0 = jnp.where(hit, jnp.where(second, 0.0, 1.0), 0.0).astype(BF16)
        oh1 = jnp.where(hit, jnp.where(second, 1.0, 0.0), 0.0).astype(BF16)
        acc = jnp.zeros((NA_DR, NA_BIAS_CHUNK), F32)
        for a, b in zip(p0, p1):
            acc = acc + jnp.dot(a, oh0, preferred_element_type=F32) + jnp.dot(b, oh1, preferred_element_type=F32)
        c0 = jnp.clip(qc[0:1] - NA_WIN_C // 2, 0, GRID_W - NA_WIN_C)
        ok = (kc[0:1] >= c0) & (kc[0:1] < c0 + NA_WIN_C)
        o_ref[0, :, c * NA_BIAS_CHUNK:(c + 1) * NA_BIAS_CHUNK] = jnp.where(ok, acc, -jnp.inf)


def _na_bias_table(rpb):
    nh, ndr, ndc = rpb.shape
    r0 = jnp.pad(rpb, ((0, 0), (0, NA_DR - ndr), (0, LANE - ndc)))
    r1 = jnp.pad(rpb[:, 1:], ((0, 0), (0, NA_DR - ndr + 1), (0, LANE - ndc)))
    spec = pl.BlockSpec((1, NA_DR, LANE), lambda h: (h, 0, 0))
    out = pl.pallas_call(
        _na_bias_kernel,
        grid=(nh,),
        in_specs=[spec, spec],
        out_specs=pl.BlockSpec((1, NA_DR, GRID_W * LANE), lambda h: (h, 0, 0)),
        out_shape=jax.ShapeDtypeStruct((nh, NA_DR, GRID_W * LANE), F32),
        compiler_params=_params("arbitrary"),
        name="na_bias",
    )(r0, r1)
    return out.reshape(nh, NA_DR, GRID_W, LANE)


def _softmax_parts(scores):
    m = functools.reduce(jnp.maximum, [jnp.max(s, axis=1, keepdims=True) for s in scores])
    ps = [jnp.exp(s - m) for s in scores]
    den = functools.reduce(jnp.add, [jnp.sum(p, axis=1, keepdims=True) for p in ps])
    return [p.astype(BF16) for p in ps], den


def _pv(parts, values):
    ps, den = parts
    o = functools.reduce(jnp.add, [jnp.dot(p, v, preferred_element_type=F32) for p, v in zip(ps, values)])
    return o / den


def _softmax_pv(scores, values):
    return _pv(_softmax_parts(scores), values)


_NT = (((1,), (1,)), ((), ()))


def _na_kernel(q_ref, k_ref, v_ref, kc_ref, vc_ref, tz_ref, o_ref, *, rows_per_step, n_rows):
    t = pl.program_id(1)
    scale = NA_HEAD_DIM ** -0.5
    band = NA_WIN_R * GRID_W
    kc, vc = kc_ref[...], vc_ref[...]
    starts, scores = [], []
    for i in range(rows_per_step):
        qr = t * rows_per_step + i
        bs = jnp.clip(qr - NA_WIN_R // 2, 0, n_rows - NA_WIN_R)
        off = bs - qr + (NA_WIN_R - 1)
        start = pl.multiple_of(bs * GRID_W, GRID_W)
        q = q_ref[i * GRID_W:(i + 1) * GRID_W, :]
        kb = k_ref[pl.ds(start, band), :]
        bias = jnp.concatenate([tz_ref[0, off + 2 * w] for w in range(NA_WIN_R // 2)], axis=1)
        s_loc = lax.dot_general(q, kb, _NT, preferred_element_type=F32) * scale + bias
        s_ctx = lax.dot_general(q, kc, _NT, preferred_element_type=F32) * scale
        starts.append(start)
        scores.append((s_loc, s_ctx))
    probs = [_softmax_parts(list(s)) for s in scores]
    for i in range(rows_per_step):
        vb = v_ref[pl.ds(starts[i], band), :]
        o = _pv(probs[i], [vb, vc])
        o_ref[i * GRID_W:(i + 1) * GRID_W, :] = o.astype(o_ref.dtype)


def _na_latent(q_r, k_r, p_l, v_blk, kc_r, p_c, vc_blk, tz, rows_per_step=8):
    length = q_r.shape[0]
    lc = kc_r.shape[0]
    n_rows = length // GRID_W
    assert n_rows >= NA_WIN_R and n_rows % rows_per_step == 0
    hd = NA_HEAD_DIM
    kern = functools.partial(_na_kernel, rows_per_step=rows_per_step, n_rows=n_rows)
    return pl.pallas_call(
        kern,
        grid=(NA_HEADS, n_rows // rows_per_step),
        in_specs=[pl.BlockSpec((rows_per_step * GRID_W, hd), lambda h, t: (t, h)),
                  pl.BlockSpec((length, hd), lambda h, t: (0, h)),
                  pl.BlockSpec((length, hd), lambda h, t: (0, v_blk + h)),
                  pl.BlockSpec((lc, hd), lambda h, t: (0, h)),
                  pl.BlockSpec((lc, hd), lambda h, t: (0, vc_blk + h)),
                  pl.BlockSpec((1, NA_DR, GRID_W, LANE), lambda h, t: (h, 0, 0, 0))],
        out_specs=pl.BlockSpec((rows_per_step * GRID_W, hd), lambda h, t: (t, h)),
        out_shape=jax.ShapeDtypeStruct((length, NA_WIDTH), BF16),
        compiler_params=_params("arbitrary", "arbitrary"),
        name="na_latent",
    )(q_r, k_r, p_l, kc_r, p_c, tz)


def _ctx_attn_kernel(q_ref, k_ref, v_ref, o_ref):
    s = lax.dot_general(q_ref[...], k_ref[...], _NT, preferred_element_type=F32) * (NA_HEAD_DIM ** -0.5)
    o_ref[...] = _softmax_pv([s], [v_ref[...]]).astype(o_ref.dtype)


def _na_context(qc_r, kc_r, p_c, vc_blk):
    lc = qc_r.shape[0]
    hd = NA_HEAD_DIM
    spec = pl.BlockSpec((lc, hd), lambda h: (0, h))
    return pl.pallas_call(
        _ctx_attn_kernel,
        grid=(NA_HEADS,),
        in_specs=[spec, spec, pl.BlockSpec((lc, hd), lambda h: (0, vc_blk + h))],
        out_specs=spec,
        out_shape=jax.ShapeDtypeStruct((lc, NA_WIDTH), BF16),
        compiler_params=_params("arbitrary"),
        name="na_context",
    )(qc_r, kc_r, p_c)


POOL_HALO = BF16_SUBLANE


def _pool_kernel(u_ref, up_ref, un_ref, w_ref, sc_ref, o_ref, ext_ref, *, tm, n_tiles, length):
    i = pl.program_id(0)
    prv, nxt = up_ref[...], un_ref[...]
    ext_ref[0:POOL_HALO, :] = jnp.where(i == 0, jnp.zeros_like(prv), prv).astype(F32)
    ext_ref[POOL_HALO:POOL_HALO + tm, :] = u_ref[...].astype(F32)
    ext_ref[POOL_HALO + tm:POOL_HALO + tm + POOL_HALO, :] = jnp.where(i == n_tiles - 1, jnp.zeros_like(nxt),
                                                                     nxt).astype(F32)
    t = i * tm + lax.broadcasted_iota(jnp.int32, (tm, 1), 0)
    for g, w in enumerate(POOL_WINDOWS):
        cs = slice(g * POOL_GROUP, (g + 1) * POOL_GROUP)
        acc = ext_ref[pl.ds(POOL_HALO - w // 2, tm), cs]
        for k in range(1, w):
            acc = acc + ext_ref[pl.ds(POOL_HALO - w // 2 + k, tm), cs]
        cnt = (jnp.minimum(t + w // 2, length) - jnp.maximum(t - w // 2, 0)).astype(F32)
        pooled = acc / cnt - ext_ref[pl.ds(POOL_HALO, tm), cs]
        y = jnp.dot(pooled.astype(BF16), w_ref[g], preferred_element_type=F32) * sc_ref[:, cs]
        o_ref[:, cs] = y.astype(o_ref.dtype)


def _pool(p, u_blk, pool_w, pool_scale, tm=512):
    length = p.shape[0]
    tm = min(tm, length)
    n_tiles = length // tm
    hb = tm // POOL_HALO
    last_hb = length // POOL_HALO - 1
    kern = functools.partial(_pool_kernel, tm=tm, n_tiles=n_tiles, length=length)
    return pl.pallas_call(
        kern,
        grid=(n_tiles,),
        in_specs=[pl.BlockSpec((tm, POOL_WIDTH), lambda i: (i, u_blk)),
                  pl.BlockSpec((POOL_HALO, POOL_WIDTH), lambda i: (jnp.maximum(i * hb - 1, 0), u_blk)),
                  pl.BlockSpec((POOL_HALO, POOL_WIDTH), lambda i: (jnp.minimum((i + 1) * hb, last_hb), u_blk)),
                  pl.BlockSpec((len(POOL_WINDOWS), POOL_GROUP, POOL_GROUP), lambda i: (0, 0, 0)),
                  pl.BlockSpec((1, POOL_WIDTH), lambda i: (0, 0))],
        out_specs=pl.BlockSpec((tm, POOL_WIDTH), lambda i: (i, 0)),
        out_shape=jax.ShapeDtypeStruct((length, POOL_WIDTH), BF16),
        scratch_shapes=[pltpu.VMEM((tm + 2 * POOL_HALO, POOL_WIDTH), F32)],
        compiler_params=_params("arbitrary"),
        name="pool",
    )(p, p, p, pool_w.astype(BF16), pool_scale.reshape(1, POOL_WIDTH))


SSD_HALO = BF16_SUBLANE
SSD_CONV_BLK = 1024
SSD_PAIRS = SSD_HEADS // 2


def _ssd_conv_kernel(u_ref, up_ref, un_ref, tab_ref, o_ref, ext_ref, *, tm, n_tiles):
    i = pl.program_id(0)
    prv, nxt = up_ref[...], un_ref[...]
    ext_ref[0:SSD_HALO, :] = jnp.where(i == 0, jnp.zeros_like(prv), prv).astype(F32)
    ext_ref[SSD_HALO:SSD_HALO + tm, :] = u_ref[...].astype(F32)
    ext_ref[SSD_HALO + tm:SSD_HALO + tm + SSD_HALO, :] = jnp.where(i == n_tiles - 1, jnp.zeros_like(nxt),
                                                                   nxt).astype(F32)
    tab = tab_ref[...]
    left = (SSD_CONV - 1) // 2
    acc = tab[SSD_CONV:SSD_CONV + 1]
    for k in range(SSD_CONV):
        acc = acc + ext_ref[pl.ds(SSD_HALO - left + k, tm), :] * tab[k:k + 1]
    o_ref[...] = (acc * jax.nn.sigmoid(acc)).astype(o_ref.dtype)


def _ssd_conv(p, blk0, conv_w, conv_b, tm=512):
    length = p.shape[0]
    tm = min(tm, length)
    n_tiles = length // tm
    hb = tm // SSD_HALO
    last_hb = length // SSD_HALO - 1
    cb = SSD_CONV_BLK
    tab = jnp.concatenate([conv_w, conv_b[None], jnp.zeros((8 - SSD_CONV - 1, SSD_XBC), F32)], axis=0)
    kern = functools.partial(_ssd_conv_kernel, tm=tm, n_tiles=n_tiles)
    return pl.pallas_call(
        kern,
        grid=(n_tiles, SSD_XBC // cb),
        in_specs=[pl.BlockSpec((tm, cb), lambda i, j: (i, blk0 + j)),
                  pl.BlockSpec((SSD_HALO, cb), lambda i, j: (jnp.maximum(i * hb - 1, 0), blk0 + j)),
                  pl.BlockSpec((SSD_HALO, cb), lambda i, j: (jnp.minimum((i + 1) * hb, last_hb), blk0 + j)),
                  pl.BlockSpec((8, cb), lambda i, j: (0, j))],
        out_specs=pl.BlockSpec((tm, cb), lambda i, j: (i, j)),
        out_shape=jax.ShapeDtypeStruct((length, SSD_XBC), BF16),
        scratch_shapes=[pltpu.VMEM((tm + 2 * SSD_HALO, cb), F32)],
        compiler_params=_params("arbitrary", "arbitrary"),
        name="ssd_conv",
    )(p, p, p, tab)


def _ssd_dt_kernel(raw_ref, bias_ref, alog_ref, dt_ref, da_ref):
    v = raw_ref[...] + bias_ref[...]
    dt = jnp.maximum(v, 0.0) + jnp.log1p(jnp.exp(-jnp.abs(v)))
    dt_ref[...] = dt
    da_ref[...] = dt * -jnp.exp(alog_ref[...])


def _ssd_dt(raw, dt_bias, a_log, tm=2048):
    length = raw.shape[0]
    tm = min(tm, length)
    pad = LANE - 2 * SSD_HEADS
    spec = pl.BlockSpec((tm, LANE), lambda i: (i, 0))
    vec = pl.BlockSpec((1, LANE), lambda i: (0, 0))
    out = jax.ShapeDtypeStruct((length, LANE), F32)
    return pl.pallas_call(
        _ssd_dt_kernel,
        grid=(length // tm,),
        in_specs=[spec, vec, vec],
        out_specs=[spec, spec],
        out_shape=[out, out],
        compiler_params=_params("arbitrary"),
        name="ssd_dt",
    )(raw, jnp.pad(dt_bias.reshape(1, -1), ((0, 0), (0, pad))), jnp.pad(a_log.reshape(1, -1), ((0, 0), (0, pad))))


def _pieces(a):
    out = []
    for _ in range(3):
        piece = a.astype(BF16)
        out.append(piece)
        a = a - piece.astype(F32)
    return out


def _ssd_scan_kernel(*refs, direction, reverse, finalize, chunks):
    if finalize:
        (x_ref, b_ref, c_ref, dac_ref, dtr_ref, dar_ref, h0_ref, yp_ref, z_ref, dsk_ref, nw_ref,
         y_ref, hn_ref, s_ref, yacc_ref) = refs
    else:
        x_ref, b_ref, c_ref, dac_ref, dtr_ref, dar_ref, h0_ref, y_ref, hn_ref, s_ref = refs
        yacc_ref = y_ref
    step = pl.program_id(0)
    nsteps = pl.num_programs(0)
    t_len = SSD_CHUNK

    @pl.when(step == 0)
    def _():
        s_ref[...] = h0_ref[...]

    row = lax.broadcasted_iota(jnp.int32, (t_len, t_len), 0)
    col = lax.broadcasted_iota(jnp.int32, (t_len, t_len), 1)
    mask = (col >= row) if reverse else (col <= row)
    mask_t = (row >= col) if reverse else (row <= col)
    ones = jnp.ones((t_len, t_len), BF16)
    tri = jnp.where(mask, 1.0, 0.0).astype(BF16)
    tri_t = jnp.where(mask_t, 1.0, 0.0).astype(BF16)
    hsl = slice(direction * SSD_HEADS, (direction + 1) * SSD_HEADS)
    lane = lax.broadcasted_iota(jnp.int32, (t_len, LANE), 1)
    first = lane < SSD_HEAD_DIM
    edge = slice(0, 1) if reverse else slice(t_len - 1, t_len)

    for u in range(chunks):
        q = chunks - 1 - u if reverse else u
        rs = slice(q * t_len, (q + 1) * t_len)
        _ssd_chunk(x_ref, b_ref, c_ref, dac_ref, dtr_ref, dar_ref, s_ref, yacc_ref, rs, hsl, direction,
                   mask, tri, tri_t, ones, first, edge)

    @pl.when(step == nsteps - 1)
    def _():
        hn_ref[...] = s_ref[...]

    if finalize:
        y = yacc_ref[...] + yp_ref[...] + x_ref[...].astype(F32) * dsk_ref[...]
        zz = z_ref[...].astype(F32)
        gated = y * (zz * jax.nn.sigmoid(zz))
        out = gated * lax.rsqrt(jnp.mean(gated * gated, axis=-1, keepdims=True) + RMS_EPS) * nw_ref[...]
        y_ref[...] = out.astype(y_ref.dtype)


def _ssd_chunk(x_ref, b_ref, c_ref, dac_ref, dtr_ref, dar_ref, s_ref, yacc_ref, rs, hsl, direction,
               mask, tri, tri_t, ones, first, edge):
    t_len = SSD_CHUNK
    g_col = sum(jnp.dot(tri, piece, preferred_element_type=F32) for piece in _pieces(dac_ref[rs, :]))
    da_row_p = _pieces(dar_ref[hsl, rs])
    g_row = sum(jnp.dot(piece, tri_t, preferred_element_type=F32) for piece in da_row_p)
    tot_row = sum(jnp.dot(piece, ones, preferred_element_type=F32) for piece in da_row_p)
    dt_row = dtr_ref[hsl, rs]
    wdt_row = jnp.exp(tot_row - g_row) * dt_row

    for grp in range(SSD_GROUPS):
        nsl = slice(grp * SSD_STATE, (grp + 1) * SSD_STATE)
        b_g, c_g = b_ref[rs, nsl], c_ref[rs, nsl]
        cb = lax.dot_general(c_g, b_g, _NT, preferred_element_type=F32)
        bt_g = b_g.astype(F32).T
        for k in range(grp * SSD_HPG // 2, (grp + 1) * SSD_HPG // 2):
            ms, es, bws = [], [], []
            for h in (2 * k, 2 * k + 1):
                j = direction * SSD_HEADS + h
                g_bc = jnp.broadcast_to(g_col[:, j:j + 1], (t_len, t_len))
                decay = jnp.where(mask, jnp.exp(g_bc - g_row[h:h + 1, :]), 0.0)
                ms.append((decay * cb * dt_row[h:h + 1, :]).astype(BF16))
                es.append(jnp.exp(g_bc))
                bws.append((bt_g * wdt_row[h:h + 1, :]).astype(BF16))
            csl = slice(k * LANE, (k + 1) * LANE)
            xp = x_ref[rs, csl]
            xa = jnp.where(first, xp, jnp.zeros_like(xp))
            xb = jnp.where(first, jnp.zeros_like(xp), xp)
            e_pair = jnp.where(first, es[0], es[1])
            s_old = s_ref[k]
            y = (jnp.dot(ms[0], xa, preferred_element_type=F32) + jnp.dot(ms[1], xb, preferred_element_type=F32)
                 + jnp.dot(c_g, s_old.astype(BF16), preferred_element_type=F32) * e_pair)
            s_ref[k] = (s_old * e_pair[edge] + jnp.dot(bws[0], xa, preferred_element_type=F32)
                        + jnp.dot(bws[1], xb, preferred_element_type=F32))
            yacc_ref[rs, csl] = y


def _ssd_scan(xbc, da_col, dt_row, da_row, h0, direction, final=None, chunks=2):
    length = xbc.shape[0]
    t_len = SSD_CHUNK
    n_chunks = length // t_len
    chunks = min(chunks, n_chunks)
    n_steps = n_chunks // chunks
    rows = chunks * t_len
    reverse = direction == 1
    pos = (lambda c: n_steps - 1 - c) if reverse else (lambda c: c)
    di, gn = SSD_D_INNER, SSD_GN
    state = jax.ShapeDtypeStruct((SSD_PAIRS, SSD_STATE, LANE), F32)
    state_spec = pl.BlockSpec((SSD_PAIRS, SSD_STATE, LANE), lambda c: (0, 0, 0))
    in_specs = [pl.BlockSpec((rows, di), lambda c: (pos(c), 0)),
                pl.BlockSpec((rows, gn), lambda c: (pos(c), di // gn)),
                pl.BlockSpec((rows, gn), lambda c: (pos(c), di // gn + 1)),
                pl.BlockSpec((rows, LANE), lambda c: (pos(c), 0)),
                pl.BlockSpec((2 * SSD_HEADS, rows), lambda c: (0, pos(c))),
                pl.BlockSpec((2 * SSD_HEADS, rows), lambda c: (0, pos(c))),
                state_spec]
    args = [xbc, xbc, xbc, da_col, dt_row, da_row, h0]
    scratch = [pltpu.VMEM((SSD_PAIRS, SSD_STATE, LANE), F32)]
    y_spec = pl.BlockSpec((rows, di), lambda c: (pos(c), 0))
    row_spec = pl.BlockSpec((1, di), lambda c: (0, 0))
    if final is not None:
        y_other, p, d_skip_row, norm_w_row = final
        in_specs += [y_spec, y_spec, row_spec, row_spec]
        args += [y_other, p, d_skip_row, norm_w_row]
        scratch.append(pltpu.VMEM((rows, di), F32))
    y_dtype = BF16 if final is not None else F32
    kern = functools.partial(_ssd_scan_kernel, direction=direction, reverse=reverse, finalize=final is not None,
                             chunks=chunks)
    return pl.pallas_call(
        kern,
        grid=(n_steps,),
        in_specs=in_specs,
        out_specs=[y_spec, state_spec],
        out_shape=[jax.ShapeDtypeStruct((length, di), y_dtype), state],
        scratch_shapes=scratch,
        compiler_params=_params("arbitrary"),
        name="ssd_scan",
    )(*args)


def _ssd_inputs(p, dt_raw, conv_w, conv_b, dt_bias, a_log):
    xbc = _ssd_conv(p, P_XBC // SSD_CONV_BLK, conv_w, conv_b)
    dt, da = _ssd_dt(dt_raw, dt_bias, a_log)
    nh = 2 * SSD_HEADS
    return xbc, da, dt[:, :nh].T, da[:, :nh].T


def _ssd_pallas(p_l, dt_raw_l, p_c, dt_raw_c, conv_w, conv_b, dt_bias, a_log, d_skip, norm_w, ctx_out):
    xbc_l, dac_l, dtr_l, dar_l = _ssd_inputs(p_l, dt_raw_l, conv_w, conv_b, dt_bias, a_log)
    xbc_c, dac_c, dtr_c, dar_c = _ssd_inputs(p_c, dt_raw_c, conv_w, conv_b, dt_bias, a_log)
    zero = jnp.zeros((SSD_PAIRS, SSD_STATE, LANE), F32)
    fin = (jnp.repeat(d_skip, SSD_HEAD_DIM).reshape(1, -1), norm_w.reshape(1, -1))
    yc0, h_fwd = _ssd_scan(xbc_c, dac_c, dtr_c, dar_c, zero, 0)
    if ctx_out:
        out_c, h_bwd = _ssd_scan(xbc_c, dac_c, dtr_c, dar_c, zero, 1, final=(yc0, p_c) + fin)
    else:
        out_c = None
        _, h_bwd = _ssd_scan(xbc_c, dac_c, dtr_c, dar_c, zero, 1)
    yl0, _ = _ssd_scan(xbc_l, dac_l, dtr_l, dar_l, h_fwd, 0)
    out_l, _ = _ssd_scan(xbc_l, dac_l, dtr_l, dar_l, h_bwd, 1, final=(yl0, p_l) + fin)
    return out_l, out_c


P_Z = 0
P_Q = P_Z + SSD_D_INNER
P_POOL = P_Q + NA_WIDTH
P_XBC = P_POOL + POOL_WIDTH
P_K = P_XBC + SSD_XBC
P_V = P_K + NA_WIDTH
P_COLS = P_V + NA_WIDTH


def _layer(x, ctx, mod_l, mod_c, norm1_w, w_in, ssd_conv_w, ssd_conv_b, ssd_dt_bias, ssd_a_log, ssd_d,
           ssd_norm_w, na_q_norm_w, na_k_norm_w, na_rpb, pool_w, pool_scale, w_out, norm2_w, mlp_w_up,
           mlp_conv_w, mlp_conv_b, mlp_w_down, rope_l, rope_c, ctx_out):
    sh1_l, sc1_l, g1_l, sh2_l, sc2_l, g2_l = jnp.split(mod_l, N_MOD)
    sh1_c, sc1_c, g1_c, sh2_c, sc2_c, g2_c = jnp.split(mod_c, N_MOD)

    w_main = jnp.concatenate([w_in[:, :OFF_DT], w_in[:, OFF_K:]], axis=1).astype(BF16)
    w_dt = jnp.pad(w_in[:, OFF_DT:OFF_K], ((0, 0), (0, LANE - 2 * SSD_HEADS))).astype(BF16)
    w_out_b = w_out.astype(BF16)
    w_up_b = mlp_w_up.astype(BF16)
    w_down_b = mlp_w_down
    conv_tab = jnp.concatenate([mlp_conv_w, mlp_conv_b[None], jnp.zeros((4, 2 * D_FF), F32)], axis=0)

    h_l = _norm_mod(x, norm1_w, sc1_l, sh1_l)
    h_c = _norm_mod(ctx, norm1_w, sc1_c, sh1_c)
    p_l = _matmul(h_l, w_main, BF16)
    dt_l = _matmul(h_l, w_dt, F32)
    p_c = _matmul(h_c, w_main, BF16)
    dt_c = _matmul(h_c, w_dt, F32)

    y_ssd_l, y_ssd_c = _ssd_pallas(p_l, dt_l, p_c, dt_c, ssd_conv_w, ssd_conv_b, ssd_dt_bias, ssd_a_log, ssd_d,
                                   ssd_norm_w, ctx_out)
    q_blk, k_blk, v_blk = P_Q // NA_WIDTH, P_K // NA_WIDTH, P_V // NA_HEAD_DIM
    q_r, k_r = _qk_prep(p_l, q_blk, k_blk, na_q_norm_w, na_k_norm_w, rope_l)
    qc_r, kc_r = _qk_prep(p_c, q_blk, k_blk, na_q_norm_w, na_k_norm_w, rope_c)
    y_na_l = _na_latent(q_r, k_r, p_l, v_blk, kc_r, p_c, v_blk, _na_bias_table(na_rpb))
    y_pool_l = _pool(p_l, P_POOL // POOL_WIDTH, pool_w, pool_scale)

    x_mid = _outproj([y_ssd_l, y_na_l, y_pool_l], w_out_b, x, g1_l)
    h2_l = _norm_mod(x_mid, norm2_w, sc2_l, sh2_l)
    x_new = _conv_ffn(h2_l, w_up_b, conv_tab, w_down_b, x_mid, g2_l)
    ctx_new = ctx
    if ctx_out:
        y_na_c = _na_context(qc_r, kc_r, p_c, v_blk)
        y_pool_c = _pool(p_c, P_POOL // POOL_WIDTH, pool_w, pool_scale)
        c_mid = _outproj([y_ssd_c, y_na_c, y_pool_c], w_out_b, ctx, g1_c)
        h2_c = _norm_mod(c_mid, norm2_w, sc2_c, sh2_c)
        ctx_new = _conv_ffn(h2_c, w_up_b, conv_tab, w_down_b, c_mid, g2_c)
    return x_new, ctx_new


def kernel(x, c, ctx, c_ctx, ada_w, ada_b, norm1_w, w_in, ssd_conv_w, ssd_conv_b, ssd_dt_bias, ssd_a_log,
           ssd_d, ssd_norm_w, na_q_norm_w, na_k_norm_w, na_rpb, pool_w, pool_scale, w_out, norm2_w,
           mlp_w_up, mlp_conv_w, mlp_conv_b, mlp_w_down):
    depth = ada_w.shape[0]
    d = x.shape[-1]
    c_rows = jnp.concatenate([c.reshape(1, d), c_ctx.reshape(1, d), jnp.zeros((MOD_ROWS - 2, d), F32)], axis=0)
    mods = _modulation(c_rows, ada_w, ada_b)
    xs, cs = x[0], ctx[0]
    rope_l = _rope_tables(xs.shape[0])
    ctx_tab = (cs.shape[0], NA_HEAD_DIM)
    rope_c = (jnp.ones(ctx_tab, F32), jnp.zeros(ctx_tab, F32), jnp.zeros(ctx_tab, F32))
    for i in range(depth):
        xs, cs = _layer(xs, cs, mods[i, 0], mods[i, 1], norm1_w[i], w_in[i], ssd_conv_w[i], ssd_conv_b[i],
                        ssd_dt_bias[i], ssd_a_log[i], ssd_d[i], ssd_norm_w[i], na_q_norm_w[i], na_k_norm_w[i],
                        na_rpb[i], pool_w[i], pool_scale[i], w_out[i], norm2_w[i], mlp_w_up[i], mlp_conv_w[i],
                        mlp_conv_b[i], mlp_w_down[i], rope_l, rope_c, ctx_out=(i < depth - 1))
    return xs[None]
```

```python
import functools
import math

import jax
import jax.numpy as jnp
from jax import lax
from jax.experimental import pallas as pl
from jax.experimental.pallas import tpu as pltpu

F32 = jnp.float32
BF16 = jnp.bfloat16

D_MODEL = 4096
DEPTH = 2
GRID_W = 64
D_MIX = D_MODEL
SSD_D_INNER = D_MIX // 2
SSD_HEAD_DIM = 64
SSD_HEADS = SSD_D_INNER // SSD_HEAD_DIM
SSD_GROUPS = 4
SSD_HPG = SSD_HEADS // SSD_GROUPS
SSD_STATE = 128
SSD_GN = SSD_GROUPS * SSD_STATE
SSD_XBC = SSD_D_INNER + 2 * SSD_GN
SSD_CONV = 5
SSD_CHUNK = 128
NA_WIDTH = D_MIX // 4
NA_HEAD_DIM = 128
NA_HEADS = NA_WIDTH // NA_HEAD_DIM
NA_WIN_R = 8
NA_WIN_C = 16
ROPE_BASE = 10000.0
POOL_WIDTH = D_MIX - SSD_D_INNER - NA_WIDTH
POOL_WINDOWS = (2, 4, 8, 16)
POOL_GROUP = POOL_WIDTH // len(POOL_WINDOWS)
D_FF = 11008
N_MOD = 6
RMS_EPS = 1e-6
OFF_Z = 0
OFF_Q = OFF_Z + SSD_D_INNER
OFF_POOL = OFF_Q + NA_WIDTH
OFF_XBC = OFF_POOL + POOL_WIDTH
OFF_DT = OFF_XBC + SSD_XBC
OFF_K = OFF_DT + 2 * SSD_HEADS
OFF_V = OFF_K + NA_WIDTH
IN_COLS = OFF_V + NA_WIDTH

VMEM_LIMIT_BYTES = 58 * 1024 * 1024
LANE = 128
BF16_SUBLANE = 16
MOD_ROWS = 8


def _params(*sem):
    return pltpu.CompilerParams(dimension_semantics=sem, vmem_limit_bytes=VMEM_LIMIT_BYTES)


def _mod_kernel(c_ref, w_ref, b_ref, o_ref):
    c = c_ref[...]
    s = (c * jax.nn.sigmoid(c)).astype(BF16)
    o_ref[0] = jnp.dot(s, w_ref[0].astype(BF16), preferred_element_type=F32) + b_ref[0]


def _modulation(c_rows, ada_w, ada_b, tn=1024):
    depth, d, n = ada_w.shape
    return pl.pallas_call(
        _mod_kernel,
        grid=(depth, n // tn),
        in_specs=[pl.BlockSpec((MOD_ROWS, d), lambda l, j: (0, 0)),
                  pl.BlockSpec((1, d, tn), lambda l, j: (l, 0, j)),
                  pl.BlockSpec((1, 1, tn), lambda l, j: (l, 0, j))],
        out_specs=pl.BlockSpec((1, MOD_ROWS, tn), lambda l, j: (l, 0, j)),
        out_shape=jax.ShapeDtypeStruct((depth, MOD_ROWS, n), F32),
        compiler_params=_params("arbitrary", "arbitrary"),
        name="modulation",
    )(c_rows, ada_w, ada_b.reshape(depth, 1, n))


def _norm_kernel(x_ref, w_ref, sc_ref, sh_ref, o_ref):
    x = x_ref[...]
    y = x * lax.rsqrt(jnp.mean(x * x, axis=-1, keepdims=True) + RMS_EPS) * w_ref[...]
    o_ref[...] = (y * (1.0 + sc_ref[...]) + sh_ref[...]).astype(o_ref.dtype)


def _norm_mod(x, w, scale, shift, tm=512):
    m, d = x.shape
    tm = min(tm, m)
    row = pl.BlockSpec((1, d), lambda i: (0, 0))
    return pl.pallas_call(
        _norm_kernel,
        grid=(m // tm,),
        in_specs=[pl.BlockSpec((tm, d), lambda i: (i, 0)), row, row, row],
        out_specs=pl.BlockSpec((tm, d), lambda i: (i, 0)),
        out_shape=jax.ShapeDtypeStruct((m, d), BF16),
        compiler_params=_params("arbitrary"),
        name="norm_mod",
    )(x, w.reshape(1, d), scale.reshape(1, d), shift.reshape(1, d))


def _mm_kernel(a_ref, b_ref, o_ref):
    o_ref[...] = jnp.dot(a_ref[...], b_ref[...], preferred_element_type=F32).astype(o_ref.dtype)


def _matmul(a, b, out_dtype, tm=1024, tn=1024):
    m, k = a.shape
    n = b.shape[1]
    tm, tn = min(tm, m), min(tn, n)
    return pl.pallas_call(
        _mm_kernel,
        grid=(m // tm, n // tn),
        in_specs=[pl.BlockSpec((tm, k), lambda i, j: (i, 0)),
                  pl.BlockSpec((k, tn), lambda i, j: (0, j))],
        out_specs=pl.BlockSpec((tm, tn), lambda i, j: (i, j)),
        out_shape=jax.ShapeDtypeStruct((m, n), out_dtype),
        compiler_params=_params("arbitrary", "arbitrary"),
        name="matmul",
    )(a, b)


def _outproj_kernel(*refs, widths):
    a_refs, (w_ref, x_ref, g_ref, o_ref) = refs[:len(widths)], refs[len(widths):]
    acc, off = None, 0
    for a_ref, k in zip(a_refs, widths):
        part = jnp.dot(a_ref[...], w_ref[0, off:off + k, :].astype(BF16), preferred_element_type=F32)
        acc = part if acc is None else acc + part
        off += k
    o_ref[...] = x_ref[...] + g_ref[...] * acc


def _outproj(parts, w_all, layer, x, gate, tm=1024, tn=512):
    m = x.shape[0]
    _, k, n = w_all.shape
    tm = min(tm, m)
    widths = tuple(a.shape[1] for a in parts)
    assert sum(widths) == k
    return pl.pallas_call(
        functools.partial(_outproj_kernel, widths=widths),
        grid=(m // tm, n // tn),
        in_specs=[pl.BlockSpec((tm, kw), lambda i, j: (i, 0)) for kw in widths] + [
            pl.BlockSpec((1, k, tn), lambda i, j: (layer, 0, j)),
            pl.BlockSpec((tm, tn), lambda i, j: (i, j)),
            pl.BlockSpec((1, tn), lambda i, j: (0, j))],
        out_specs=pl.BlockSpec((tm, tn), lambda i, j: (i, j)),
        out_shape=jax.ShapeDtypeStruct((m, n), F32),
        compiler_params=_params("arbitrary", "arbitrary"),
        name="outproj",
    )(*parts, w_all, x, gate.reshape(1, n))


FFN_HALO = BF16_SUBLANE


def _ffn_kernel(h_ref, hp_ref, hn_ref, wg_ref, wv_ref, tab_ref, wd_ref, x_ref, g_ref,
                o_ref, ext_ref, ua_ref, ub_ref, *, tm, n_mtiles, nf, tn_d):
    i = pl.program_id(0)
    f = pl.program_id(1)
    d = o_ref.shape[1]
    u_refs = (ua_ref, ub_ref)

    @pl.when(f == 0)
    def _():
        ext_ref[0:tm, :] = h_ref[...]
        nxt = hn_ref[...]
        prv = hp_ref[...]
        ext_ref[tm:tm + FFN_HALO, :] = jnp.where(i == n_mtiles - 1, jnp.zeros_like(nxt), nxt)
        ext_ref[tm + FFN_HALO:tm + 2 * FFN_HALO, :] = jnp.where(i == 0, jnp.zeros_like(prv), prv)
        o_ref[...] = jnp.zeros_like(o_ref)
        ub_ref[...] = jnp.zeros_like(ub_ref)

    def conv(u_ref, br, c):
        um = u_ref[br, 0:tm]
        row = lax.broadcasted_iota(jnp.int32, um.shape, 0)
        before = tm + 2 * FFN_HALO - 1
        up = jnp.where(row == 0, u_ref[br, before:before + 1], pltpu.roll(um, 1, 0))
        un = jnp.where(row == tm - 1, u_ref[br, tm:tm + 1], pltpu.roll(um, tm - 1, 0))
        return up * c[0:1] + um * c[1:2] + un * c[2:3] + c[3:4]

    def step(produce, consume):
        h = ext_ref[...]
        p_ref, c_ref = u_refs[produce], u_refs[consume]
        p_ref[0] = jnp.dot(h, wg_ref[...], preferred_element_type=F32)
        p_ref[1] = jnp.dot(h, wv_ref[...], preferred_element_type=F32)
        tile = jnp.maximum(f - 1, 0)
        gate = conv(c_ref, 0, tab_ref[tile])
        val = conv(c_ref, 1, tab_ref[tile + nf])
        act = gate * jax.nn.sigmoid(gate) * val
        act = jnp.where(f > 0, act, jnp.zeros_like(act)).astype(BF16)
        for n in range(d // tn_d):
            sl = slice(n * tn_d, (n + 1) * tn_d)
            o_ref[:, sl] += jnp.dot(act, wd_ref[0, :, sl].astype(BF16), preferred_element_type=F32)

    @pl.when(f % 2 == 0)
    def _():
        step(0, 1)

    @pl.when(f % 2 == 1)
    def _():
        step(1, 0)

    @pl.when(f == nf)
    def _():
        o_ref[...] = x_ref[...] + g_ref[...] * o_ref[...]


def _conv_ffn(h, w_up, conv_tab, w_down_all, layer, x, gate, tm=512, tf=256, tn_d=1024):
    m, d = h.shape
    ff = w_down_all.shape[1]
    tm = min(tm, m)
    tn_d = min(tn_d, d)
    n_mtiles, nf = m // tm, ff // tf
    hb = tm // FFN_HALO
    last_hb = m // FFN_HALO - 1
    kern = functools.partial(_ffn_kernel, tm=tm, n_mtiles=n_mtiles, nf=nf, tn_d=tn_d)
    tab = conv_tab.reshape(8, 2 * nf, tf).transpose(1, 0, 2)

    def prod(f):
        return jnp.minimum(f, nf - 1)

    def cons(f):
        return jnp.maximum(f - 1, 0)

    return pl.pallas_call(
        kern,
        grid=(n_mtiles, nf + 1),
        in_specs=[
            pl.BlockSpec((tm, d), lambda i, f: (i, 0), pipeline_mode=pl.Buffered(1)),
            pl.BlockSpec((FFN_HALO, d), lambda i, f: (jnp.maximum(i * hb - 1, 0), 0)),
            pl.BlockSpec((FFN_HALO, d), lambda i, f: (jnp.minimum((i + 1) * hb, last_hb), 0)),
            pl.BlockSpec((d, tf), lambda i, f: (0, prod(f))),
            pl.BlockSpec((d, tf), lambda i, f: (0, prod(f) + nf)),
            pl.BlockSpec((2 * nf, 8, tf), lambda i, f: (0, 0, 0)),
            pl.BlockSpec((1, tf, d), lambda i, f: (layer, cons(f), 0)),
            pl.BlockSpec((tm, d), lambda i, f: (i, 0), pipeline_mode=pl.Buffered(1)),
            pl.BlockSpec((1, d), lambda i, f: (0, 0)),
        ],
        out_specs=pl.BlockSpec((tm, d), lambda i, f: (i, 0)),
        out_shape=jax.ShapeDtypeStruct((m, d), F32),
        scratch_shapes=[pltpu.VMEM((tm + 2 * FFN_HALO, d), BF16),
                        pltpu.VMEM((2, tm + 2 * FFN_HALO, tf), F32),
                        pltpu.VMEM((2, tm + 2 * FFN_HALO, tf), F32)],
        compiler_params=_params("arbitrary", "arbitrary"),
        name="conv_ffn",
    )(h, h, h, w_up, w_up, tab, w_down_all, x, gate.reshape(1, d))


def _rope_tables(length):
    t = jnp.arange(length)
    half = NA_HEAD_DIM // 2
    inv = ROPE_BASE ** (-jnp.arange(0, half, 2, dtype=F32) / half)
    ar = (t // GRID_W).astype(F32)[:, None] * inv
    ac = (t % GRID_W).astype(F32)[:, None] * inv
    ang = jnp.concatenate([ar, ar, ac, ac], axis=-1)
    cos, sin = jnp.cos(ang), jnp.sin(ang)
    first = (jnp.arange(NA_HEAD_DIM) % half) < half // 2
    return cos, jnp.where(first, -sin, 0.0), jnp.where(first, 0.0, sin)


def _qk_prep_kernel(q_ref, k_ref, qw_ref, kw_ref, cos_ref, sa_ref, sb_ref, qo_ref, ko_ref):
    cos, sa, sb = cos_ref[...], sa_ref[...], sb_ref[...]
    quarter = NA_HEAD_DIM // 4
    for h in range(NA_HEADS):
        sl = slice(h * NA_HEAD_DIM, (h + 1) * NA_HEAD_DIM)
        for src, w_ref, dst in ((q_ref, qw_ref, qo_ref), (k_ref, kw_ref, ko_ref)):
            x = src[:, sl].astype(F32)
            y = x * lax.rsqrt(jnp.mean(x * x, axis=-1, keepdims=True) + RMS_EPS) * w_ref[...]
            y = (y * cos + pltpu.roll(y, NA_HEAD_DIM - quarter, 1) * sa + pltpu.roll(y, quarter, 1) * sb)
            dst[:, sl] = y.astype(dst.dtype)


def _qk_prep(p, q_blk, k_blk, qw, kw, tables, tm=512):
    m = p.shape[0]
    tm = min(tm, m)
    tab = pl.BlockSpec((tm, NA_HEAD_DIM), lambda i: (i, 0))
    vec = pl.BlockSpec((1, NA_HEAD_DIM), lambda i: (0, 0))
    out = jax.ShapeDtypeStruct((m, NA_WIDTH), BF16)
    return pl.pallas_call(
        _qk_prep_kernel,
        grid=(m // tm,),
        in_specs=[pl.BlockSpec((tm, NA_WIDTH), lambda i: (i, q_blk)),
                  pl.BlockSpec((tm, NA_WIDTH), lambda i: (i, k_blk)), vec, vec, tab, tab, tab],
        out_specs=[pl.BlockSpec((tm, NA_WIDTH), lambda i: (i, 0))] * 2,
        out_shape=[out, out],
        compiler_params=_params("arbitrary"),
        name="qk_prep",
    )(p, p, qw.reshape(1, -1), kw.reshape(1, -1), *tables)


NA_DR = 2 * NA_WIN_R
NA_BIAS_CHUNK = 1024


def _na_bias_kernel(r0_ref, r1_ref, o_ref):
    def pieces(r):
        out = []
        for _ in range(3):
            piece = r.astype(BF16)
            out.append(piece)
            r = r - piece.astype(F32)
        return out

    p0, p1 = pieces(r0_ref[0]), pieces(r1_ref[0])
    for c in range(GRID_W * LANE // NA_BIAS_CHUNK):
        shape = (LANE, NA_BIAS_CHUNK)
        pos = c * NA_BIAS_CHUNK + lax.broadcasted_iota(jnp.int32, shape, 1)
        j = lax.broadcasted_iota(jnp.int32, shape, 0)
        qc = pos // LANE
        lane = pos % LANE
        kc = lane % GRID_W
        idx = jnp.clip(kc - qc + (NA_WIN_C - 1), 0, 2 * NA_WIN_C - 2)
        hit = idx == j
        second = lane >= GRID_W
        oh0 = jnp.where(hit, jnp.where(second, 0.0, 1.0), 0.0).astype(BF16)
        oh1 = jnp.where(hit, jnp.where(second, 1.0, 0.0), 0.0).astype(BF16)
        acc = jnp.zeros((NA_DR, NA_BIAS_CHUNK), F32)
        for a, b in zip(p0, p1):
            acc = acc + jnp.dot(a, oh0, preferred_element_type=F32) + jnp.dot(b, oh1, preferred_element_type=F32)
        c0 = jnp.clip(qc[0:1] - NA_WIN_C // 2, 0, GRID_W - NA_WIN_C)
        ok = (kc[0:1] >= c0) & (kc[0:1] < c0 + NA_WIN_C)
        o_ref[0, :, c * NA_BIAS_CHUNK:(c + 1) * NA_BIAS_CHUNK] = jnp.where(ok, acc, -jnp.inf)


def _na_bias_table(rpb):
    nh, ndr, ndc = rpb.shape
    r0 = jnp.pad(rpb, ((0, 0), (0, NA_DR - ndr), (0, LANE - ndc)))
    r1 = jnp.pad(rpb[:, 1:], ((0, 0), (0, NA_DR - ndr + 1), (0, LANE - ndc)))
    spec = pl.BlockSpec((1, NA_DR, LANE), lambda h: (h, 0, 0))
    out = pl.pallas_call(
        _na_bias_kernel,
        grid=(nh,),
        in_specs=[spec, spec],
        out_specs=pl.BlockSpec((1, NA_DR, GRID_W * LANE), lambda h: (h, 0, 0)),
        out_shape=jax.ShapeDtypeStruct((nh, NA_DR, GRID_W * LANE), F32),
        compiler_params=_params("arbitrary"),
        name="na_bias",
    )(r0, r1)
    return out.reshape(nh, NA_DR, GRID_W, LANE)


def _softmax_parts(scores):
    m = functools.reduce(jnp.maximum, [jnp.max(s, axis=1, keepdims=True) for s in scores])
    ps = [jnp.exp(s - m) for s in scores]
    den = functools.reduce(jnp.add, [jnp.sum(p, axis=1, keepdims=True) for p in ps])
    return [p.astype(BF16) for p in ps], den


def _pv(parts, values):
    ps, den = parts
    o = functools.reduce(jnp.add, [jnp.dot(p, v, preferred_element_type=F32) for p, v in zip(ps, values)])
    return o / den


def _softmax_pv(scores, values):
    return _pv(_softmax_parts(scores), values)


_NT = (((1,), (1,)), ((), ()))


def _na_kernel(q_ref, k_ref, v_ref, kc_ref, vc_ref, tz_ref, o_ref, *, rows_per_step, n_rows):
    t = pl.program_id(1)
    scale = NA_HEAD_DIM ** -0.5
    band = NA_WIN_R * GRID_W
    kc, vc = kc_ref[...], vc_ref[...]
    starts, scores = [], []
    for i in range(rows_per_step):
        qr = t * rows_per_step + i
        bs = jnp.clip(qr - NA_WIN_R // 2, 0, n_rows - NA_WIN_R)
        off = bs - qr + (NA_WIN_R - 1)
        start = pl.multiple_of(bs * GRID_W, GRID_W)
        q = q_ref[i * GRID_W:(i + 1) * GRID_W, :]
        kb = k_ref[pl.ds(start, band), :]
        bias = jnp.concatenate([tz_ref[0, off + 2 * w] for w in range(NA_WIN_R // 2)], axis=1)
        s_loc = lax.dot_general(q, kb, _NT, preferred_element_type=F32) * scale + bias
        s_ctx = lax.dot_general(q, kc, _NT, preferred_element_type=F32) * scale
        starts.append(start)
        scores.append((s_loc, s_ctx))
    probs = [_softmax_parts(list(s)) for s in scores]
    for i in range(rows_per_step):
        vb = v_ref[pl.ds(starts[i], band), :]
        o = _pv(probs[i], [vb, vc])
        o_ref[i * GRID_W:(i + 1) * GRID_W, :] = o.astype(o_ref.dtype)


def _na_latent(q_r, k_r, p_l, v_blk, kc_r, p_c, vc_blk, tz, rows_per_step=8):
    length = q_r.shape[0]
    lc = kc_r.shape[0]
    n_rows = length // GRID_W
    assert n_rows >= NA_WIN_R and n_rows % rows_per_step == 0
    hd = NA_HEAD_DIM
    kern = functools.partial(_na_kernel, rows_per_step=rows_per_step, n_rows=n_rows)
    return pl.pallas_call(
        kern,
        grid=(NA_HEADS, n_rows // rows_per_step),
        in_specs=[pl.BlockSpec((rows_per_step * GRID_W, hd), lambda h, t: (t, h)),
                  pl.BlockSpec((length, hd), lambda h, t: (0, h)),
                  pl.BlockSpec((length, hd), lambda h, t: (0, v_blk + h)),
                  pl.BlockSpec((lc, hd), lambda h, t: (0, h)),
                  pl.BlockSpec((lc, hd), lambda h, t: (0, vc_blk + h)),
                  pl.BlockSpec((1, NA_DR, GRID_W, LANE), lambda h, t: (h, 0, 0, 0))],
        out_specs=pl.BlockSpec((rows_per_step * GRID_W, hd), lambda h, t: (t, h)),
        out_shape=jax.ShapeDtypeStruct((length, NA_WIDTH), BF16),
        compiler_params=_params("arbitrary", "arbitrary"),
        name="na_latent",
    )(q_r, k_r, p_l, kc_r, p_c, tz)


def _ctx_attn_kernel(q_ref, k_ref, v_ref, o_ref):
    s = lax.dot_general(q_ref[...], k_ref[...], _NT, preferred_element_type=F32) * (NA_HEAD_DIM ** -0.5)
    o_ref[...] = _softmax_pv([s], [v_ref[...]]).astype(o_ref.dtype)


def _na_context(qc_r, kc_r, p_c, vc_blk):
    lc = qc_r.shape[0]
    hd = NA_HEAD_DIM
    spec = pl.BlockSpec((lc, hd), lambda h: (0, h))
    return pl.pallas_call(
        _ctx_attn_kernel,
        grid=(NA_HEADS,),
        in_specs=[spec, spec, pl.BlockSpec((lc, hd), lambda h: (0, vc_blk + h))],
        out_specs=spec,
        out_shape=jax.ShapeDtypeStruct((lc, NA_WIDTH), BF16),
        compiler_params=_params("arbitrary"),
        name="na_context",
    )(qc_r, kc_r, p_c)


POOL_HALO = BF16_SUBLANE


def _pool_kernel(u_ref, up_ref, un_ref, w_ref, sc_ref, o_ref, ext_ref, *, tm, n_tiles, length):
    i = pl.program_id(0)
    prv, nxt = up_ref[...], un_ref[...]
    ext_ref[0:POOL_HALO, :] = jnp.where(i == 0, jnp.zeros_like(prv), prv).astype(F32)
    ext_ref[POOL_HALO:POOL_HALO + tm, :] = u_ref[...].astype(F32)
    ext_ref[POOL_HALO + tm:POOL_HALO + tm + POOL_HALO, :] = jnp.where(i == n_tiles - 1, jnp.zeros_like(nxt),
                                                                     nxt).astype(F32)
    t = i * tm + lax.broadcasted_iota(jnp.int32, (tm, 1), 0)
    for g, w in enumerate(POOL_WINDOWS):
        cs = slice(g * POOL_GROUP, (g + 1) * POOL_GROUP)
        acc = ext_ref[pl.ds(POOL_HALO - w // 2, tm), cs]
        for k in range(1, w):
            acc = acc + ext_ref[pl.ds(POOL_HALO - w // 2 + k, tm), cs]
        cnt = (jnp.minimum(t + w // 2, length) - jnp.maximum(t - w // 2, 0)).astype(F32)
        pooled = acc / cnt - ext_ref[pl.ds(POOL_HALO, tm), cs]
        y = jnp.dot(pooled.astype(BF16), w_ref[g], preferred_element_type=F32) * sc_ref[:, cs]
        o_ref[:, cs] = y.astype(o_ref.dtype)


def _pool(p, u_blk, pool_w, pool_scale, tm=512):
    length = p.shape[0]
    tm = min(tm, length)
    n_tiles = length // tm
    hb = tm // POOL_HALO
    last_hb = length // POOL_HALO - 1
    kern = functools.partial(_pool_kernel, tm=tm, n_tiles=n_tiles, length=length)
    return pl.pallas_call(
        kern,
        grid=(n_tiles,),
        in_specs=[pl.BlockSpec((tm, POOL_WIDTH), lambda i: (i, u_blk)),
                  pl.BlockSpec((POOL_HALO, POOL_WIDTH), lambda i: (jnp.maximum(i * hb - 1, 0), u_blk)),
                  pl.BlockSpec((POOL_HALO, POOL_WIDTH), lambda i: (jnp.minimum((i + 1) * hb, last_hb), u_blk)),
                  pl.BlockSpec((len(POOL_WINDOWS), POOL_GROUP, POOL_GROUP), lambda i: (0, 0, 0)),
                  pl.BlockSpec((1, POOL_WIDTH), lambda i: (0, 0))],
        out_specs=pl.BlockSpec((tm, POOL_WIDTH), lambda i: (i, 0)),
        out_shape=jax.ShapeDtypeStruct((length, POOL_WIDTH), BF16),
        scratch_shapes=[pltpu.VMEM((tm + 2 * POOL_HALO, POOL_WIDTH), F32)],
        compiler_params=_params("arbitrary"),
        name="pool",
    )(p, p, p, pool_w.astype(BF16), pool_scale.reshape(1, POOL_WIDTH))


SSD_HALO = BF16_SUBLANE
SSD_CONV_BLK = 1024
SSD_PAIRS = SSD_HEADS // 2


def _ssd_conv_kernel(u_ref, up_ref, un_ref, tab_ref, o_ref, ext_ref, *, tm, n_tiles):
    i = pl.program_id(0)
    prv, nxt = up_ref[...], un_ref[...]
    ext_ref[0:SSD_HALO, :] = jnp.where(i == 0, jnp.zeros_like(prv), prv).astype(F32)
    ext_ref[SSD_HALO:SSD_HALO + tm, :] = u_ref[...].astype(F32)
    ext_ref[SSD_HALO + tm:SSD_HALO + tm + SSD_HALO, :] = jnp.where(i == n_tiles - 1, jnp.zeros_like(nxt),
                                                                   nxt).astype(F32)
    tab = tab_ref[...]
    left = (SSD_CONV - 1) // 2
    acc = tab[SSD_CONV:SSD_CONV + 1]
    for k in range(SSD_CONV):
        acc = acc + ext_ref[pl.ds(SSD_HALO - left + k, tm), :] * tab[k:k + 1]
    o_ref[...] = (acc * jax.nn.sigmoid(acc)).astype(o_ref.dtype)


def _ssd_conv(p, blk0, conv_w, conv_b, tm=512):
    length = p.shape[0]
    tm = min(tm, length)
    n_tiles = length // tm
    hb = tm // SSD_HALO
    last_hb = length // SSD_HALO - 1
    cb = SSD_CONV_BLK
    tab = jnp.concatenate([conv_w, conv_b[None], jnp.zeros((8 - SSD_CONV - 1, SSD_XBC), F32)], axis=0)
    kern = functools.partial(_ssd_conv_kernel, tm=tm, n_tiles=n_tiles)
    return pl.pallas_call(
        kern,
        grid=(n_tiles, SSD_XBC // cb),
        in_specs=[pl.BlockSpec((tm, cb), lambda i, j: (i, blk0 + j)),
                  pl.BlockSpec((SSD_HALO, cb), lambda i, j: (jnp.maximum(i * hb - 1, 0), blk0 + j)),
                  pl.BlockSpec((SSD_HALO, cb), lambda i, j: (jnp.minimum((i + 1) * hb, last_hb), blk0 + j)),
                  pl.BlockSpec((8, cb), lambda i, j: (0, j))],
        out_specs=pl.BlockSpec((tm, cb), lambda i, j: (i, j)),
        out_shape=jax.ShapeDtypeStruct((length, SSD_XBC), BF16),
        scratch_shapes=[pltpu.VMEM((tm + 2 * SSD_HALO, cb), F32)],
        compiler_params=_params("arbitrary", "arbitrary"),
        name="ssd_conv",
    )(p, p, p, tab)


def _ssd_dt_kernel(raw_ref, bias_ref, alog_ref, dt_ref, da_ref):
    v = raw_ref[...] + bias_ref[...]
    dt = jnp.maximum(v, 0.0) + jnp.log1p(jnp.exp(-jnp.abs(v)))
    dt_ref[...] = dt
    da_ref[...] = dt * -jnp.exp(alog_ref[...])


def _ssd_dt(raw, dt_bias, a_log, tm=2048):
    length = raw.shape[0]
    tm = min(tm, length)
    pad = LANE - 2 * SSD_HEADS
    spec = pl.BlockSpec((tm, LANE), lambda i: (i, 0))
    vec = pl.BlockSpec((1, LANE), lambda i: (0, 0))
    out = jax.ShapeDtypeStruct((length, LANE), F32)
    return pl.pallas_call(
        _ssd_dt_kernel,
        grid=(length // tm,),
        in_specs=[spec, vec, vec],
        out_specs=[spec, spec],
        out_shape=[out, out],
        compiler_params=_params("arbitrary"),
        name="ssd_dt",
    )(raw, jnp.pad(dt_bias.reshape(1, -1), ((0, 0), (0, pad))), jnp.pad(a_log.reshape(1, -1), ((0, 0), (0, pad))))


def _pieces(a):
    out = []
    for _ in range(3):
        piece = a.astype(BF16)
        out.append(piece)
        a = a - piece.astype(F32)
    return out


def _ssd_scan_kernel(*refs, direction, reverse, finalize, chunks):
    if finalize:
        (x_ref, b_ref, c_ref, dac_ref, dtr_ref, dar_ref, h0_ref, yp_ref, z_ref, dsk_ref, nw_ref,
         y_ref, hn_ref, s_ref, yacc_ref) = refs
    else:
        x_ref, b_ref, c_ref, dac_ref, dtr_ref, dar_ref, h0_ref, y_ref, hn_ref, s_ref = refs
        yacc_ref = y_ref
    step = pl.program_id(0)
    nsteps = pl.num_programs(0)
    t_len = SSD_CHUNK

    @pl.when(step == 0)
    def _():
        s_ref[...] = h0_ref[...]

    row = lax.broadcasted_iota(jnp.int32, (t_len, t_len), 0)
    col = lax.broadcasted_iota(jnp.int32, (t_len, t_len), 1)
    mask = (col >= row) if reverse else (col <= row)
    mask_t = (row >= col) if reverse else (row <= col)
    ones = jnp.ones((t_len, t_len), BF16)
    tri = jnp.where(mask, 1.0, 0.0).astype(BF16)
    tri_t = jnp.where(mask_t, 1.0, 0.0).astype(BF16)
    hsl = slice(direction * SSD_HEADS, (direction + 1) * SSD_HEADS)
    lane = lax.broadcasted_iota(jnp.int32, (t_len, LANE), 1)
    first = lane < SSD_HEAD_DIM
    edge = slice(0, 1) if reverse else slice(t_len - 1, t_len)

    for u in range(chunks):
        q = chunks - 1 - u if reverse else u
        rs = slice(q * t_len, (q + 1) * t_len)
        _ssd_chunk(x_ref, b_ref, c_ref, dac_ref, dtr_ref, dar_ref, s_ref, yacc_ref, rs, hsl, direction,
                   mask, tri, tri_t, ones, first, edge)

    @pl.when(step == nsteps - 1)
    def _():
        hn_ref[...] = s_ref[...]

    if finalize:
        y = yacc_ref[...] + yp_ref[...] + x_ref[...].astype(F32) * dsk_ref[...]
        zz = z_ref[...].astype(F32)
        gated = y * (zz * jax.nn.sigmoid(zz))
        out = gated * lax.rsqrt(jnp.mean(gated * gated, axis=-1, keepdims=True) + RMS_EPS) * nw_ref[...]
        y_ref[...] = out.astype(y_ref.dtype)


def _ssd_chunk(x_ref, b_ref, c_ref, dac_ref, dtr_ref, dar_ref, s_ref, yacc_ref, rs, hsl, direction,
               mask, tri, tri_t, ones, first, edge):
    t_len = SSD_CHUNK
    g_col = sum(jnp.dot(tri, piece, preferred_element_type=F32) for piece in _pieces(dac_ref[rs, :]))
    da_row_p = _pieces(dar_ref[hsl, rs])
    g_row = sum(jnp.dot(piece, tri_t, preferred_element_type=F32) for piece in da_row_p)
    tot_row = sum(jnp.dot(piece, ones, preferred_element_type=F32) for piece in da_row_p)
    dt_row = dtr_ref[hsl, rs]
    wdt_row = jnp.exp(tot_row - g_row) * dt_row

    for grp in range(SSD_GROUPS):
        nsl = slice(grp * SSD_STATE, (grp + 1) * SSD_STATE)
        b_g, c_g = b_ref[rs, nsl], c_ref[rs, nsl]
        cb = lax.dot_general(c_g, b_g, _NT, preferred_element_type=F32)
        bt_g = b_g.astype(F32).T
        for k in range(grp * SSD_HPG // 2, (grp + 1) * SSD_HPG // 2):
            ms, es, bws = [], [], []
            for h in (2 * k, 2 * k + 1):
                j = direction * SSD_HEADS + h
                g_bc = jnp.broadcast_to(g_col[:, j:j + 1], (t_len, t_len))
                decay = jnp.where(mask, jnp.exp(g_bc - g_row[h:h + 1, :]), 0.0)
                ms.append((decay * cb * dt_row[h:h + 1, :]).astype(BF16))
                es.append(jnp.exp(g_bc))
                bws.append((bt_g * wdt_row[h:h + 1, :]).astype(BF16))
            csl = slice(k * LANE, (k + 1) * LANE)
            xp = x_ref[rs, csl]
            xa = jnp.where(first, xp, jnp.zeros_like(xp))
            xb = jnp.where(first, jnp.zeros_like(xp), xp)
            e_pair = jnp.where(first, es[0], es[1])
            s_old = s_ref[k]
            y = (jnp.dot(ms[0], xa, preferred_element_type=F32) + jnp.dot(ms[1], xb, preferred_element_type=F32)
                 + jnp.dot(c_g, s_old.astype(BF16), preferred_element_type=F32) * e_pair)
            s_ref[k] = (s_old * e_pair[edge] + jnp.dot(bws[0], xa, preferred_element_type=F32)
                        + jnp.dot(bws[1], xb, preferred_element_type=F32))
            yacc_ref[rs, csl] = y


def _ssd_scan(xbc, da_col, dt_row, da_row, h0, direction, final=None, chunks=2):
    length = xbc.shape[0]
    t_len = SSD_CHUNK
    n_chunks = length // t_len
    chunks = min(chunks, n_chunks)
    n_steps = n_chunks // chunks
    rows = chunks * t_len
    reverse = direction == 1
    pos = (lambda c: n_steps - 1 - c) if reverse else (lambda c: c)
    di, gn = SSD_D_INNER, SSD_GN
    state = jax.ShapeDtypeStruct((SSD_PAIRS, SSD_STATE, LANE), F32)
    state_spec = pl.BlockSpec((SSD_PAIRS, SSD_STATE, LANE), lambda c: (0, 0, 0))
    in_specs = [pl.BlockSpec((rows, di), lambda c: (pos(c), 0)),
                pl.BlockSpec((rows, gn), lambda c: (pos(c), di // gn)),
                pl.BlockSpec((rows, gn), lambda c: (pos(c), di // gn + 1)),
                pl.BlockSpec((rows, LANE), lambda c: (pos(c), 0)),
                pl.BlockSpec((2 * SSD_HEADS, rows), lambda c: (0, pos(c))),
                pl.BlockSpec((2 * SSD_HEADS, rows), lambda c: (0, pos(c))),
                state_spec]
    args = [xbc, xbc, xbc, da_col, dt_row, da_row, h0]
    scratch = [pltpu.VMEM((SSD_PAIRS, SSD_STATE, LANE), F32)]
    y_spec = pl.BlockSpec((rows, di), lambda c: (pos(c), 0))
    row_spec = pl.BlockSpec((1, di), lambda c: (0, 0))
    if final is not None:
        y_other, p, d_skip_row, norm_w_row = final
        in_specs += [y_spec, y_spec, row_spec, row_spec]
        args += [y_other, p, d_skip_row, norm_w_row]
        scratch.append(pltpu.VMEM((rows, di), F32))
    y_dtype = BF16 if final is not None else F32
    kern = functools.partial(_ssd_scan_kernel, direction=direction, reverse=reverse, finalize=final is not None,
                             chunks=chunks)
    return pl.pallas_call(
        kern,
        grid=(n_steps,),
        in_specs=in_specs,
        out_specs=[y_spec, state_spec],
        out_shape=[jax.ShapeDtypeStruct((length, di), y_dtype), state],
        scratch_shapes=scratch,
        compiler_params=_params("arbitrary"),
        name="ssd_scan",
    )(*args)


def _ssd_inputs(p, dt_raw, conv_w, conv_b, dt_bias, a_log):
    xbc = _ssd_conv(p, P_XBC // SSD_CONV_BLK, conv_w, conv_b)
    dt, da = _ssd_dt(dt_raw, dt_bias, a_log)
    nh = 2 * SSD_HEADS
    return xbc, da, dt[:, :nh].T, da[:, :nh].T


def _ssd_pallas(p_l, dt_raw_l, p_c, dt_raw_c, conv_w, conv_b, dt_bias, a_log, d_skip, norm_w, ctx_out):
    xbc_l, dac_l, dtr_l, dar_l = _ssd_inputs(p_l, dt_raw_l, conv_w, conv_b, dt_bias, a_log)
    xbc_c, dac_c, dtr_c, dar_c = _ssd_inputs(p_c, dt_raw_c, conv_w, conv_b, dt_bias, a_log)
    zero = jnp.zeros((SSD_PAIRS, SSD_STATE, LANE), F32)
    fin = (jnp.repeat(d_skip, SSD_HEAD_DIM).reshape(1, -1), norm_w.reshape(1, -1))
    yc0, h_fwd = _ssd_scan(xbc_c, dac_c, dtr_c, dar_c, zero, 0)
    if ctx_out:
        out_c, h_bwd = _ssd_scan(xbc_c, dac_c, dtr_c, dar_c, zero, 1, final=(yc0, p_c) + fin)
    else:
        out_c = None
        _, h_bwd = _ssd_scan(xbc_c, dac_c, dtr_c, dar_c, zero, 1)
    yl0, _ = _ssd_scan(xbc_l, dac_l, dtr_l, dar_l, h_fwd, 0)
    out_l, _ = _ssd_scan(xbc_l, dac_l, dtr_l, dar_l, h_bwd, 1, final=(yl0, p_l) + fin)
    return out_l, out_c


P_Z = 0
P_Q = P_Z + SSD_D_INNER
P_POOL = P_Q + NA_WIDTH
P_XBC = P_POOL + POOL_WIDTH
P_K = P_XBC + SSD_XBC
P_V = P_K + NA_WIDTH
P_COLS = P_V + NA_WIDTH


def _layer(layer, x, ctx, mod_l, mod_c, norm1_w, w_in, ssd_conv_w, ssd_conv_b, ssd_dt_bias, ssd_a_log, ssd_d,
           ssd_norm_w, na_q_norm_w, na_k_norm_w, na_rpb, pool_w, pool_scale, w_out_all, norm2_w, mlp_w_up,
           mlp_conv_w, mlp_conv_b, mlp_w_down_all, rope_l, rope_c, ctx_out):
    sh1_l, sc1_l, g1_l, sh2_l, sc2_l, g2_l = jnp.split(mod_l, N_MOD)
    sh1_c, sc1_c, g1_c, sh2_c, sc2_c, g2_c = jnp.split(mod_c, N_MOD)

    w_main = jnp.concatenate([w_in[:, :OFF_DT], w_in[:, OFF_K:]], axis=1).astype(BF16)
    w_dt = jnp.pad(w_in[:, OFF_DT:OFF_K], ((0, 0), (0, LANE - 2 * SSD_HEADS))).astype(BF16)
    w_up_b = mlp_w_up.astype(BF16)
    conv_tab = jnp.concatenate([mlp_conv_w, mlp_conv_b[None], jnp.zeros((4, 2 * D_FF), F32)], axis=0)

    h_l = _norm_mod(x, norm1_w, sc1_l, sh1_l)
    h_c = _norm_mod(ctx, norm1_w, sc1_c, sh1_c)
    p_l = _matmul(h_l, w_main, BF16)
    dt_l = _matmul(h_l, w_dt, F32)
    p_c = _matmul(h_c, w_main, BF16)
    dt_c = _matmul(h_c, w_dt, F32)

    y_ssd_l, y_ssd_c = _ssd_pallas(p_l, dt_l, p_c, dt_c, ssd_conv_w, ssd_conv_b, ssd_dt_bias, ssd_a_log, ssd_d,
                                   ssd_norm_w, ctx_out)
    q_blk, k_blk, v_blk = P_Q // NA_WIDTH, P_K // NA_WIDTH, P_V // NA_HEAD_DIM
    q_r, k_r = _qk_prep(p_l, q_blk, k_blk, na_q_norm_w, na_k_norm_w, rope_l)
    qc_r, kc_r = _qk_prep(p_c, q_blk, k_blk, na_q_norm_w, na_k_norm_w, rope_c)
    y_na_l = _na_latent(q_r, k_r, p_l, v_blk, kc_r, p_c, v_blk, _na_bias_table(na_rpb))
    y_pool_l = _pool(p_l, P_POOL // POOL_WIDTH, pool_w, pool_scale)

    x_mid = _outproj([y_ssd_l, y_na_l, y_pool_l], w_out_all, layer, x, g1_l)
    h2_l = _norm_mod(x_mid, norm2_w, sc2_l, sh2_l)
    x_new = _conv_ffn(h2_l, w_up_b, conv_tab, mlp_w_down_all, layer, x_mid, g2_l)
    ctx_new = ctx
    if ctx_out:
        y_na_c = _na_context(qc_r, kc_r, p_c, v_blk)
        y_pool_c = _pool(p_c, P_POOL // POOL_WIDTH, pool_w, pool_scale)
        c_mid = _outproj([y_ssd_c, y_na_c, y_pool_c], w_out_all, layer, ctx, g1_c)
        h2_c = _norm_mod(c_mid, norm2_w, sc2_c, sh2_c)
        ctx_new = _conv_ffn(h2_c, w_up_b, conv_tab, mlp_w_down_all, layer, c_mid, g2_c)
    return x_new, ctx_new


def kernel(x, c, ctx, c_ctx, ada_w, ada_b, norm1_w, w_in, ssd_conv_w, ssd_conv_b, ssd_dt_bias, ssd_a_log,
           ssd_d, ssd_norm_w, na_q_norm_w, na_k_norm_w, na_rpb, pool_w, pool_scale, w_out, norm2_w,
           mlp_w_up, mlp_conv_w, mlp_conv_b, mlp_w_down):
    depth = ada_w.shape[0]
    d = x.shape[-1]
    c_rows = jnp.concatenate([c.reshape(1, d), c_ctx.reshape(1, d), jnp.zeros((MOD_ROWS - 2, d), F32)], axis=0)
    mods = _modulation(c_rows, ada_w, ada_b)
    xs, cs = x[0], ctx[0]
    rope_l = _rope_tables(xs.shape[0])
    ctx_tab = (cs.shape[0], NA_HEAD_DIM)
    rope_c = (jnp.ones(ctx_tab, F32), jnp.zeros(ctx_tab, F32), jnp.zeros(ctx_tab, F32))
    for i in range(depth):
        xs, cs = _layer(i, xs, cs, mods[i, 0], mods[i, 1], norm1_w[i], w_in[i], ssd_conv_w[i], ssd_conv_b[i],
                        ssd_dt_bias[i], ssd_a_log[i], ssd_d[i], ssd_norm_w[i], na_q_norm_w[i], na_k_norm_w[i],
                        na_rpb[i], pool_w[i], pool_scale[i], w_out, norm2_w[i], mlp_w_up[i], mlp_conv_w[i],
                        mlp_conv_b[i], mlp_w_down, rope_l, rope_c, ctx_out=(i < depth - 1))
    return xs[None]
```

```python
import functools
import math

import jax
import jax.numpy as jnp
from jax import lax
from jax.experimental import pallas as pl
from jax.experimental.pallas import tpu as pltpu

F32 = jnp.float32
BF16 = jnp.bfloat16

D_MODEL = 4096
DEPTH = 2
GRID_W = 64
D_MIX = D_MODEL
SSD_D_INNER = D_MIX // 2
SSD_HEAD_DIM = 64
SSD_HEADS = SSD_D_INNER // SSD_HEAD_DIM
SSD_GROUPS = 4
SSD_HPG = SSD_HEADS // SSD_GROUPS
SSD_STATE = 128
SSD_GN = SSD_GROUPS * SSD_STATE
SSD_XBC = SSD_D_INNER + 2 * SSD_GN
SSD_CONV = 5
SSD_CHUNK = 128
NA_WIDTH = D_MIX // 4
NA_HEAD_DIM = 128
NA_HEADS = NA_WIDTH // NA_HEAD_DIM
NA_WIN_R = 8
NA_WIN_C = 16
ROPE_BASE = 10000.0
POOL_WIDTH = D_MIX - SSD_D_INNER - NA_WIDTH
POOL_WINDOWS = (2, 4, 8, 16)
POOL_GROUP = POOL_WIDTH // len(POOL_WINDOWS)
D_FF = 11008
N_MOD = 6
RMS_EPS = 1e-6
OFF_Z = 0
OFF_Q = OFF_Z + SSD_D_INNER
OFF_POOL = OFF_Q + NA_WIDTH
OFF_XBC = OFF_POOL + POOL_WIDTH
OFF_DT = OFF_XBC + SSD_XBC
OFF_K = OFF_DT + 2 * SSD_HEADS
OFF_V = OFF_K + NA_WIDTH
IN_COLS = OFF_V + NA_WIDTH

VMEM_LIMIT_BYTES = 58 * 1024 * 1024
LANE = 128
BF16_SUBLANE = 16
MOD_ROWS = 8


def _params(*sem):
    return pltpu.CompilerParams(dimension_semantics=sem, vmem_limit_bytes=VMEM_LIMIT_BYTES)


def _mod_kernel(c_ref, w_ref, b_ref, o_ref):
    c = c_ref[...]
    s = (c * jax.nn.sigmoid(c)).astype(BF16)
    o_ref[0] = jnp.dot(s, w_ref[0].astype(BF16), preferred_element_type=F32) + b_ref[0]


def _modulation(c_rows, ada_w, ada_b, tn=1024):
    depth, d, n = ada_w.shape
    return pl.pallas_call(
        _mod_kernel,
        grid=(depth, n // tn),
        in_specs=[pl.BlockSpec((MOD_ROWS, d), lambda l, j: (0, 0)),
                  pl.BlockSpec((1, d, tn), lambda l, j: (l, 0, j)),
                  pl.BlockSpec((1, 1, tn), lambda l, j: (l, 0, j))],
        out_specs=pl.BlockSpec((1, MOD_ROWS, tn), lambda l, j: (l, 0, j)),
        out_shape=jax.ShapeDtypeStruct((depth, MOD_ROWS, n), F32),
        compiler_params=_params("arbitrary", "arbitrary"),
        name="modulation",
    )(c_rows, ada_w, ada_b.reshape(depth, 1, n))


def _norm_kernel(x_ref, w_ref, sc_ref, sh_ref, o_ref):
    x = x_ref[...]
    y = x * lax.rsqrt(jnp.mean(x * x, axis=-1, keepdims=True) + RMS_EPS) * w_ref[...]
    o_ref[...] = (y * (1.0 + sc_ref[...]) + sh_ref[...]).astype(o_ref.dtype)


def _norm_mod(x, w, scale, shift, tm=512):
    m, d = x.shape
    tm = min(tm, m)
    row = pl.BlockSpec((1, d), lambda i: (0, 0))
    return pl.pallas_call(
        _norm_kernel,
        grid=(m // tm,),
        in_specs=[pl.BlockSpec((tm, d), lambda i: (i, 0)), row, row, row],
        out_specs=pl.BlockSpec((tm, d), lambda i: (i, 0)),
        out_shape=jax.ShapeDtypeStruct((m, d), BF16),
        compiler_params=_params("arbitrary"),
        name="norm_mod",
    )(x, w.reshape(1, d), scale.reshape(1, d), shift.reshape(1, d))


def _mm_kernel(a_ref, b_ref, o_ref):
    o_ref[...] = jnp.dot(a_ref[...], b_ref[0], preferred_element_type=F32).astype(o_ref.dtype)


def _matmul(a, b_all, layer, out_dtype, tm=1024, tn=1024):
    m, k = a.shape
    n = b_all.shape[2]
    tm, tn = min(tm, m), min(tn, n)
    return pl.pallas_call(
        _mm_kernel,
        grid=(m // tm, n // tn),
        in_specs=[pl.BlockSpec((tm, k), lambda i, j: (i, 0)),
                  pl.BlockSpec((1, k, tn), lambda i, j: (layer, 0, j))],
        out_specs=pl.BlockSpec((tm, tn), lambda i, j: (i, j)),
        out_shape=jax.ShapeDtypeStruct((m, n), out_dtype),
        compiler_params=_params("arbitrary", "arbitrary"),
        name="matmul",
    )(a, b_all)


def _outproj_kernel(*refs, widths):
    a_refs, (w_ref, x_ref, g_ref, o_ref) = refs[:len(widths)], refs[len(widths):]
    acc, off = None, 0
    for a_ref, k in zip(a_refs, widths):
        part = jnp.dot(a_ref[...], w_ref[0, off:off + k, :], preferred_element_type=F32)
        acc = part if acc is None else acc + part
        off += k
    o_ref[...] = x_ref[...] + g_ref[...] * acc


def _outproj(parts, w_all, layer, x, gate, tm=1024, tn=1024):
    m = x.shape[0]
    _, k, n = w_all.shape
    tm = min(tm, m)
    widths = tuple(a.shape[1] for a in parts)
    assert sum(widths) == k
    return pl.pallas_call(
        functools.partial(_outproj_kernel, widths=widths),
        grid=(m // tm, n // tn),
        in_specs=[pl.BlockSpec((tm, kw), lambda i, j: (i, 0)) for kw in widths] + [
            pl.BlockSpec((1, k, tn), lambda i, j: (layer, 0, j)),
            pl.BlockSpec((tm, tn), lambda i, j: (i, j)),
            pl.BlockSpec((1, tn), lambda i, j: (0, j))],
        out_specs=pl.BlockSpec((tm, tn), lambda i, j: (i, j)),
        out_shape=jax.ShapeDtypeStruct((m, n), F32),
        compiler_params=_params("arbitrary", "arbitrary"),
        name="outproj",
    )(*parts, w_all, x, gate.reshape(1, n))


FFN_HALO = BF16_SUBLANE


def _ffn_kernel(h_ref, hp_ref, hn_ref, wg_ref, wv_ref, tab_ref, wd_ref, x_ref, g_ref,
                o_ref, ext_ref, ua_ref, ub_ref, *, tm, n_mtiles, nf, tn_d):
    i = pl.program_id(0)
    f = pl.program_id(1)
    d = o_ref.shape[1]
    u_refs = (ua_ref, ub_ref)

    @pl.when(f == 0)
    def _():
        ext_ref[0:tm, :] = h_ref[...]
        nxt = hn_ref[...]
        prv = hp_ref[...]
        ext_ref[tm:tm + FFN_HALO, :] = jnp.where(i == n_mtiles - 1, jnp.zeros_like(nxt), nxt)
        ext_ref[tm + FFN_HALO:tm + 2 * FFN_HALO, :] = jnp.where(i == 0, jnp.zeros_like(prv), prv)
        o_ref[...] = jnp.zeros_like(o_ref)
        ub_ref[...] = jnp.zeros_like(ub_ref)

    def conv(u_ref, br, c):
        um = u_ref[br, 0:tm]
        row = lax.broadcasted_iota(jnp.int32, um.shape, 0)
        before = tm + 2 * FFN_HALO - 1
        up = jnp.where(row == 0, u_ref[br, before:before + 1], pltpu.roll(um, 1, 0))
        un = jnp.where(row == tm - 1, u_ref[br, tm:tm + 1], pltpu.roll(um, tm - 1, 0))
        return up * c[0:1] + um * c[1:2] + un * c[2:3] + c[3:4]

    def step(produce, consume):
        h = ext_ref[...]
        p_ref, c_ref = u_refs[produce], u_refs[consume]
        p_ref[0] = jnp.dot(h, wg_ref[0], preferred_element_type=F32)
        p_ref[1] = jnp.dot(h, wv_ref[0], preferred_element_type=F32)
        tile = jnp.maximum(f - 1, 0)
        gate = conv(c_ref, 0, tab_ref[tile])
        val = conv(c_ref, 1, tab_ref[tile + nf])
        act = gate * jax.nn.sigmoid(gate) * val
        act = jnp.where(f > 0, act, jnp.zeros_like(act)).astype(BF16)
        for n in range(d // tn_d):
            sl = slice(n * tn_d, (n + 1) * tn_d)
            o_ref[:, sl] += jnp.dot(act, wd_ref[0, :, sl].astype(BF16), preferred_element_type=F32)

    @pl.when(f % 2 == 0)
    def _():
        step(0, 1)

    @pl.when(f % 2 == 1)
    def _():
        step(1, 0)

    @pl.when(f == nf)
    def _():
        o_ref[...] = x_ref[...] + g_ref[...] * o_ref[...]


def _conv_ffn(h, w_up_all, conv_tab, w_down_all, layer, x, gate, tm=512, tf=256, tn_d=1024):
    m, d = h.shape
    ff = w_down_all.shape[1]
    tm = min(tm, m)
    tn_d = min(tn_d, d)
    n_mtiles, nf = m // tm, ff // tf
    hb = tm // FFN_HALO
    last_hb = m // FFN_HALO - 1
    kern = functools.partial(_ffn_kernel, tm=tm, n_mtiles=n_mtiles, nf=nf, tn_d=tn_d)
    tab = conv_tab.reshape(8, 2 * nf, tf).transpose(1, 0, 2)

    def prod(f):
        return jnp.minimum(f, nf - 1)

    def cons(f):
        return jnp.maximum(f - 1, 0)

    return pl.pallas_call(
        kern,
        grid=(n_mtiles, nf + 1),
        in_specs=[
            pl.BlockSpec((tm, d), lambda i, f: (i, 0), pipeline_mode=pl.Buffered(1)),
            pl.BlockSpec((FFN_HALO, d), lambda i, f: (jnp.maximum(i * hb - 1, 0), 0)),
            pl.BlockSpec((FFN_HALO, d), lambda i, f: (jnp.minimum((i + 1) * hb, last_hb), 0)),
            pl.BlockSpec((1, d, tf), lambda i, f: (layer, 0, prod(f))),
            pl.BlockSpec((1, d, tf), lambda i, f: (layer, 0, prod(f) + nf)),
            pl.BlockSpec((2 * nf, 8, tf), lambda i, f: (0, 0, 0)),
            pl.BlockSpec((1, tf, d), lambda i, f: (layer, cons(f), 0)),
            pl.BlockSpec((tm, d), lambda i, f: (i, 0), pipeline_mode=pl.Buffered(1)),
            pl.BlockSpec((1, d), lambda i, f: (0, 0)),
        ],
        out_specs=pl.BlockSpec((tm, d), lambda i, f: (i, 0)),
        out_shape=jax.ShapeDtypeStruct((m, d), F32),
        scratch_shapes=[pltpu.VMEM((tm + 2 * FFN_HALO, d), BF16),
                        pltpu.VMEM((2, tm + 2 * FFN_HALO, tf), F32),
                        pltpu.VMEM((2, tm + 2 * FFN_HALO, tf), F32)],
        compiler_params=_params("arbitrary", "arbitrary"),
        name="conv_ffn",
    )(h, h, h, w_up_all, w_up_all, tab, w_down_all, x, gate.reshape(1, d))


def _rope_tables(length):
    t = jnp.arange(length)
    half = NA_HEAD_DIM // 2
    inv = ROPE_BASE ** (-jnp.arange(0, half, 2, dtype=F32) / half)
    ar = (t // GRID_W).astype(F32)[:, None] * inv
    ac = (t % GRID_W).astype(F32)[:, None] * inv
    ang = jnp.concatenate([ar, ar, ac, ac], axis=-1)
    cos, sin = jnp.cos(ang), jnp.sin(ang)
    first = (jnp.arange(NA_HEAD_DIM) % half) < half // 2
    return cos, jnp.where(first, -sin, 0.0), jnp.where(first, 0.0, sin)


def _qk_prep_kernel(q_ref, k_ref, qw_ref, kw_ref, cos_ref, sa_ref, sb_ref, qo_ref, ko_ref):
    cos, sa, sb = cos_ref[...], sa_ref[...], sb_ref[...]
    quarter = NA_HEAD_DIM // 4
    for h in range(NA_HEADS):
        sl = slice(h * NA_HEAD_DIM, (h + 1) * NA_HEAD_DIM)
        for src, w_ref, dst in ((q_ref, qw_ref, qo_ref), (k_ref, kw_ref, ko_ref)):
            x = src[:, sl].astype(F32)
            y = x * lax.rsqrt(jnp.mean(x * x, axis=-1, keepdims=True) + RMS_EPS) * w_ref[...]
            y = (y * cos + pltpu.roll(y, NA_HEAD_DIM - quarter, 1) * sa + pltpu.roll(y, quarter, 1) * sb)
            dst[:, sl] = y.astype(dst.dtype)


def _qk_prep(p, q_blk, k_blk, qw, kw, tables, tm=512):
    m = p.shape[0]
    tm = min(tm, m)
    tab = pl.BlockSpec((tm, NA_HEAD_DIM), lambda i: (i, 0))
    vec = pl.BlockSpec((1, NA_HEAD_DIM), lambda i: (0, 0))
    out = jax.ShapeDtypeStruct((m, NA_WIDTH), BF16)
    return pl.pallas_call(
        _qk_prep_kernel,
        grid=(m // tm,),
        in_specs=[pl.BlockSpec((tm, NA_WIDTH), lambda i: (i, q_blk)),
                  pl.BlockSpec((tm, NA_WIDTH), lambda i: (i, k_blk)), vec, vec, tab, tab, tab],
        out_specs=[pl.BlockSpec((tm, NA_WIDTH), lambda i: (i, 0))] * 2,
        out_shape=[out, out],
        compiler_params=_params("arbitrary"),
        name="qk_prep",
    )(p, p, qw.reshape(1, -1), kw.reshape(1, -1), *tables)


NA_DR = 2 * NA_WIN_R
NA_BIAS_CHUNK = 1024


def _na_bias_kernel(r0_ref, r1_ref, o_ref):
    def pieces(r):
        out = []
        for _ in range(3):
            piece = r.astype(BF16)
            out.append(piece)
            r = r - piece.astype(F32)
        return out

    p0, p1 = pieces(r0_ref[0]), pieces(r1_ref[0])
    for c in range(GRID_W * LANE // NA_BIAS_CHUNK):
        shape = (LANE, NA_BIAS_CHUNK)
        pos = c * NA_BIAS_CHUNK + lax.broadcasted_iota(jnp.int32, shape, 1)
        j = lax.broadcasted_iota(jnp.int32, shape, 0)
        qc = pos // LANE
        lane = pos % LANE
        kc = lane % GRID_W
        idx = jnp.clip(kc - qc + (NA_WIN_C - 1), 0, 2 * NA_WIN_C - 2)
        hit = idx == j
        second = lane >= GRID_W
        oh0 = jnp.where(hit, jnp.where(second, 0.0, 1.0), 0.0).astype(BF16)
        oh1 = jnp.where(hit, jnp.where(second, 1.0, 0.0), 0.0).astype(BF16)
        acc = jnp.zeros((NA_DR, NA_BIAS_CHUNK), F32)
        for a, b in zip(p0, p1):
            acc = acc + jnp.dot(a, oh0, preferred_element_type=F32) + jnp.dot(b, oh1, preferred_element_type=F32)
        c0 = jnp.clip(qc[0:1] - NA_WIN_C // 2, 0, GRID_W - NA_WIN_C)
        ok = (kc[0:1] >= c0) & (kc[0:1] < c0 + NA_WIN_C)
        o_ref[0, :, c * NA_BIAS_CHUNK:(c + 1) * NA_BIAS_CHUNK] = jnp.where(ok, acc, -jnp.inf)


def _na_bias_table(rpb):
    nh, ndr, ndc = rpb.shape
    r0 = jnp.pad(rpb, ((0, 0), (0, NA_DR - ndr), (0, LANE - ndc)))
    r1 = jnp.pad(rpb[:, 1:], ((0, 0), (0, NA_DR - ndr + 1), (0, LANE - ndc)))
    spec = pl.BlockSpec((1, NA_DR, LANE), lambda h: (h, 0, 0))
    out = pl.pallas_call(
        _na_bias_kernel,
        grid=(nh,),
        in_specs=[spec, spec],
        out_specs=pl.BlockSpec((1, NA_DR, GRID_W * LANE), lambda h: (h, 0, 0)),
        out_shape=jax.ShapeDtypeStruct((nh, NA_DR, GRID_W * LANE), F32),
        compiler_params=_params("arbitrary"),
        name="na_bias",
    )(r0, r1)
    return out.reshape(nh, NA_DR, GRID_W, LANE)


def _softmax_parts(scores):
    m = functools.reduce(jnp.maximum, [jnp.max(s, axis=1, keepdims=True) for s in scores])
    ps = [jnp.exp(s - m) for s in scores]
    den = functools.reduce(jnp.add, [jnp.sum(p, axis=1, keepdims=True) for p in ps])
    return [p.astype(BF16) for p in ps], den


def _pv(parts, values):
    ps, den = parts
    o = functools.reduce(jnp.add, [jnp.dot(p, v, preferred_element_type=F32) for p, v in zip(ps, values)])
    return o / den


def _softmax_pv(scores, values):
    return _pv(_softmax_parts(scores), values)


_NT = (((1,), (1,)), ((), ()))


def _na_kernel(q_ref, k_ref, v_ref, kc_ref, vc_ref, tz_ref, o_ref, *, rows_per_step, n_rows):
    t = pl.program_id(1)
    scale = NA_HEAD_DIM ** -0.5
    band = NA_WIN_R * GRID_W
    kc, vc = kc_ref[...], vc_ref[...]
    starts, scores = [], []
    for i in range(rows_per_step):
        qr = t * rows_per_step + i
        bs = jnp.clip(qr - NA_WIN_R // 2, 0, n_rows - NA_WIN_R)
        off = bs - qr + (NA_WIN_R - 1)
        start = pl.multiple_of(bs * GRID_W, GRID_W)
        q = q_ref[i * GRID_W:(i + 1) * GRID_W, :]
        kb = k_ref[pl.ds(start, band), :]
        bias = jnp.concatenate([tz_ref[0, off + 2 * w] for w in range(NA_WIN_R // 2)], axis=1)
        s_loc = lax.dot_general(q, kb, _NT, preferred_element_type=F32) * scale + bias
        s_ctx = lax.dot_general(q, kc, _NT, preferred_element_type=F32) * scale
        starts.append(start)
        scores.append((s_loc, s_ctx))
    probs = [_softmax_parts(list(s)) for s in scores]
    for i in range(rows_per_step):
        vb = v_ref[pl.ds(starts[i], band), :]
        o = _pv(probs[i], [vb, vc])
        o_ref[i * GRID_W:(i + 1) * GRID_W, :] = o.astype(o_ref.dtype)


def _na_latent(q_r, k_r, p_l, v_blk, kc_r, p_c, vc_blk, tz, rows_per_step=8):
    length = q_r.shape[0]
    lc = kc_r.shape[0]
    n_rows = length // GRID_W
    assert n_rows >= NA_WIN_R and n_rows % rows_per_step == 0
    hd = NA_HEAD_DIM
    kern = functools.partial(_na_kernel, rows_per_step=rows_per_step, n_rows=n_rows)
    return pl.pallas_call(
        kern,
        grid=(NA_HEADS, n_rows // rows_per_step),
        in_specs=[pl.BlockSpec((rows_per_step * GRID_W, hd), lambda h, t: (t, h)),
                  pl.BlockSpec((length, hd), lambda h, t: (0, h)),
                  pl.BlockSpec((length, hd), lambda h, t: (0, v_blk + h)),
                  pl.BlockSpec((lc, hd), lambda h, t: (0, h)),
                  pl.BlockSpec((lc, hd), lambda h, t: (0, vc_blk + h)),
                  pl.BlockSpec((1, NA_DR, GRID_W, LANE), lambda h, t: (h, 0, 0, 0))],
        out_specs=pl.BlockSpec((rows_per_step * GRID_W, hd), lambda h, t: (t, h)),
        out_shape=jax.ShapeDtypeStruct((length, NA_WIDTH), BF16),
        compiler_params=_params("arbitrary", "arbitrary"),
        name="na_latent",
    )(q_r, k_r, p_l, kc_r, p_c, tz)


def _ctx_attn_kernel(q_ref, k_ref, v_ref, o_ref):
    s = lax.dot_general(q_ref[...], k_ref[...], _NT, preferred_element_type=F32) * (NA_HEAD_DIM ** -0.5)
    o_ref[...] = _softmax_pv([s], [v_ref[...]]).astype(o_ref.dtype)


def _na_context(qc_r, kc_r, p_c, vc_blk):
    lc = qc_r.shape[0]
    hd = NA_HEAD_DIM
    spec = pl.BlockSpec((lc, hd), lambda h: (0, h))
    return pl.pallas_call(
        _ctx_attn_kernel,
        grid=(NA_HEADS,),
        in_specs=[spec, spec, pl.BlockSpec((lc, hd), lambda h: (0, vc_blk + h))],
        out_specs=spec,
        out_shape=jax.ShapeDtypeStruct((lc, NA_WIDTH), BF16),
        compiler_params=_params("arbitrary"),
        name="na_context",
    )(qc_r, kc_r, p_c)


POOL_HALO = BF16_SUBLANE


def _pool_kernel(u_ref, up_ref, un_ref, w_ref, sc_ref, o_ref, ext_ref, *, tm, n_tiles, length):
    i = pl.program_id(0)
    prv, nxt = up_ref[...], un_ref[...]
    ext_ref[0:POOL_HALO, :] = jnp.where(i == 0, jnp.zeros_like(prv), prv).astype(F32)
    ext_ref[POOL_HALO:POOL_HALO + tm, :] = u_ref[...].astype(F32)
    ext_ref[POOL_HALO + tm:POOL_HALO + tm + POOL_HALO, :] = jnp.where(i == n_tiles - 1, jnp.zeros_like(nxt),
                                                                     nxt).astype(F32)
    t = i * tm + lax.broadcasted_iota(jnp.int32, (tm, 1), 0)
    for g, w in enumerate(POOL_WINDOWS):
        cs = slice(g * POOL_GROUP, (g + 1) * POOL_GROUP)
        acc = ext_ref[pl.ds(POOL_HALO - w // 2, tm), cs]
        for k in range(1, w):
            acc = acc + ext_ref[pl.ds(POOL_HALO - w // 2 + k, tm), cs]
        cnt = (jnp.minimum(t + w // 2, length) - jnp.maximum(t - w // 2, 0)).astype(F32)
        pooled = acc / cnt - ext_ref[pl.ds(POOL_HALO, tm), cs]
        y = jnp.dot(pooled.astype(BF16), w_ref[g], preferred_element_type=F32) * sc_ref[:, cs]
        o_ref[:, cs] = y.astype(o_ref.dtype)


def _pool(p, u_blk, pool_w, pool_scale, tm=512):
    length = p.shape[0]
    tm = min(tm, length)
    n_tiles = length // tm
    hb = tm // POOL_HALO
    last_hb = length // POOL_HALO - 1
    kern = functools.partial(_pool_kernel, tm=tm, n_tiles=n_tiles, length=length)
    return pl.pallas_call(
        kern,
        grid=(n_tiles,),
        in_specs=[pl.BlockSpec((tm, POOL_WIDTH), lambda i: (i, u_blk)),
                  pl.BlockSpec((POOL_HALO, POOL_WIDTH), lambda i: (jnp.maximum(i * hb - 1, 0), u_blk)),
                  pl.BlockSpec((POOL_HALO, POOL_WIDTH), lambda i: (jnp.minimum((i + 1) * hb, last_hb), u_blk)),
                  pl.BlockSpec((len(POOL_WINDOWS), POOL_GROUP, POOL_GROUP), lambda i: (0, 0, 0)),
                  pl.BlockSpec((1, POOL_WIDTH), lambda i: (0, 0))],
        out_specs=pl.BlockSpec((tm, POOL_WIDTH), lambda i: (i, 0)),
        out_shape=jax.ShapeDtypeStruct((length, POOL_WIDTH), BF16),
        scratch_shapes=[pltpu.VMEM((tm + 2 * POOL_HALO, POOL_WIDTH), F32)],
        compiler_params=_params("arbitrary"),
        name="pool",
    )(p, p, p, pool_w.astype(BF16), pool_scale.reshape(1, POOL_WIDTH))


SSD_HALO = BF16_SUBLANE
SSD_CONV_BLK = 1024
SSD_PAIRS = SSD_HEADS // 2


def _ssd_conv_kernel(u_ref, up_ref, un_ref, tab_ref, o_ref, ext_ref, *, tm, n_tiles):
    i = pl.program_id(0)
    prv, nxt = up_ref[...], un_ref[...]
    ext_ref[0:SSD_HALO, :] = jnp.where(i == 0, jnp.zeros_like(prv), prv).astype(F32)
    ext_ref[SSD_HALO:SSD_HALO + tm, :] = u_ref[...].astype(F32)
    ext_ref[SSD_HALO + tm:SSD_HALO + tm + SSD_HALO, :] = jnp.where(i == n_tiles - 1, jnp.zeros_like(nxt),
                                                                   nxt).astype(F32)
    tab = tab_ref[...]
    left = (SSD_CONV - 1) // 2
    acc = tab[SSD_CONV:SSD_CONV + 1]
    for k in range(SSD_CONV):
        acc = acc + ext_ref[pl.ds(SSD_HALO - left + k, tm), :] * tab[k:k + 1]
    o_ref[...] = (acc * jax.nn.sigmoid(acc)).astype(o_ref.dtype)


def _ssd_conv(p, blk0, conv_w, conv_b, tm=512):
    length = p.shape[0]
    tm = min(tm, length)
    n_tiles = length // tm
    hb = tm // SSD_HALO
    last_hb = length // SSD_HALO - 1
    cb = SSD_CONV_BLK
    tab = jnp.concatenate([conv_w, conv_b[None], jnp.zeros((8 - SSD_CONV - 1, SSD_XBC), F32)], axis=0)
    kern = functools.partial(_ssd_conv_kernel, tm=tm, n_tiles=n_tiles)
    return pl.pallas_call(
        kern,
        grid=(n_tiles, SSD_XBC // cb),
        in_specs=[pl.BlockSpec((tm, cb), lambda i, j: (i, blk0 + j)),
                  pl.BlockSpec((SSD_HALO, cb), lambda i, j: (jnp.maximum(i * hb - 1, 0), blk0 + j)),
                  pl.BlockSpec((SSD_HALO, cb), lambda i, j: (jnp.minimum((i + 1) * hb, last_hb), blk0 + j)),
                  pl.BlockSpec((8, cb), lambda i, j: (0, j))],
        out_specs=pl.BlockSpec((tm, cb), lambda i, j: (i, j)),
        out_shape=jax.ShapeDtypeStruct((length, SSD_XBC), BF16),
        scratch_shapes=[pltpu.VMEM((tm + 2 * SSD_HALO, cb), F32)],
        compiler_params=_params("arbitrary", "arbitrary"),
        name="ssd_conv",
    )(p, p, p, tab)


def _ssd_dt_kernel(raw_ref, bias_ref, alog_ref, dt_ref, da_ref):
    v = raw_ref[...] + bias_ref[...]
    dt = jnp.maximum(v, 0.0) + jnp.log1p(jnp.exp(-jnp.abs(v)))
    dt_ref[...] = dt
    da_ref[...] = dt * -jnp.exp(alog_ref[...])


def _ssd_dt(raw, dt_bias, a_log, tm=2048):
    length = raw.shape[0]
    tm = min(tm, length)
    pad = LANE - 2 * SSD_HEADS
    spec = pl.BlockSpec((tm, LANE), lambda i: (i, 0))
    vec = pl.BlockSpec((1, LANE), lambda i: (0, 0))
    out = jax.ShapeDtypeStruct((length, LANE), F32)
    return pl.pallas_call(
        _ssd_dt_kernel,
        grid=(length // tm,),
        in_specs=[spec, vec, vec],
        out_specs=[spec, spec],
        out_shape=[out, out],
        compiler_params=_params("arbitrary"),
        name="ssd_dt",
    )(raw, jnp.pad(dt_bias.reshape(1, -1), ((0, 0), (0, pad))), jnp.pad(a_log.reshape(1, -1), ((0, 0), (0, pad))))


def _pieces(a):
    out = []
    for _ in range(3):
        piece = a.astype(BF16)
        out.append(piece)
        a = a - piece.astype(F32)
    return out


def _ssd_scan_kernel(*refs, direction, reverse, finalize, chunks):
    if finalize:
        (x_ref, b_ref, c_ref, dac_ref, dtr_ref, dar_ref, h0_ref, yp_ref, z_ref, dsk_ref, nw_ref,
         y_ref, hn_ref, s_ref, yacc_ref) = refs
    else:
        x_ref, b_ref, c_ref, dac_ref, dtr_ref, dar_ref, h0_ref, y_ref, hn_ref, s_ref = refs
        yacc_ref = y_ref
    step = pl.program_id(0)
    nsteps = pl.num_programs(0)
    t_len = SSD_CHUNK

    @pl.when(step == 0)
    def _():
        s_ref[...] = h0_ref[...]

    row = lax.broadcasted_iota(jnp.int32, (t_len, t_len), 0)
    col = lax.broadcasted_iota(jnp.int32, (t_len, t_len), 1)
    mask = (col >= row) if reverse else (col <= row)
    mask_t = (row >= col) if reverse else (row <= col)
    ones = jnp.ones((t_len, t_len), BF16)
    tri = jnp.where(mask, 1.0, 0.0).astype(BF16)
    tri_t = jnp.where(mask_t, 1.0, 0.0).astype(BF16)
    hsl = slice(direction * SSD_HEADS, (direction + 1) * SSD_HEADS)
    lane = lax.broadcasted_iota(jnp.int32, (t_len, LANE), 1)
    first = lane < SSD_HEAD_DIM
    edge = slice(0, 1) if reverse else slice(t_len - 1, t_len)

    for u in range(chunks):
        q = chunks - 1 - u if reverse else u
        rs = slice(q * t_len, (q + 1) * t_len)
        _ssd_chunk(x_ref, b_ref, c_ref, dac_ref, dtr_ref, dar_ref, s_ref, yacc_ref, rs, hsl, direction,
                   mask, tri, tri_t, ones, first, edge)

    @pl.when(step == nsteps - 1)
    def _():
        hn_ref[...] = s_ref[...]

    if finalize:
        y = yacc_ref[...] + yp_ref[...] + x_ref[...].astype(F32) * dsk_ref[...]
        zz = z_ref[...].astype(F32)
        gated = y * (zz * jax.nn.sigmoid(zz))
        out = gated * lax.rsqrt(jnp.mean(gated * gated, axis=-1, keepdims=True) + RMS_EPS) * nw_ref[...]
        y_ref[...] = out.astype(y_ref.dtype)


def _ssd_chunk(x_ref, b_ref, c_ref, dac_ref, dtr_ref, dar_ref, s_ref, yacc_ref, rs, hsl, direction,
               mask, tri, tri_t, ones, first, edge):
    t_len = SSD_CHUNK
    g_col = sum(jnp.dot(tri, piece, preferred_element_type=F32) for piece in _pieces(dac_ref[rs, :]))
    da_row_p = _pieces(dar_ref[hsl, rs])
    g_row = sum(jnp.dot(piece, tri_t, preferred_element_type=F32) for piece in da_row_p)
    tot_row = sum(jnp.dot(piece, ones, preferred_element_type=F32) for piece in da_row_p)
    dt_row = dtr_ref[hsl, rs]
    wdt_row = jnp.exp(tot_row - g_row) * dt_row

    for grp in range(SSD_GROUPS):
        nsl = slice(grp * SSD_STATE, (grp + 1) * SSD_STATE)
        b_g, c_g = b_ref[rs, nsl], c_ref[rs, nsl]
        cb = lax.dot_general(c_g, b_g, _NT, preferred_element_type=F32)
        bt_g = b_g.astype(F32).T
        for k in range(grp * SSD_HPG // 2, (grp + 1) * SSD_HPG // 2):
            ms, es, bws = [], [], []
            for h in (2 * k, 2 * k + 1):
                j = direction * SSD_HEADS + h
                g_bc = jnp.broadcast_to(g_col[:, j:j + 1], (t_len, t_len))
                decay = jnp.where(mask, jnp.exp(g_bc - g_row[h:h + 1, :]), 0.0)
                ms.append((decay * cb * dt_row[h:h + 1, :]).astype(BF16))
                es.append(jnp.exp(g_bc))
                bws.append((bt_g * wdt_row[h:h + 1, :]).astype(BF16))
            csl = slice(k * LANE, (k + 1) * LANE)
            xp = x_ref[rs, csl]
            xa = jnp.where(first, xp, jnp.zeros_like(xp))
            xb = jnp.where(first, jnp.zeros_like(xp), xp)
            e_pair = jnp.where(first, es[0], es[1])
            s_old = s_ref[k]
            y = (jnp.dot(ms[0], xa, preferred_element_type=F32) + jnp.dot(ms[1], xb, preferred_element_type=F32)
                 + jnp.dot(c_g, s_old.astype(BF16), preferred_element_type=F32) * e_pair)
            s_ref[k] = (s_old * e_pair[edge] + jnp.dot(bws[0], xa, preferred_element_type=F32)
                        + jnp.dot(bws[1], xb, preferred_element_type=F32))
            yacc_ref[rs, csl] = y


def _ssd_scan(xbc, da_col, dt_row, da_row, h0, direction, final=None, chunks=2):
    length = xbc.shape[0]
    t_len = SSD_CHUNK
    n_chunks = length // t_len
    chunks = min(chunks, n_chunks)
    n_steps = n_chunks // chunks
    rows = chunks * t_len
    reverse = direction == 1
    pos = (lambda c: n_steps - 1 - c) if reverse else (lambda c: c)
    di, gn = SSD_D_INNER, SSD_GN
    state = jax.ShapeDtypeStruct((SSD_PAIRS, SSD_STATE, LANE), F32)
    state_spec = pl.BlockSpec((SSD_PAIRS, SSD_STATE, LANE), lambda c: (0, 0, 0))
    in_specs = [pl.BlockSpec((rows, di), lambda c: (pos(c), 0)),
                pl.BlockSpec((rows, gn), lambda c: (pos(c), di // gn)),
                pl.BlockSpec((rows, gn), lambda c: (pos(c), di // gn + 1)),
                pl.BlockSpec((rows, LANE), lambda c: (pos(c), 0)),
                pl.BlockSpec((2 * SSD_HEADS, rows), lambda c: (0, pos(c))),
                pl.BlockSpec((2 * SSD_HEADS, rows), lambda c: (0, pos(c))),
                state_spec]
    args = [xbc, xbc, xbc, da_col, dt_row, da_row, h0]
    scratch = [pltpu.VMEM((SSD_PAIRS, SSD_STATE, LANE), F32)]
    y_spec = pl.BlockSpec((rows, di), lambda c: (pos(c), 0))
    row_spec = pl.BlockSpec((1, di), lambda c: (0, 0))
    if final is not None:
        y_other, p, d_skip_row, norm_w_row = final
        in_specs += [y_spec, y_spec, row_spec, row_spec]
        args += [y_other, p, d_skip_row, norm_w_row]
        scratch.append(pltpu.VMEM((rows, di), F32))
    y_dtype = BF16 if final is not None else F32
    kern = functools.partial(_ssd_scan_kernel, direction=direction, reverse=reverse, finalize=final is not None,
                             chunks=chunks)
    return pl.pallas_call(
        kern,
        grid=(n_steps,),
        in_specs=in_specs,
        out_specs=[y_spec, state_spec],
        out_shape=[jax.ShapeDtypeStruct((length, di), y_dtype), state],
        scratch_shapes=scratch,
        compiler_params=_params("arbitrary"),
        name="ssd_scan",
    )(*args)


def _ssd_inputs(p, dt_raw, conv_w, conv_b, dt_bias, a_log):
    xbc = _ssd_conv(p, P_XBC // SSD_CONV_BLK, conv_w, conv_b)
    dt, da = _ssd_dt(dt_raw, dt_bias, a_log)
    nh = 2 * SSD_HEADS
    return xbc, da, dt[:, :nh].T, da[:, :nh].T


def _ssd_pallas(p_l, dt_raw_l, p_c, dt_raw_c, conv_w, conv_b, dt_bias, a_log, d_skip, norm_w, ctx_out):
    xbc_l, dac_l, dtr_l, dar_l = _ssd_inputs(p_l, dt_raw_l, conv_w, conv_b, dt_bias, a_log)
    xbc_c, dac_c, dtr_c, dar_c = _ssd_inputs(p_c, dt_raw_c, conv_w, conv_b, dt_bias, a_log)
    zero = jnp.zeros((SSD_PAIRS, SSD_STATE, LANE), F32)
    fin = (jnp.repeat(d_skip, SSD_HEAD_DIM).reshape(1, -1), norm_w.reshape(1, -1))
    yc0, h_fwd = _ssd_scan(xbc_c, dac_c, dtr_c, dar_c, zero, 0)
    if ctx_out:
        out_c, h_bwd = _ssd_scan(xbc_c, dac_c, dtr_c, dar_c, zero, 1, final=(yc0, p_c) + fin)
    else:
        out_c = None
        _, h_bwd = _ssd_scan(xbc_c, dac_c, dtr_c, dar_c, zero, 1)
    yl0, _ = _ssd_scan(xbc_l, dac_l, dtr_l, dar_l, h_fwd, 0)
    out_l, _ = _ssd_scan(xbc_l, dac_l, dtr_l, dar_l, h_bwd, 1, final=(yl0, p_l) + fin)
    return out_l, out_c


P_Z = 0
P_Q = P_Z + SSD_D_INNER
P_POOL = P_Q + NA_WIDTH
P_XBC = P_POOL + POOL_WIDTH
P_K = P_XBC + SSD_XBC
P_V = P_K + NA_WIDTH
P_COLS = P_V + NA_WIDTH


def _layer(layer, x, ctx, mod_l, mod_c, norm1_w, w_main_all, w_dt_all, ssd_conv_w, ssd_conv_b, ssd_dt_bias,
           ssd_a_log, ssd_d, ssd_norm_w, na_q_norm_w, na_k_norm_w, na_rpb, pool_w, pool_scale, w_out_all, norm2_w,
           w_up_all, mlp_conv_w, mlp_conv_b, w_down_all, rope_l, rope_c, ctx_out):
    sh1_l, sc1_l, g1_l, sh2_l, sc2_l, g2_l = jnp.split(mod_l, N_MOD)
    sh1_c, sc1_c, g1_c, sh2_c, sc2_c, g2_c = jnp.split(mod_c, N_MOD)
    conv_tab = jnp.concatenate([mlp_conv_w, mlp_conv_b[None], jnp.zeros((4, 2 * D_FF), F32)], axis=0)

    h_l = _norm_mod(x, norm1_w, sc1_l, sh1_l)
    h_c = _norm_mod(ctx, norm1_w, sc1_c, sh1_c)
    p_l = _matmul(h_l, w_main_all, layer, BF16)
    dt_l = _matmul(h_l, w_dt_all, layer, F32)
    p_c = _matmul(h_c, w_main_all, layer, BF16)
    dt_c = _matmul(h_c, w_dt_all, layer, F32)

    y_ssd_l, y_ssd_c = _ssd_pallas(p_l, dt_l, p_c, dt_c, ssd_conv_w, ssd_conv_b, ssd_dt_bias, ssd_a_log, ssd_d,
                                   ssd_norm_w, ctx_out)
    q_blk, k_blk, v_blk = P_Q // NA_WIDTH, P_K // NA_WIDTH, P_V // NA_HEAD_DIM
    q_r, k_r = _qk_prep(p_l, q_blk, k_blk, na_q_norm_w, na_k_norm_w, rope_l)
    qc_r, kc_r = _qk_prep(p_c, q_blk, k_blk, na_q_norm_w, na_k_norm_w, rope_c)
    y_na_l = _na_latent(q_r, k_r, p_l, v_blk, kc_r, p_c, v_blk, _na_bias_table(na_rpb))
    y_pool_l = _pool(p_l, P_POOL // POOL_WIDTH, pool_w, pool_scale)

    x_mid = _outproj([y_ssd_l, y_na_l, y_pool_l], w_out_all, layer, x, g1_l)
    h2_l = _norm_mod(x_mid, norm2_w, sc2_l, sh2_l)
    x_new = _conv_ffn(h2_l, w_up_all, conv_tab, w_down_all, layer, x_mid, g2_l)
    ctx_new = ctx
    if ctx_out:
        y_na_c = _na_context(qc_r, kc_r, p_c, v_blk)
        y_pool_c = _pool(p_c, P_POOL // POOL_WIDTH, pool_w, pool_scale)
        c_mid = _outproj([y_ssd_c, y_na_c, y_pool_c], w_out_all, layer, ctx, g1_c)
        h2_c = _norm_mod(c_mid, norm2_w, sc2_c, sh2_c)
        ctx_new = _conv_ffn(h2_c, w_up_all, conv_tab, w_down_all, layer, c_mid, g2_c)
    return x_new, ctx_new


def kernel(x, c, ctx, c_ctx, ada_w, ada_b, norm1_w, w_in, ssd_conv_w, ssd_conv_b, ssd_dt_bias, ssd_a_log,
           ssd_d, ssd_norm_w, na_q_norm_w, na_k_norm_w, na_rpb, pool_w, pool_scale, w_out, norm2_w,
           mlp_w_up, mlp_conv_w, mlp_conv_b, mlp_w_down):
    depth = ada_w.shape[0]
    d = x.shape[-1]
    c_rows = jnp.concatenate([c.reshape(1, d), c_ctx.reshape(1, d), jnp.zeros((MOD_ROWS - 2, d), F32)], axis=0)
    mods = _modulation(c_rows, ada_w, ada_b)
    xs, cs = x[0], ctx[0]
    rope_l = _rope_tables(xs.shape[0])
    ctx_tab = (cs.shape[0], NA_HEAD_DIM)
    rope_c = (jnp.ones(ctx_tab, F32), jnp.zeros(ctx_tab, F32), jnp.zeros(ctx_tab, F32))
    w_main_all = jnp.concatenate([w_in[:, :, :OFF_DT], w_in[:, :, OFF_K:]], axis=2).astype(BF16)
    w_dt_all = jnp.pad(w_in[:, :, OFF_DT:OFF_K], ((0, 0), (0, 0), (0, LANE - 2 * SSD_HEADS))).astype(BF16)
    w_out_all = w_out.astype(BF16)
    w_up_all = mlp_w_up.astype(BF16)
    for i in range(depth):
        xs, cs = _layer(i, xs, cs, mods[i, 0], mods[i, 1], norm1_w[i], w_main_all, w_dt_all, ssd_conv_w[i],
                        ssd_conv_b[i], ssd_dt_bias[i], ssd_a_log[i], ssd_d[i], ssd_norm_w[i], na_q_norm_w[i],
                        na_k_norm_w[i], na_rpb[i], pool_w[i], pool_scale[i], w_out_all, norm2_w[i], w_up_all,
                        mlp_conv_w[i], mlp_conv_b[i], mlp_w_down, rope_l, rope_c, ctx_out=(i < depth - 1))
    return xs[None]
```

```python
import functools
import math

import jax
import jax.numpy as jnp
from jax import lax
from jax.experimental import pallas as pl
from jax.experimental.pallas import tpu as pltpu

F32 = jnp.float32
BF16 = jnp.bfloat16

D_MODEL = 4096
DEPTH = 2
GRID_W = 64
D_MIX = D_MODEL
SSD_D_INNER = D_MIX // 2
SSD_HEAD_DIM = 64
SSD_HEADS = SSD_D_INNER // SSD_HEAD_DIM
SSD_GROUPS = 4
SSD_HPG = SSD_HEADS // SSD_GROUPS
SSD_STATE = 128
SSD_GN = SSD_GROUPS * SSD_STATE
SSD_XBC = SSD_D_INNER + 2 * SSD_GN
SSD_CONV = 5
SSD_CHUNK = 128
NA_WIDTH = D_MIX // 4
NA_HEAD_DIM = 128
NA_HEADS = NA_WIDTH // NA_HEAD_DIM
NA_WIN_R = 8
NA_WIN_C = 16
ROPE_BASE = 10000.0
POOL_WIDTH = D_MIX - SSD_D_INNER - NA_WIDTH
POOL_WINDOWS = (2, 4, 8, 16)
POOL_GROUP = POOL_WIDTH // len(POOL_WINDOWS)
D_FF = 11008
N_MOD = 6
RMS_EPS = 1e-6
OFF_Z = 0
OFF_Q = OFF_Z + SSD_D_INNER
OFF_POOL = OFF_Q + NA_WIDTH
OFF_XBC = OFF_POOL + POOL_WIDTH
OFF_DT = OFF_XBC + SSD_XBC
OFF_K = OFF_DT + 2 * SSD_HEADS
OFF_V = OFF_K + NA_WIDTH
IN_COLS = OFF_V + NA_WIDTH

VMEM_LIMIT_BYTES = 58 * 1024 * 1024
LANE = 128
BF16_SUBLANE = 16
MOD_ROWS = 8
STREAM_PARTS = 4


def _params(*sem):
    return pltpu.CompilerParams(dimension_semantics=sem, vmem_limit_bytes=VMEM_LIMIT_BYTES)


def _mod_kernel(c_ref, *refs):
    w_refs, (b_ref, o_ref) = refs[:STREAM_PARTS], refs[STREAM_PARTS:]
    c = c_ref[...]
    s = (c * jax.nn.sigmoid(c)).astype(BF16)
    kp = s.shape[1] // STREAM_PARTS
    acc = b_ref[0]
    for part, w_ref in enumerate(w_refs):
        acc = acc + jnp.dot(s[:, part * kp:(part + 1) * kp], w_ref[0].astype(BF16), preferred_element_type=F32)
    o_ref[0] = acc


def _modulation(c_rows, ada_w, ada_b, tn=1024):
    depth, d, n = ada_w.shape
    kp = d // STREAM_PARTS
    w_specs = [pl.BlockSpec((1, kp, tn), functools.partial(lambda l, j, part: (l, part, j), part=part))
               for part in range(STREAM_PARTS)]
    return pl.pallas_call(
        _mod_kernel,
        grid=(depth, n // tn),
        in_specs=[pl.BlockSpec((MOD_ROWS, d), lambda l, j: (0, 0)), *w_specs,
                  pl.BlockSpec((1, 1, tn), lambda l, j: (l, 0, j))],
        out_specs=pl.BlockSpec((1, MOD_ROWS, tn), lambda l, j: (l, 0, j)),
        out_shape=jax.ShapeDtypeStruct((depth, MOD_ROWS, n), F32),
        compiler_params=_params("arbitrary", "arbitrary"),
        name="modulation",
    )(c_rows, *([ada_w] * STREAM_PARTS), ada_b.reshape(depth, 1, n))


def _norm_kernel(*refs):
    x_refs, (w_ref, sc_ref, sh_ref, o_ref) = refs[:STREAM_PARTS], refs[STREAM_PARTS:]
    rows = x_refs[0].shape[0]
    for part, x_ref in enumerate(x_refs):
        x = x_ref[...]
        y = x * lax.rsqrt(jnp.mean(x * x, axis=-1, keepdims=True) + RMS_EPS) * w_ref[...]
        o_ref[part * rows:(part + 1) * rows, :] = (y * (1.0 + sc_ref[...]) + sh_ref[...]).astype(o_ref.dtype)


def _norm_mod(x, w, scale, shift, tm=512):
    m, d = x.shape
    tm = min(tm, m)
    rows = tm // STREAM_PARTS
    row = pl.BlockSpec((1, d), lambda i: (0, 0))
    x_specs = [pl.BlockSpec((rows, d), functools.partial(lambda i, part: (STREAM_PARTS * i + part, 0), part=part))
               for part in range(STREAM_PARTS)]
    return pl.pallas_call(
        _norm_kernel,
        grid=(m // tm,),
        in_specs=[*x_specs, row, row, row],
        out_specs=pl.BlockSpec((tm, d), lambda i: (i, 0)),
        out_shape=jax.ShapeDtypeStruct((m, d), BF16),
        compiler_params=_params("arbitrary"),
        name="norm_mod",
    )(*([x] * STREAM_PARTS), w.reshape(1, d), scale.reshape(1, d), shift.reshape(1, d))


def _mm_kernel(a_ref, b_ref, o_ref):
    o_ref[...] = jnp.dot(a_ref[...], b_ref[0], preferred_element_type=F32).astype(o_ref.dtype)


def _matmul(a, b_all, layer, out_dtype, tm=1024, tn=1024):
    m, k = a.shape
    n = b_all.shape[2]
    tm, tn = min(tm, m), min(tn, n)
    return pl.pallas_call(
        _mm_kernel,
        grid=(m // tm, n // tn),
        in_specs=[pl.BlockSpec((tm, k), lambda i, j: (i, 0)),
                  pl.BlockSpec((1, k, tn), lambda i, j: (layer, 0, j))],
        out_specs=pl.BlockSpec((tm, tn), lambda i, j: (i, j)),
        out_shape=jax.ShapeDtypeStruct((m, n), out_dtype),
        compiler_params=_params("arbitrary", "arbitrary"),
        name="matmul",
    )(a, b_all)


def _outproj_kernel(*refs, widths):
    a_refs, (w_ref, x_ref, g_ref, o_ref) = refs[:len(widths)], refs[len(widths):]
    acc, off = None, 0
    for a_ref, k in zip(a_refs, widths):
        part = jnp.dot(a_ref[...], w_ref[0, off:off + k, :], preferred_element_type=F32)
        acc = part if acc is None else acc + part
        off += k
    o_ref[...] = x_ref[...] + g_ref[...] * acc


def _outproj(parts, w_all, layer, x, gate, tm=1024, tn=1024):
    m = x.shape[0]
    _, k, n = w_all.shape
    tm = min(tm, m)
    widths = tuple(a.shape[1] for a in parts)
    assert sum(widths) == k
    return pl.pallas_call(
        functools.partial(_outproj_kernel, widths=widths),
        grid=(m // tm, n // tn),
        in_specs=[pl.BlockSpec((tm, kw), lambda i, j: (i, 0)) for kw in widths] + [
            pl.BlockSpec((1, k, tn), lambda i, j: (layer, 0, j)),
            pl.BlockSpec((tm, tn), lambda i, j: (i, j)),
            pl.BlockSpec((1, tn), lambda i, j: (0, j))],
        out_specs=pl.BlockSpec((tm, tn), lambda i, j: (i, j)),
        out_shape=jax.ShapeDtypeStruct((m, n), F32),
        compiler_params=_params("arbitrary", "arbitrary"),
        name="outproj",
    )(*parts, w_all, x, gate.reshape(1, n))


FFN_HALO = BF16_SUBLANE


def _ffn_kernel(h_ref, hp_ref, hn_ref, wg_ref, wv_ref, tab_ref, wd_ref, x_ref, g_ref,
                o_ref, ext_ref, ua_ref, ub_ref, *, tm, n_mtiles, nf, tn_d):
    i = pl.program_id(0)
    f = pl.program_id(1)
    d = o_ref.shape[1]
    u_refs = (ua_ref, ub_ref)

    @pl.when(f == 0)
    def _():
        ext_ref[0:tm, :] = h_ref[...]
        nxt = hn_ref[...]
        prv = hp_ref[...]
        ext_ref[tm:tm + FFN_HALO, :] = jnp.where(i == n_mtiles - 1, jnp.zeros_like(nxt), nxt)
        ext_ref[tm + FFN_HALO:tm + 2 * FFN_HALO, :] = jnp.where(i == 0, jnp.zeros_like(prv), prv)
        o_ref[...] = jnp.zeros_like(o_ref)
        ub_ref[...] = jnp.zeros_like(ub_ref)

    def conv(u_ref, br, c):
        um = u_ref[br, 0:tm]
        row = lax.broadcasted_iota(jnp.int32, um.shape, 0)
        before = tm + 2 * FFN_HALO - 1
        up = jnp.where(row == 0, u_ref[br, before:before + 1], pltpu.roll(um, 1, 0))
        un = jnp.where(row == tm - 1, u_ref[br, tm:tm + 1], pltpu.roll(um, tm - 1, 0))
        return up * c[0:1] + um * c[1:2] + un * c[2:3] + c[3:4]

    def step(produce, consume):
        h = ext_ref[...]
        p_ref, c_ref = u_refs[produce], u_refs[consume]
        p_ref[0] = jnp.dot(h, wg_ref[0], preferred_element_type=F32)
        p_ref[1] = jnp.dot(h, wv_ref[0], preferred_element_type=F32)
        tile = jnp.maximum(f - 1, 0)
        gate = conv(c_ref, 0, tab_ref[tile])
        val = conv(c_ref, 1, tab_ref[tile + nf])
        act = gate * jax.nn.sigmoid(gate) * val
        act = jnp.where(f > 0, act, jnp.zeros_like(act)).astype(BF16)
        for n in range(d // tn_d):
            sl = slice(n * tn_d, (n + 1) * tn_d)
            o_ref[:, sl] += jnp.dot(act, wd_ref[0, :, sl].astype(BF16), preferred_element_type=F32)

    @pl.when(f % 2 == 0)
    def _():
        step(0, 1)

    @pl.when(f % 2 == 1)
    def _():
        step(1, 0)

    @pl.when(f == nf)
    def _():
        o_ref[...] = x_ref[...] + g_ref[...] * o_ref[...]


def _conv_ffn(h, w_up_all, conv_tab, w_down_all, layer, x, gate, tm=512, tf=256, tn_d=1024):
    m, d = h.shape
    ff = w_down_all.shape[1]
    tm = min(tm, m)
    tn_d = min(tn_d, d)
    n_mtiles, nf = m // tm, ff // tf
    hb = tm // FFN_HALO
    last_hb = m // FFN_HALO - 1
    kern = functools.partial(_ffn_kernel, tm=tm, n_mtiles=n_mtiles, nf=nf, tn_d=tn_d)
    tab = conv_tab.reshape(8, 2 * nf, tf).transpose(1, 0, 2)

    def prod(f):
        return jnp.minimum(f, nf - 1)

    def cons(f):
        return jnp.maximum(f - 1, 0)

    return pl.pallas_call(
        kern,
        grid=(n_mtiles, nf + 1),
        in_specs=[
            pl.BlockSpec((tm, d), lambda i, f: (i, 0), pipeline_mode=pl.Buffered(1)),
            pl.BlockSpec((FFN_HALO, d), lambda i, f: (jnp.maximum(i * hb - 1, 0), 0)),
            pl.BlockSpec((FFN_HALO, d), lambda i, f: (jnp.minimum((i + 1) * hb, last_hb), 0)),
            pl.BlockSpec((1, d, tf), lambda i, f: (layer, 0, prod(f))),
            pl.BlockSpec((1, d, tf), lambda i, f: (layer, 0, prod(f) + nf)),
            pl.BlockSpec((2 * nf, 8, tf), lambda i, f: (0, 0, 0)),
            pl.BlockSpec((1, tf, d), lambda i, f: (layer, cons(f), 0)),
            pl.BlockSpec((tm, d), lambda i, f: (i, 0), pipeline_mode=pl.Buffered(1)),
            pl.BlockSpec((1, d), lambda i, f: (0, 0)),
        ],
        out_specs=pl.BlockSpec((tm, d), lambda i, f: (i, 0)),
        out_shape=jax.ShapeDtypeStruct((m, d), F32),
        scratch_shapes=[pltpu.VMEM((tm + 2 * FFN_HALO, d), BF16),
                        pltpu.VMEM((2, tm + 2 * FFN_HALO, tf), F32),
                        pltpu.VMEM((2, tm + 2 * FFN_HALO, tf), F32)],
        compiler_params=_params("arbitrary", "arbitrary"),
        name="conv_ffn",
    )(h, h, h, w_up_all, w_up_all, tab, w_down_all, x, gate.reshape(1, d))


def _rope_tables(length):
    t = jnp.arange(length)
    half = NA_HEAD_DIM // 2
    inv = ROPE_BASE ** (-jnp.arange(0, half, 2, dtype=F32) / half)
    ar = (t // GRID_W).astype(F32)[:, None] * inv
    ac = (t % GRID_W).astype(F32)[:, None] * inv
    ang = jnp.concatenate([ar, ar, ac, ac], axis=-1)
    cos, sin = jnp.cos(ang), jnp.sin(ang)
    first = (jnp.arange(NA_HEAD_DIM) % half) < half // 2
    return cos, jnp.where(first, -sin, 0.0), jnp.where(first, 0.0, sin)


def _qk_prep_kernel(q_ref, k_ref, qw_ref, kw_ref, cos_ref, sa_ref, sb_ref, qo_ref, ko_ref):
    cos, sa, sb = cos_ref[...], sa_ref[...], sb_ref[...]
    quarter = NA_HEAD_DIM // 4
    for h in range(NA_HEADS):
        sl = slice(h * NA_HEAD_DIM, (h + 1) * NA_HEAD_DIM)
        for src, w_ref, dst in ((q_ref, qw_ref, qo_ref), (k_ref, kw_ref, ko_ref)):
            x = src[:, sl].astype(F32)
            y = x * lax.rsqrt(jnp.mean(x * x, axis=-1, keepdims=True) + RMS_EPS) * w_ref[...]
            y = (y * cos + pltpu.roll(y, NA_HEAD_DIM - quarter, 1) * sa + pltpu.roll(y, quarter, 1) * sb)
            dst[:, sl] = y.astype(dst.dtype)


def _qk_prep(p, q_blk, k_blk, qw, kw, tables, tm=512):
    m = p.shape[0]
    tm = min(tm, m)
    tab = pl.BlockSpec((tm, NA_HEAD_DIM), lambda i: (i, 0))
    vec = pl.BlockSpec((1, NA_HEAD_DIM), lambda i: (0, 0))
    out = jax.ShapeDtypeStruct((m, NA_WIDTH), BF16)
    return pl.pallas_call(
        _qk_prep_kernel,
        grid=(m // tm,),
        in_specs=[pl.BlockSpec((tm, NA_WIDTH), lambda i: (i, q_blk)),
                  pl.BlockSpec((tm, NA_WIDTH), lambda i: (i, k_blk)), vec, vec, tab, tab, tab],
        out_specs=[pl.BlockSpec((tm, NA_WIDTH), lambda i: (i, 0))] * 2,
        out_shape=[out, out],
        compiler_params=_params("arbitrary"),
        name="qk_prep",
    )(p, p, qw.reshape(1, -1), kw.reshape(1, -1), *tables)


NA_DR = 2 * NA_WIN_R
NA_BIAS_CHUNK = 1024


def _na_bias_kernel(r0_ref, r1_ref, o_ref):
    def pieces(r):
        out = []
        for _ in range(3):
            piece = r.astype(BF16)
            out.append(piece)
            r = r - piece.astype(F32)
        return out

    p0, p1 = pieces(r0_ref[0]), pieces(r1_ref[0])
    for c in range(GRID_W * LANE // NA_BIAS_CHUNK):
        shape = (LANE, NA_BIAS_CHUNK)
        pos = c * NA_BIAS_CHUNK + lax.broadcasted_iota(jnp.int32, shape, 1)
        j = lax.broadcasted_iota(jnp.int32, shape, 0)
        qc = pos // LANE
        lane = pos % LANE
        kc = lane % GRID_W
        idx = jnp.clip(kc - qc + (NA_WIN_C - 1), 0, 2 * NA_WIN_C - 2)
        hit = idx == j
        second = lane >= GRID_W
        oh0 = jnp.where(hit, jnp.where(second, 0.0, 1.0), 0.0).astype(BF16)
        oh1 = jnp.where(hit, jnp.where(second, 1.0, 0.0), 0.0).astype(BF16)
        acc = jnp.zeros((NA_DR, NA_BIAS_CHUNK), F32)
        for a, b in zip(p0, p1):
            acc = acc + jnp.dot(a, oh0, preferred_element_type=F32) + jnp.dot(b, oh1, preferred_element_type=F32)
        c0 = jnp.clip(qc[0:1] - NA_WIN_C // 2, 0, GRID_W - NA_WIN_C)
        ok = (kc[0:1] >= c0) & (kc[0:1] < c0 + NA_WIN_C)
        o_ref[0, :, c * NA_BIAS_CHUNK:(c + 1) * NA_BIAS_CHUNK] = jnp.where(ok, acc, -jnp.inf)


def _na_bias_table(rpb):
    nh, ndr, ndc = rpb.shape
    r0 = jnp.pad(rpb, ((0, 0), (0, NA_DR - ndr), (0, LANE - ndc)))
    r1 = jnp.pad(rpb[:, 1:], ((0, 0), (0, NA_DR - ndr + 1), (0, LANE - ndc)))
    spec = pl.BlockSpec((1, NA_DR, LANE), lambda h: (h, 0, 0))
    out = pl.pallas_call(
        _na_bias_kernel,
        grid=(nh,),
        in_specs=[spec, spec],
        out_specs=pl.BlockSpec((1, NA_DR, GRID_W * LANE), lambda h: (h, 0, 0)),
        out_shape=jax.ShapeDtypeStruct((nh, NA_DR, GRID_W * LANE), F32),
        compiler_params=_params("arbitrary"),
        name="na_bias",
    )(r0, r1)
    return out.reshape(nh, NA_DR, GRID_W, LANE)


def _softmax_parts(scores):
    m = functools.reduce(jnp.maximum, [jnp.max(s, axis=1, keepdims=True) for s in scores])
    ps = [jnp.exp(s - m) for s in scores]
    den = functools.reduce(jnp.add, [jnp.sum(p, axis=1, keepdims=True) for p in ps])
    return [p.astype(BF16) for p in ps], den


def _pv(parts, values):
    ps, den = parts
    o = functools.reduce(jnp.add, [jnp.dot(p, v, preferred_element_type=F32) for p, v in zip(ps, values)])
    return o / den


def _softmax_pv(scores, values):
    return _pv(_softmax_parts(scores), values)


_NT = (((1,), (1,)), ((), ()))


def _na_kernel(q_ref, k_ref, v_ref, kc_ref, vc_ref, tz_ref, o_ref, *, rows_per_step, n_rows):
    t = pl.program_id(1)
    scale = NA_HEAD_DIM ** -0.5
    band = NA_WIN_R * GRID_W
    kc, vc = kc_ref[...], vc_ref[...]
    starts, scores = [], []
    for i in range(rows_per_step):
        qr = t * rows_per_step + i
        bs = jnp.clip(qr - NA_WIN_R // 2, 0, n_rows - NA_WIN_R)
        off = bs - qr + (NA_WIN_R - 1)
        start = pl.multiple_of(bs * GRID_W, GRID_W)
        q = q_ref[i * GRID_W:(i + 1) * GRID_W, :]
        kb = k_ref[pl.ds(start, band), :]
        bias = jnp.concatenate([tz_ref[0, off + 2 * w] for w in range(NA_WIN_R // 2)], axis=1)
        s_loc = lax.dot_general(q, kb, _NT, preferred_element_type=F32) * scale + bias
        s_ctx = lax.dot_general(q, kc, _NT, preferred_element_type=F32) * scale
        starts.append(start)
        scores.append((s_loc, s_ctx))
    probs = [_softmax_parts(list(s)) for s in scores]
    for i in range(rows_per_step):
        vb = v_ref[pl.ds(starts[i], band), :]
        o = _pv(probs[i], [vb, vc])
        o_ref[i * GRID_W:(i + 1) * GRID_W, :] = o.astype(o_ref.dtype)


def _na_latent(q_r, k_r, p_l, v_blk, kc_r, p_c, vc_blk, tz, rows_per_step=8):
    length = q_r.shape[0]
    lc = kc_r.shape[0]
    n_rows = length // GRID_W
    assert n_rows >= NA_WIN_R and n_rows % rows_per_step == 0
    hd = NA_HEAD_DIM
    kern = functools.partial(_na_kernel, rows_per_step=rows_per_step, n_rows=n_rows)
    return pl.pallas_call(
        kern,
        grid=(NA_HEADS, n_rows // rows_per_step),
        in_specs=[pl.BlockSpec((rows_per_step * GRID_W, hd), lambda h, t: (t, h)),
                  pl.BlockSpec((length, hd), lambda h, t: (0, h)),
                  pl.BlockSpec((length, hd), lambda h, t: (0, v_blk + h)),
                  pl.BlockSpec((lc, hd), lambda h, t: (0, h)),
                  pl.BlockSpec((lc, hd), lambda h, t: (0, vc_blk + h)),
                  pl.BlockSpec((1, NA_DR, GRID_W, LANE), lambda h, t: (h, 0, 0, 0))],
        out_specs=pl.BlockSpec((rows_per_step * GRID_W, hd), lambda h, t: (t, h)),
        out_shape=jax.ShapeDtypeStruct((length, NA_WIDTH), BF16),
        compiler_params=_params("arbitrary", "arbitrary"),
        name="na_latent",
    )(q_r, k_r, p_l, kc_r, p_c, tz)


def _ctx_attn_kernel(q_ref, k_ref, v_ref, o_ref):
    s = lax.dot_general(q_ref[...], k_ref[...], _NT, preferred_element_type=F32) * (NA_HEAD_DIM ** -0.5)
    o_ref[...] = _softmax_pv([s], [v_ref[...]]).astype(o_ref.dtype)


def _na_context(qc_r, kc_r, p_c, vc_blk):
    lc = qc_r.shape[0]
    hd = NA_HEAD_DIM
    spec = pl.BlockSpec((lc, hd), lambda h: (0, h))
    return pl.pallas_call(
        _ctx_attn_kernel,
        grid=(NA_HEADS,),
        in_specs=[spec, spec, pl.BlockSpec((lc, hd), lambda h: (0, vc_blk + h))],
        out_specs=spec,
        out_shape=jax.ShapeDtypeStruct((lc, NA_WIDTH), BF16),
        compiler_params=_params("arbitrary"),
        name="na_context",
    )(qc_r, kc_r, p_c)


POOL_HALO = BF16_SUBLANE


def _pool_kernel(u_ref, up_ref, un_ref, w_ref, sc_ref, o_ref, ext_ref, *, tm, n_tiles, length):
    i = pl.program_id(0)
    prv, nxt = up_ref[...], un_ref[...]
    ext_ref[0:POOL_HALO, :] = jnp.where(i == 0, jnp.zeros_like(prv), prv).astype(F32)
    ext_ref[POOL_HALO:POOL_HALO + tm, :] = u_ref[...].astype(F32)
    ext_ref[POOL_HALO + tm:POOL_HALO + tm + POOL_HALO, :] = jnp.where(i == n_tiles - 1, jnp.zeros_like(nxt),
                                                                     nxt).astype(F32)
    t = i * tm + lax.broadcasted_iota(jnp.int32, (tm, 1), 0)
    for g, w in enumerate(POOL_WINDOWS):
        cs = slice(g * POOL_GROUP, (g + 1) * POOL_GROUP)
        acc = ext_ref[pl.ds(POOL_HALO - w // 2, tm), cs]
        for k in range(1, w):
            acc = acc + ext_ref[pl.ds(POOL_HALO - w // 2 + k, tm), cs]
        cnt = (jnp.minimum(t + w // 2, length) - jnp.maximum(t - w // 2, 0)).astype(F32)
        pooled = acc / cnt - ext_ref[pl.ds(POOL_HALO, tm), cs]
        y = jnp.dot(pooled.astype(BF16), w_ref[g], preferred_element_type=F32) * sc_ref[:, cs]
        o_ref[:, cs] = y.astype(o_ref.dtype)


def _pool(p, u_blk, pool_w, pool_scale, tm=512):
    length = p.shape[0]
    tm = min(tm, length)
    n_tiles = length // tm
    hb = tm // POOL_HALO
    last_hb = length // POOL_HALO - 1
    kern = functools.partial(_pool_kernel, tm=tm, n_tiles=n_tiles, length=length)
    return pl.pallas_call(
        kern,
        grid=(n_tiles,),
        in_specs=[pl.BlockSpec((tm, POOL_WIDTH), lambda i: (i, u_blk)),
                  pl.BlockSpec((POOL_HALO, POOL_WIDTH), lambda i: (jnp.maximum(i * hb - 1, 0), u_blk)),
                  pl.BlockSpec((POOL_HALO, POOL_WIDTH), lambda i: (jnp.minimum((i + 1) * hb, last_hb), u_blk)),
                  pl.BlockSpec((len(POOL_WINDOWS), POOL_GROUP, POOL_GROUP), lambda i: (0, 0, 0)),
                  pl.BlockSpec((1, POOL_WIDTH), lambda i: (0, 0))],
        out_specs=pl.BlockSpec((tm, POOL_WIDTH), lambda i: (i, 0)),
        out_shape=jax.ShapeDtypeStruct((length, POOL_WIDTH), BF16),
        scratch_shapes=[pltpu.VMEM((tm + 2 * POOL_HALO, POOL_WIDTH), F32)],
        compiler_params=_params("arbitrary"),
        name="pool",
    )(p, p, p, pool_w.astype(BF16), pool_scale.reshape(1, POOL_WIDTH))


SSD_HALO = BF16_SUBLANE
SSD_CONV_BLK = 1024
SSD_PAIRS = SSD_HEADS // 2


def _ssd_conv_kernel(u_ref, up_ref, un_ref, tab_ref, o_ref, ext_ref, *, tm, n_tiles):
    i = pl.program_id(0)
    prv, nxt = up_ref[...], un_ref[...]
    ext_ref[0:SSD_HALO, :] = jnp.where(i == 0, jnp.zeros_like(prv), prv).astype(F32)
    ext_ref[SSD_HALO:SSD_HALO + tm, :] = u_ref[...].astype(F32)
    ext_ref[SSD_HALO + tm:SSD_HALO + tm + SSD_HALO, :] = jnp.where(i == n_tiles - 1, jnp.zeros_like(nxt),
                                                                   nxt).astype(F32)
    tab = tab_ref[...]
    left = (SSD_CONV - 1) // 2
    acc = tab[SSD_CONV:SSD_CONV + 1]
    for k in range(SSD_CONV):
        acc = acc + ext_ref[pl.ds(SSD_HALO - left + k, tm), :] * tab[k:k + 1]
    o_ref[...] = (acc * jax.nn.sigmoid(acc)).astype(o_ref.dtype)


def _ssd_conv(p, blk0, conv_w, conv_b, tm=512):
    length = p.shape[0]
    tm = min(tm, length)
    n_tiles = length // tm
    hb = tm // SSD_HALO
    last_hb = length // SSD_HALO - 1
    cb = SSD_CONV_BLK
    tab = jnp.concatenate([conv_w, conv_b[None], jnp.zeros((8 - SSD_CONV - 1, SSD_XBC), F32)], axis=0)
    kern = functools.partial(_ssd_conv_kernel, tm=tm, n_tiles=n_tiles)
    return pl.pallas_call(
        kern,
        grid=(n_tiles, SSD_XBC // cb),
        in_specs=[pl.BlockSpec((tm, cb), lambda i, j: (i, blk0 + j)),
                  pl.BlockSpec((SSD_HALO, cb), lambda i, j: (jnp.maximum(i * hb - 1, 0), blk0 + j)),
                  pl.BlockSpec((SSD_HALO, cb), lambda i, j: (jnp.minimum((i + 1) * hb, last_hb), blk0 + j)),
                  pl.BlockSpec((8, cb), lambda i, j: (0, j))],
        out_specs=pl.BlockSpec((tm, cb), lambda i, j: (i, j)),
        out_shape=jax.ShapeDtypeStruct((length, SSD_XBC), BF16),
        scratch_shapes=[pltpu.VMEM((tm + 2 * SSD_HALO, cb), F32)],
        compiler_params=_params("arbitrary", "arbitrary"),
        name="ssd_conv",
    )(p, p, p, tab)


def _ssd_dt_kernel(raw_ref, bias_ref, alog_ref, dt_ref, da_ref):
    v = raw_ref[...] + bias_ref[...]
    dt = jnp.maximum(v, 0.0) + jnp.log1p(jnp.exp(-jnp.abs(v)))
    dt_ref[...] = dt
    da_ref[...] = dt * -jnp.exp(alog_ref[...])


def _ssd_dt(raw, dt_bias, a_log, tm=2048):
    length = raw.shape[0]
    tm = min(tm, length)
    pad = LANE - 2 * SSD_HEADS
    spec = pl.BlockSpec((tm, LANE), lambda i: (i, 0))
    vec = pl.BlockSpec((1, LANE), lambda i: (0, 0))
    out = jax.ShapeDtypeStruct((length, LANE), F32)
    return pl.pallas_call(
        _ssd_dt_kernel,
        grid=(length // tm,),
        in_specs=[spec, vec, vec],
        out_specs=[spec, spec],
        out_shape=[out, out],
        compiler_params=_params("arbitrary"),
        name="ssd_dt",
    )(raw, jnp.pad(dt_bias.reshape(1, -1), ((0, 0), (0, pad))), jnp.pad(a_log.reshape(1, -1), ((0, 0), (0, pad))))


def _pieces(a):
    out = []
    for _ in range(3):
        piece = a.astype(BF16)
        out.append(piece)
        a = a - piece.astype(F32)
    return out


def _ssd_scan_kernel(*refs, direction, reverse, finalize, chunks):
    if finalize:
        (x_ref, b_ref, c_ref, dac_ref, dtr_ref, dar_ref, h0_ref, yp_ref, z_ref, dsk_ref, nw_ref,
         y_ref, hn_ref, s_ref, yacc_ref) = refs
    else:
        x_ref, b_ref, c_ref, dac_ref, dtr_ref, dar_ref, h0_ref, y_ref, hn_ref, s_ref = refs
        yacc_ref = y_ref
    step = pl.program_id(0)
    nsteps = pl.num_programs(0)
    t_len = SSD_CHUNK

    @pl.when(step == 0)
    def _():
        s_ref[...] = h0_ref[...]

    row = lax.broadcasted_iota(jnp.int32, (t_len, t_len), 0)
    col = lax.broadcasted_iota(jnp.int32, (t_len, t_len), 1)
    mask = (col >= row) if reverse else (col <= row)
    mask_t = (row >= col) if reverse else (row <= col)
    ones = jnp.ones((t_len, t_len), BF16)
    tri = jnp.where(mask, 1.0, 0.0).astype(BF16)
    tri_t = jnp.where(mask_t, 1.0, 0.0).astype(BF16)
    hsl = slice(direction * SSD_HEADS, (direction + 1) * SSD_HEADS)
    lane = lax.broadcasted_iota(jnp.int32, (t_len, LANE), 1)
    first = lane < SSD_HEAD_DIM
    edge = slice(0, 1) if reverse else slice(t_len - 1, t_len)

    for u in range(chunks):
        q = chunks - 1 - u if reverse else u
        rs = slice(q * t_len, (q + 1) * t_len)
        _ssd_chunk(x_ref, b_ref, c_ref, dac_ref, dtr_ref, dar_ref, s_ref, yacc_ref, rs, hsl, direction,
                   mask, tri, tri_t, ones, first, edge)

    @pl.when(step == nsteps - 1)
    def _():
        hn_ref[...] = s_ref[...]

    if finalize:
        y = yacc_ref[...] + yp_ref[...] + x_ref[...].astype(F32) * dsk_ref[...]
        zz = z_ref[...].astype(F32)
        gated = y * (zz * jax.nn.sigmoid(zz))
        out = gated * lax.rsqrt(jnp.mean(gated * gated, axis=-1, keepdims=True) + RMS_EPS) * nw_ref[...]
        y_ref[...] = out.astype(y_ref.dtype)


def _ssd_chunk(x_ref, b_ref, c_ref, dac_ref, dtr_ref, dar_ref, s_ref, yacc_ref, rs, hsl, direction,
               mask, tri, tri_t, ones, first, edge):
    t_len = SSD_CHUNK
    g_col = sum(jnp.dot(tri, piece, preferred_element_type=F32) for piece in _pieces(dac_ref[rs, :]))
    da_row_p = _pieces(dar_ref[hsl, rs])
    g_row = sum(jnp.dot(piece, tri_t, preferred_element_type=F32) for piece in da_row_p)
    tot_row = sum(jnp.dot(piece, ones, preferred_element_type=F32) for piece in da_row_p)
    dt_row = dtr_ref[hsl, rs]
    wdt_row = jnp.exp(tot_row - g_row) * dt_row

    for grp in range(SSD_GROUPS):
        nsl = slice(grp * SSD_STATE, (grp + 1) * SSD_STATE)
        b_g, c_g = b_ref[rs, nsl], c_ref[rs, nsl]
        cb = lax.dot_general(c_g, b_g, _NT, preferred_element_type=F32)
        bt_g = b_g.astype(F32).T
        for k in range(grp * SSD_HPG // 2, (grp + 1) * SSD_HPG // 2):
            ms, es, bws = [], [], []
            for h in (2 * k, 2 * k + 1):
                j = direction * SSD_HEADS + h
                g_bc = jnp.broadcast_to(g_col[:, j:j + 1], (t_len, t_len))
                decay = jnp.where(mask, jnp.exp(g_bc - g_row[h:h + 1, :]), 0.0)
                ms.append((decay * cb * dt_row[h:h + 1, :]).astype(BF16))
                es.append(jnp.exp(g_bc))
                bws.append((bt_g * wdt_row[h:h + 1, :]).astype(BF16))
            csl = slice(k * LANE, (k + 1) * LANE)
            xp = x_ref[rs, csl]
            xa = jnp.where(first, xp, jnp.zeros_like(xp))
            xb = jnp.where(first, jnp.zeros_like(xp), xp)
            e_pair = jnp.where(first, es[0], es[1])
            s_old = s_ref[k]
            y = (jnp.dot(ms[0], xa, preferred_element_type=F32) + jnp.dot(ms[1], xb, preferred_element_type=F32)
                 + jnp.dot(c_g, s_old.astype(BF16), preferred_element_type=F32) * e_pair)
            s_ref[k] = (s_old * e_pair[edge] + jnp.dot(bws[0], xa, preferred_element_type=F32)
                        + jnp.dot(bws[1], xb, preferred_element_type=F32))
            yacc_ref[rs, csl] = y


def _ssd_scan(xbc, da_col, dt_row, da_row, h0, direction, final=None, chunks=2):
    length = xbc.shape[0]
    t_len = SSD_CHUNK
    n_chunks = length // t_len
    chunks = min(chunks, n_chunks)
    n_steps = n_chunks // chunks
    rows = chunks * t_len
    reverse = direction == 1
    pos = (lambda c: n_steps - 1 - c) if reverse else (lambda c: c)
    di, gn = SSD_D_INNER, SSD_GN
    state = jax.ShapeDtypeStruct((SSD_PAIRS, SSD_STATE, LANE), F32)
    state_spec = pl.BlockSpec((SSD_PAIRS, SSD_STATE, LANE), lambda c: (0, 0, 0))
    in_specs = [pl.BlockSpec((rows, di), lambda c: (pos(c), 0)),
                pl.BlockSpec((rows, gn), lambda c: (pos(c), di // gn)),
                pl.BlockSpec((rows, gn), lambda c: (pos(c), di // gn + 1)),
                pl.BlockSpec((rows, LANE), lambda c: (pos(c), 0)),
                pl.BlockSpec((2 * SSD_HEADS, rows), lambda c: (0, pos(c))),
                pl.BlockSpec((2 * SSD_HEADS, rows), lambda c: (0, pos(c))),
                state_spec]
    args = [xbc, xbc, xbc, da_col, dt_row, da_row, h0]
    scratch = [pltpu.VMEM((SSD_PAIRS, SSD_STATE, LANE), F32)]
    y_spec = pl.BlockSpec((rows, di), lambda c: (pos(c), 0))
    row_spec = pl.BlockSpec((1, di), lambda c: (0, 0))
    if final is not None:
        y_other, p, d_skip_row, norm_w_row = final
        in_specs += [y_spec, y_spec, row_spec, row_spec]
        args += [y_other, p, d_skip_row, norm_w_row]
        scratch.append(pltpu.VMEM((rows, di), F32))
    y_dtype = BF16 if final is not None else F32
    kern = functools.partial(_ssd_scan_kernel, direction=direction, reverse=reverse, finalize=final is not None,
                             chunks=chunks)
    return pl.pallas_call(
        kern,
        grid=(n_steps,),
        in_specs=in_specs,
        out_specs=[y_spec, state_spec],
        out_shape=[jax.ShapeDtypeStruct((length, di), y_dtype), state],
        scratch_shapes=scratch,
        compiler_params=_params("arbitrary"),
        name="ssd_scan",
    )(*args)


def _ssd_inputs(p, dt_raw, conv_w, conv_b, dt_bias, a_log):
    xbc = _ssd_conv(p, P_XBC // SSD_CONV_BLK, conv_w, conv_b)
    dt, da = _ssd_dt(dt_raw, dt_bias, a_log)
    nh = 2 * SSD_HEADS
    return xbc, da, dt[:, :nh].T, da[:, :nh].T


def _ssd_pallas(p_l, dt_raw_l, p_c, dt_raw_c, conv_w, conv_b, dt_bias, a_log, d_skip, norm_w, ctx_out):
    xbc_l, dac_l, dtr_l, dar_l = _ssd_inputs(p_l, dt_raw_l, conv_w, conv_b, dt_bias, a_log)
    xbc_c, dac_c, dtr_c, dar_c = _ssd_inputs(p_c, dt_raw_c, conv_w, conv_b, dt_bias, a_log)
    zero = jnp.zeros((SSD_PAIRS, SSD_STATE, LANE), F32)
    fin = (jnp.repeat(d_skip, SSD_HEAD_DIM).reshape(1, -1), norm_w.reshape(1, -1))
    yc0, h_fwd = _ssd_scan(xbc_c, dac_c, dtr_c, dar_c, zero, 0)
    if ctx_out:
        out_c, h_bwd = _ssd_scan(xbc_c, dac_c, dtr_c, dar_c, zero, 1, final=(yc0, p_c) + fin)
    else:
        out_c = None
        _, h_bwd = _ssd_scan(xbc_c, dac_c, dtr_c, dar_c, zero, 1)
    yl0, _ = _ssd_scan(xbc_l, dac_l, dtr_l, dar_l, h_fwd, 0)
    out_l, _ = _ssd_scan(xbc_l, dac_l, dtr_l, dar_l, h_bwd, 1, final=(yl0, p_l) + fin)
    return out_l, out_c


P_Z = 0
P_Q = P_Z + SSD_D_INNER
P_POOL = P_Q + NA_WIDTH
P_XBC = P_POOL + POOL_WIDTH
P_K = P_XBC + SSD_XBC
P_V = P_K + NA_WIDTH
P_COLS = P_V + NA_WIDTH


def _layer(layer, x, ctx, mod_l, mod_c, norm1_w, w_main_all, w_dt_all, ssd_conv_w, ssd_conv_b, ssd_dt_bias,
           ssd_a_log, ssd_d, ssd_norm_w, na_q_norm_w, na_k_norm_w, na_rpb, pool_w, pool_scale, w_out_all, norm2_w,
           w_up_all, mlp_conv_w, mlp_conv_b, w_down_all, rope_l, rope_c, ctx_out):
    sh1_l, sc1_l, g1_l, sh2_l, sc2_l, g2_l = jnp.split(mod_l, N_MOD)
    sh1_c, sc1_c, g1_c, sh2_c, sc2_c, g2_c = jnp.split(mod_c, N_MOD)
    conv_tab = jnp.concatenate([mlp_conv_w, mlp_conv_b[None], jnp.zeros((4, 2 * D_FF), F32)], axis=0)

    h_l = _norm_mod(x, norm1_w, sc1_l, sh1_l)
    h_c = _norm_mod(ctx, norm1_w, sc1_c, sh1_c)
    p_l = _matmul(h_l, w_main_all, layer, BF16)
    dt_l = _matmul(h_l, w_dt_all, layer, F32)
    p_c = _matmul(h_c, w_main_all, layer, BF16)
    dt_c = _matmul(h_c, w_dt_all, layer, F32)

    y_ssd_l, y_ssd_c = _ssd_pallas(p_l, dt_l, p_c, dt_c, ssd_conv_w, ssd_conv_b, ssd_dt_bias, ssd_a_log, ssd_d,
                                   ssd_norm_w, ctx_out)
    q_blk, k_blk, v_blk = P_Q // NA_WIDTH, P_K // NA_WIDTH, P_V // NA_HEAD_DIM
    q_r, k_r = _qk_prep(p_l, q_blk, k_blk, na_q_norm_w, na_k_norm_w, rope_l)
    qc_r, kc_r = _qk_prep(p_c, q_blk, k_blk, na_q_norm_w, na_k_norm_w, rope_c)
    y_na_l = _na_latent(q_r, k_r, p_l, v_blk, kc_r, p_c, v_blk, _na_bias_table(na_rpb))
    y_pool_l = _pool(p_l, P_POOL // POOL_WIDTH, pool_w, pool_scale)

    x_mid = _outproj([y_ssd_l, y_na_l, y_pool_l], w_out_all, layer, x, g1_l)
    h2_l = _norm_mod(x_mid, norm2_w, sc2_l, sh2_l)
    x_new = _conv_ffn(h2_l, w_up_all, conv_tab, w_down_all, layer, x_mid, g2_l)
    ctx_new = ctx
    if ctx_out:
        y_na_c = _na_context(qc_r, kc_r, p_c, v_blk)
        y_pool_c = _pool(p_c, P_POOL // POOL_WIDTH, pool_w, pool_scale)
        c_mid = _outproj([y_ssd_c, y_na_c, y_pool_c], w_out_all, layer, ctx, g1_c)
        h2_c = _norm_mod(c_mid, norm2_w, sc2_c, sh2_c)
        ctx_new = _conv_ffn(h2_c, w_up_all, conv_tab, w_down_all, layer, c_mid, g2_c)
    return x_new, ctx_new


def kernel(x, c, ctx, c_ctx, ada_w, ada_b, norm1_w, w_in, ssd_conv_w, ssd_conv_b, ssd_dt_bias, ssd_a_log,
           ssd_d, ssd_norm_w, na_q_norm_w, na_k_norm_w, na_rpb, pool_w, pool_scale, w_out, norm2_w,
           mlp_w_up, mlp_conv_w, mlp_conv_b, mlp_w_down):
    depth = ada_w.shape[0]
    d = x.shape[-1]
    c_rows = jnp.concatenate([c.reshape(1, d), c_ctx.reshape(1, d), jnp.zeros((MOD_ROWS - 2, d), F32)], axis=0)
    mods = _modulation(c_rows, ada_w, ada_b)
    xs, cs = x[0], ctx[0]
    rope_l = _rope_tables(xs.shape[0])
    ctx_tab = (cs.shape[0], NA_HEAD_DIM)
    rope_c = (jnp.ones(ctx_tab, F32), jnp.zeros(ctx_tab, F32), jnp.zeros(ctx_tab, F32))
    w_main_all = jnp.concatenate([w_in[:, :, :OFF_DT], w_in[:, :, OFF_K:]], axis=2).astype(BF16)
    w_dt_all = jnp.pad(w_in[:, :, OFF_DT:OFF_K], ((0, 0), (0, 0), (0, LANE - 2 * SSD_HEADS))).astype(BF16)
    w_out_all = w_out.astype(BF16)
    w_up_all = mlp_w_up.astype(BF16)
    for i in range(depth):
        xs, cs = _layer(i, xs, cs, mods[i, 0], mods[i, 1], norm1_w[i], w_main_all, w_dt_all, ssd_conv_w[i],
                        ssd_conv_b[i], ssd_dt_bias[i], ssd_a_log[i], ssd_d[i], ssd_norm_w[i], na_q_norm_w[i],
                        na_k_norm_w[i], na_rpb[i], pool_w[i], pool_scale[i], w_out_all, norm2_w[i], w_up_all,
                        mlp_conv_w[i], mlp_conv_b[i], mlp_w_down, rope_l, rope_c, ctx_out=(i < depth - 1))
    return xs[None]
```

```python
import functools
import math

import jax
import jax.numpy as jnp
from jax import lax
from jax.experimental import pallas as pl
from jax.experimental.pallas import tpu as pltpu

F32 = jnp.float32
BF16 = jnp.bfloat16

D_MODEL = 4096
DEPTH = 2
GRID_W = 64
D_MIX = D_MODEL
SSD_D_INNER = D_MIX // 2
SSD_HEAD_DIM = 64
SSD_HEADS = SSD_D_INNER // SSD_HEAD_DIM
SSD_GROUPS = 4
SSD_HPG = SSD_HEADS // SSD_GROUPS
SSD_STATE = 128
SSD_GN = SSD_GROUPS * SSD_STATE
SSD_XBC = SSD_D_INNER + 2 * SSD_GN
SSD_CONV = 5
SSD_CHUNK = 128
NA_WIDTH = D_MIX // 4
NA_HEAD_DIM = 128
NA_HEADS = NA_WIDTH // NA_HEAD_DIM
NA_WIN_R = 8
NA_WIN_C = 16
ROPE_BASE = 10000.0
POOL_WIDTH = D_MIX - SSD_D_INNER - NA_WIDTH
POOL_WINDOWS = (2, 4, 8, 16)
POOL_GROUP = POOL_WIDTH // len(POOL_WINDOWS)
D_FF = 11008
N_MOD = 6
RMS_EPS = 1e-6
OFF_Z = 0
OFF_Q = OFF_Z + SSD_D_INNER
OFF_POOL = OFF_Q + NA_WIDTH
OFF_XBC = OFF_POOL + POOL_WIDTH
OFF_DT = OFF_XBC + SSD_XBC
OFF_K = OFF_DT + 2 * SSD_HEADS
OFF_V = OFF_K + NA_WIDTH
IN_COLS = OFF_V + NA_WIDTH

VMEM_LIMIT_BYTES = 58 * 1024 * 1024
LANE = 128
BF16_SUBLANE = 16
MOD_ROWS = 8


def _params(*sem):
    return pltpu.CompilerParams(dimension_semantics=sem, vmem_limit_bytes=VMEM_LIMIT_BYTES)


def _mod_kernel(c_ref, w_ref, b_ref, o_ref):
    c = c_ref[...]
    s = (c * jax.nn.sigmoid(c)).astype(BF16)
    o_ref[0] = jnp.dot(s, w_ref[0].astype(BF16), preferred_element_type=F32) + b_ref[0]


def _modulation(c_rows, ada_w, ada_b, tn=1024):
    depth, d, n = ada_w.shape
    return pl.pallas_call(
        _mod_kernel,
        grid=(depth, n // tn),
        in_specs=[pl.BlockSpec((MOD_ROWS, d), lambda l, j: (0, 0)),
                  pl.BlockSpec((1, d, tn), lambda l, j: (l, 0, j)),
                  pl.BlockSpec((1, 1, tn), lambda l, j: (l, 0, j))],
        out_specs=pl.BlockSpec((1, MOD_ROWS, tn), lambda l, j: (l, 0, j)),
        out_shape=jax.ShapeDtypeStruct((depth, MOD_ROWS, n), F32),
        compiler_params=_params("arbitrary", "arbitrary"),
        name="modulation",
    )(c_rows, ada_w, ada_b.reshape(depth, 1, n))


def _norm_kernel(x_ref, w_ref, sc_ref, sh_ref, o_ref):
    x = x_ref[...]
    y = x * lax.rsqrt(jnp.mean(x * x, axis=-1, keepdims=True) + RMS_EPS) * w_ref[...]
    o_ref[...] = (y * (1.0 + sc_ref[...]) + sh_ref[...]).astype(o_ref.dtype)


def _norm_mod(x, w, scale, shift, tm=512):
    m, d = x.shape
    tm = min(tm, m)
    row = pl.BlockSpec((1, d), lambda i: (0, 0))
    return pl.pallas_call(
        _norm_kernel,
        grid=(m // tm,),
        in_specs=[pl.BlockSpec((tm, d), lambda i: (i, 0)), row, row, row],
        out_specs=pl.BlockSpec((tm, d), lambda i: (i, 0)),
        out_shape=jax.ShapeDtypeStruct((m, d), BF16),
        compiler_params=_params("arbitrary"),
        name="norm_mod",
    )(x, w.reshape(1, d), scale.reshape(1, d), shift.reshape(1, d))


def _mm_kernel(a_ref, b_ref, o_ref):
    o_ref[...] = jnp.dot(a_ref[...], b_ref[0], preferred_element_type=F32).astype(o_ref.dtype)


def _matmul(a, b_all, layer, out_dtype, n_cols=None, tm=1024, tn=1024):
    m, k = a.shape
    n = b_all.shape[2] if n_cols is None else n_cols
    tm, tn = min(tm, m), min(tn, n)
    assert n % tn == 0
    return pl.pallas_call(
        _mm_kernel,
        grid=(m // tm, n // tn),
        in_specs=[pl.BlockSpec((tm, k), lambda i, j: (i, 0)),
                  pl.BlockSpec((1, k, tn), lambda i, j: (layer, 0, j))],
        out_specs=pl.BlockSpec((tm, tn), lambda i, j: (i, j)),
        out_shape=jax.ShapeDtypeStruct((m, n), out_dtype),
        compiler_params=_params("arbitrary", "arbitrary"),
        name="matmul",
    )(a, b_all)


def _outproj_kernel(*refs, widths):
    a_refs, (w_ref, x_ref, g_ref, o_ref) = refs[:len(widths)], refs[len(widths):]
    acc, off = None, 0
    for a_ref, k in zip(a_refs, widths):
        part = jnp.dot(a_ref[...], w_ref[0, off:off + k, :], preferred_element_type=F32)
        acc = part if acc is None else acc + part
        off += k
    o_ref[...] = x_ref[...] + g_ref[...] * acc


def _outproj(parts, w_all, layer, x, gate, tm=1024, tn=1024):
    m = x.shape[0]
    _, k, n = w_all.shape
    tm = min(tm, m)
    widths = tuple(a.shape[1] for a in parts)
    assert sum(widths) == k
    return pl.pallas_call(
        functools.partial(_outproj_kernel, widths=widths),
        grid=(m // tm, n // tn),
        in_specs=[pl.BlockSpec((tm, kw), lambda i, j: (i, 0)) for kw in widths] + [
            pl.BlockSpec((1, k, tn), lambda i, j: (layer, 0, j)),
            pl.BlockSpec((tm, tn), lambda i, j: (i, j)),
            pl.BlockSpec((1, tn), lambda i, j: (0, j))],
        out_specs=pl.BlockSpec((tm, tn), lambda i, j: (i, j)),
        out_shape=jax.ShapeDtypeStruct((m, n), F32),
        compiler_params=_params("arbitrary", "arbitrary"),
        name="outproj",
    )(*parts, w_all, x, gate.reshape(1, n))


FFN_HALO = BF16_SUBLANE


def _ffn_kernel(h_ref, hp_ref, hn_ref, wg_ref, wv_ref, tab_ref, wd_ref, x_ref, g_ref,
                o_ref, ext_ref, ua_ref, ub_ref, *, tm, n_mtiles, nf, tn_d):
    i = pl.program_id(0)
    f = pl.program_id(1)
    d = o_ref.shape[1]
    u_refs = (ua_ref, ub_ref)

    @pl.when(f == 0)
    def _():
        ext_ref[0:tm, :] = h_ref[...]
        half = FFN_HALO // 2
        nxt = hn_ref[...].astype(F32)[0:half]
        prv = hp_ref[...].astype(F32)[half:FFN_HALO]
        nxt = jnp.where(i == n_mtiles - 1, jnp.zeros_like(nxt), nxt)
        prv = jnp.where(i == 0, jnp.zeros_like(prv), prv)
        ext_ref[tm:tm + FFN_HALO, :] = jnp.concatenate([nxt, prv], axis=0).astype(BF16)
        o_ref[...] = jnp.zeros_like(o_ref)
        ub_ref[...] = jnp.zeros_like(ub_ref)

    def conv(u_ref, br, c):
        um = u_ref[br, 0:tm]
        row = lax.broadcasted_iota(jnp.int32, um.shape, 0)
        before = tm + FFN_HALO - 1
        up = jnp.where(row == 0, u_ref[br, before:before + 1], pltpu.roll(um, 1, 0))
        un = jnp.where(row == tm - 1, u_ref[br, tm:tm + 1], pltpu.roll(um, tm - 1, 0))
        return up * c[0:1] + um * c[1:2] + un * c[2:3] + c[3:4]

    def step(produce, consume):
        h = ext_ref[...]
        p_ref, c_ref = u_refs[produce], u_refs[consume]
        p_ref[0] = jnp.dot(h, wg_ref[0], preferred_element_type=F32)
        p_ref[1] = jnp.dot(h, wv_ref[0], preferred_element_type=F32)
        tile = jnp.maximum(f - 1, 0)
        gate = conv(c_ref, 0, tab_ref[tile])
        val = conv(c_ref, 1, tab_ref[tile + nf])
        act = gate * jax.nn.sigmoid(gate) * val
        act = jnp.where(f > 0, act, jnp.zeros_like(act)).astype(BF16)
        for n in range(d // tn_d):
            sl = slice(n * tn_d, (n + 1) * tn_d)
            o_ref[:, sl] += jnp.dot(act, wd_ref[0, :, sl].astype(BF16), preferred_element_type=F32)

    @pl.when(f % 2 == 0)
    def _():
        step(0, 1)

    @pl.when(f % 2 == 1)
    def _():
        step(1, 0)

    @pl.when(f == nf)
    def _():
        o_ref[...] = x_ref[...] + g_ref[...] * o_ref[...]


def _conv_ffn(h, w_up_all, conv_tab, w_down_all, layer, x, gate, tm=512, tf=256, tn_d=1024):
    m, d = h.shape
    ff = w_down_all.shape[1]
    tm = min(tm, m)
    tn_d = min(tn_d, d)
    n_mtiles, nf = m // tm, ff // tf
    hb = tm // FFN_HALO
    last_hb = m // FFN_HALO - 1
    kern = functools.partial(_ffn_kernel, tm=tm, n_mtiles=n_mtiles, nf=nf, tn_d=tn_d)
    tab = conv_tab.reshape(8, 2 * nf, tf).transpose(1, 0, 2)

    def prod(f):
        return jnp.minimum(f, nf - 1)

    def cons(f):
        return jnp.maximum(f - 1, 0)

    return pl.pallas_call(
        kern,
        grid=(n_mtiles, nf + 1),
        in_specs=[
            pl.BlockSpec((tm, d), lambda i, f: (i, 0), pipeline_mode=pl.Buffered(1)),
            pl.BlockSpec((FFN_HALO, d), lambda i, f: (jnp.maximum(i * hb - 1, 0), 0)),
            pl.BlockSpec((FFN_HALO, d), lambda i, f: (jnp.minimum((i + 1) * hb, last_hb), 0)),
            pl.BlockSpec((1, d, tf), lambda i, f: (layer, 0, prod(f))),
            pl.BlockSpec((1, d, tf), lambda i, f: (layer, 0, prod(f) + nf)),
            pl.BlockSpec((2 * nf, 8, tf), lambda i, f: (0, 0, 0)),
            pl.BlockSpec((1, tf, d), lambda i, f: (layer, cons(f), 0)),
            pl.BlockSpec((tm, d), lambda i, f: (i, 0), pipeline_mode=pl.Buffered(1)),
            pl.BlockSpec((1, d), lambda i, f: (0, 0)),
        ],
        out_specs=pl.BlockSpec((tm, d), lambda i, f: (i, 0)),
        out_shape=jax.ShapeDtypeStruct((m, d), F32),
        scratch_shapes=[pltpu.VMEM((tm + FFN_HALO, d), BF16),
                        pltpu.VMEM((2, tm + FFN_HALO, tf), F32),
                        pltpu.VMEM((2, tm + FFN_HALO, tf), F32)],
        compiler_params=_params("arbitrary", "arbitrary"),
        name="conv_ffn",
    )(h, h, h, w_up_all, w_up_all, tab, w_down_all, x, gate.reshape(1, d))


def _rope_tables(length):
    t = jnp.arange(length)
    half = NA_HEAD_DIM // 2
    inv = ROPE_BASE ** (-jnp.arange(0, half, 2, dtype=F32) / half)
    ar = (t // GRID_W).astype(F32)[:, None] * inv
    ac = (t % GRID_W).astype(F32)[:, None] * inv
    ang = jnp.concatenate([ar, ar, ac, ac], axis=-1)
    cos, sin = jnp.cos(ang), jnp.sin(ang)
    first = (jnp.arange(NA_HEAD_DIM) % half) < half // 2
    return cos, jnp.where(first, -sin, 0.0), jnp.where(first, 0.0, sin)


def _qk_prep_kernel(q_ref, k_ref, qw_ref, kw_ref, cos_ref, sa_ref, sb_ref, qo_ref, ko_ref):
    cos, sa, sb = cos_ref[...], sa_ref[...], sb_ref[...]
    quarter = NA_HEAD_DIM // 4
    for h in range(NA_HEADS):
        sl = slice(h * NA_HEAD_DIM, (h + 1) * NA_HEAD_DIM)
        for src, w_ref, dst in ((q_ref, qw_ref, qo_ref), (k_ref, kw_ref, ko_ref)):
            x = src[:, sl].astype(F32)
            y = x * lax.rsqrt(jnp.mean(x * x, axis=-1, keepdims=True) + RMS_EPS) * w_ref[...]
            y = (y * cos + pltpu.roll(y, NA_HEAD_DIM - quarter, 1) * sa + pltpu.roll(y, quarter, 1) * sb)
            dst[:, sl] = y.astype(dst.dtype)


def _qk_prep(p_q, q_blk, p_k, k_blk, qw, kw, tables, tm=512):
    m = p_q.shape[0]
    tm = min(tm, m)
    tab = pl.BlockSpec((tm, NA_HEAD_DIM), lambda i: (i, 0))
    vec = pl.BlockSpec((1, NA_HEAD_DIM), lambda i: (0, 0))
    out = jax.ShapeDtypeStruct((m, NA_WIDTH), BF16)
    return pl.pallas_call(
        _qk_prep_kernel,
        grid=(m // tm,),
        in_specs=[pl.BlockSpec((tm, NA_WIDTH), lambda i: (i, q_blk)),
                  pl.BlockSpec((tm, NA_WIDTH), lambda i: (i, k_blk)), vec, vec, tab, tab, tab],
        out_specs=[pl.BlockSpec((tm, NA_WIDTH), lambda i: (i, 0))] * 2,
        out_shape=[out, out],
        compiler_params=_params("arbitrary"),
        name="qk_prep",
    )(p_q, p_k, qw.reshape(1, -1), kw.reshape(1, -1), *tables)


NA_DR = 2 * NA_WIN_R
NA_BIAS_CHUNK = 1024


def _na_bias_kernel(r0_ref, r1_ref, o_ref):
    def pieces(r):
        out = []
        for _ in range(3):
            piece = r.astype(BF16)
            out.append(piece)
            r = r - piece.astype(F32)
        return out

    p0, p1 = pieces(r0_ref[0]), pieces(r1_ref[0])
    for c in range(GRID_W * LANE // NA_BIAS_CHUNK):
        shape = (LANE, NA_BIAS_CHUNK)
        pos = c * NA_BIAS_CHUNK + lax.broadcasted_iota(jnp.int32, shape, 1)
        j = lax.broadcasted_iota(jnp.int32, shape, 0)
        qc = pos // LANE
        lane = pos % LANE
        kc = lane % GRID_W
        idx = jnp.clip(kc - qc + (NA_WIN_C - 1), 0, 2 * NA_WIN_C - 2)
        hit = idx == j
        second = lane >= GRID_W
        oh0 = jnp.where(hit, jnp.where(second, 0.0, 1.0), 0.0).astype(BF16)
        oh1 = jnp.where(hit, jnp.where(second, 1.0, 0.0), 0.0).astype(BF16)
        acc = jnp.zeros((NA_DR, NA_BIAS_CHUNK), F32)
        for a, b in zip(p0, p1):
            acc = acc + jnp.dot(a, oh0, preferred_element_type=F32) + jnp.dot(b, oh1, preferred_element_type=F32)
        c0 = jnp.clip(qc[0:1] - NA_WIN_C // 2, 0, GRID_W - NA_WIN_C)
        ok = (kc[0:1] >= c0) & (kc[0:1] < c0 + NA_WIN_C)
        o_ref[0, :, c * NA_BIAS_CHUNK:(c + 1) * NA_BIAS_CHUNK] = jnp.where(ok, acc, -jnp.inf)


def _na_bias_table(rpb):
    nh, ndr, ndc = rpb.shape
    r0 = jnp.pad(rpb, ((0, 0), (0, NA_DR - ndr), (0, LANE - ndc)))
    r1 = jnp.pad(rpb[:, 1:], ((0, 0), (0, NA_DR - ndr + 1), (0, LANE - ndc)))
    spec = pl.BlockSpec((1, NA_DR, LANE), lambda h: (h, 0, 0))
    out = pl.pallas_call(
        _na_bias_kernel,
        grid=(nh,),
        in_specs=[spec, spec],
        out_specs=pl.BlockSpec((1, NA_DR, GRID_W * LANE), lambda h: (h, 0, 0)),
        out_shape=jax.ShapeDtypeStruct((nh, NA_DR, GRID_W * LANE), F32),
        compiler_params=_params("arbitrary"),
        name="na_bias",
    )(r0, r1)
    return out.reshape(nh, NA_DR, GRID_W, LANE)


def _softmax_parts(scores):
    m = functools.reduce(jnp.maximum, [jnp.max(s, axis=1, keepdims=True) for s in scores])
    ps = [jnp.exp(s - m) for s in scores]
    den = functools.reduce(jnp.add, [jnp.sum(p, axis=1, keepdims=True) for p in ps])
    return [p.astype(BF16) for p in ps], den


def _pv(parts, values):
    ps, den = parts
    o = functools.reduce(jnp.add, [jnp.dot(p, v, preferred_element_type=F32) for p, v in zip(ps, values)])
    return o / den


def _softmax_pv(scores, values):
    return _pv(_softmax_parts(scores), values)


_NT = (((1,), (1,)), ((), ()))


def _na_kernel(q_ref, k_ref, v_ref, kc_ref, vc_ref, tz_ref, o_ref, *, rows_per_step, n_rows):
    t = pl.program_id(1)
    scale = NA_HEAD_DIM ** -0.5
    band = NA_WIN_R * GRID_W
    kc, vc = kc_ref[...], vc_ref[...]
    starts, scores = [], []
    for i in range(rows_per_step):
        qr = t * rows_per_step + i
        bs = jnp.clip(qr - NA_WIN_R // 2, 0, n_rows - NA_WIN_R)
        off = bs - qr + (NA_WIN_R - 1)
        start = pl.multiple_of(bs * GRID_W, GRID_W)
        q = q_ref[i * GRID_W:(i + 1) * GRID_W, :]
        kb = k_ref[pl.ds(start, band), :]
        bias = jnp.concatenate([tz_ref[0, off + 2 * w] for w in range(NA_WIN_R // 2)], axis=1)
        s_loc = lax.dot_general(q, kb, _NT, preferred_element_type=F32) * scale + bias
        s_ctx = lax.dot_general(q, kc, _NT, preferred_element_type=F32) * scale
        starts.append(start)
        scores.append((s_loc, s_ctx))
    probs = [_softmax_parts(list(s)) for s in scores]
    for i in range(rows_per_step):
        vb = v_ref[pl.ds(starts[i], band), :]
        o = _pv(probs[i], [vb, vc])
        o_ref[i * GRID_W:(i + 1) * GRID_W, :] = o.astype(o_ref.dtype)


def _na_latent(q_r, k_r, p_l, v_blk, kc_r, p_c, vc_blk, tz, rows_per_step=16):
    length = q_r.shape[0]
    lc = kc_r.shape[0]
    n_rows = length // GRID_W
    assert n_rows >= NA_WIN_R and n_rows % rows_per_step == 0
    hd = NA_HEAD_DIM
    kern = functools.partial(_na_kernel, rows_per_step=rows_per_step, n_rows=n_rows)
    return pl.pallas_call(
        kern,
        grid=(NA_HEADS, n_rows // rows_per_step),
        in_specs=[pl.BlockSpec((rows_per_step * GRID_W, hd), lambda h, t: (t, h)),
                  pl.BlockSpec((length, hd), lambda h, t: (0, h)),
                  pl.BlockSpec((length, hd), lambda h, t: (0, v_blk + h)),
                  pl.BlockSpec((lc, hd), lambda h, t: (0, h)),
                  pl.BlockSpec((lc, hd), lambda h, t: (0, vc_blk + h)),
                  pl.BlockSpec((1, NA_DR, GRID_W, LANE), lambda h, t: (h, 0, 0, 0))],
        out_specs=pl.BlockSpec((rows_per_step * GRID_W, hd), lambda h, t: (t, h)),
        out_shape=jax.ShapeDtypeStruct((length, NA_WIDTH), BF16),
        compiler_params=_params("arbitrary", "arbitrary"),
        name="na_latent",
    )(q_r, k_r, p_l, kc_r, p_c, tz)


def _ctx_attn_kernel(q_ref, k_ref, v_ref, o_ref):
    s = lax.dot_general(q_ref[...], k_ref[...], _NT, preferred_element_type=F32) * (NA_HEAD_DIM ** -0.5)
    o_ref[...] = _softmax_pv([s], [v_ref[...]]).astype(o_ref.dtype)


def _na_context(qc_r, kc_r, p_c, vc_blk):
    lc = qc_r.shape[0]
    hd = NA_HEAD_DIM
    spec = pl.BlockSpec((lc, hd), lambda h: (0, h))
    return pl.pallas_call(
        _ctx_attn_kernel,
        grid=(NA_HEADS,),
        in_specs=[spec, spec, pl.BlockSpec((lc, hd), lambda h: (0, vc_blk + h))],
        out_specs=spec,
        out_shape=jax.ShapeDtypeStruct((lc, NA_WIDTH), BF16),
        compiler_params=_params("arbitrary"),
        name="na_context",
    )(qc_r, kc_r, p_c)


POOL_HALO = BF16_SUBLANE


def _pool_kernel(u_ref, up_ref, un_ref, w_ref, sc_ref, o_ref, ext_ref, *, tm, n_tiles, length):
    i = pl.program_id(0)
    prv, nxt = up_ref[...], un_ref[...]
    ext_ref[0:POOL_HALO, :] = jnp.where(i == 0, jnp.zeros_like(prv), prv).astype(F32)
    ext_ref[POOL_HALO:POOL_HALO + tm, :] = u_ref[...].astype(F32)
    ext_ref[POOL_HALO + tm:POOL_HALO + tm + POOL_HALO, :] = jnp.where(i == n_tiles - 1, jnp.zeros_like(nxt),
                                                                     nxt).astype(F32)
    t = i * tm + lax.broadcasted_iota(jnp.int32, (tm, 1), 0)
    for g, w in enumerate(POOL_WINDOWS):
        cs = slice(g * POOL_GROUP, (g + 1) * POOL_GROUP)
        acc = ext_ref[pl.ds(POOL_HALO - w // 2, tm), cs]
        for k in range(1, w):
            acc = acc + ext_ref[pl.ds(POOL_HALO - w // 2 + k, tm), cs]
        cnt = (jnp.minimum(t + w // 2, length) - jnp.maximum(t - w // 2, 0)).astype(F32)
        pooled = acc / cnt - ext_ref[pl.ds(POOL_HALO, tm), cs]
        y = jnp.dot(pooled.astype(BF16), w_ref[g], preferred_element_type=F32) * sc_ref[:, cs]
        o_ref[:, cs] = y.astype(o_ref.dtype)


def _pool(p, u_blk, pool_w, pool_scale, tm=512):
    length = p.shape[0]
    tm = min(tm, length)
    n_tiles = length // tm
    hb = tm // POOL_HALO
    last_hb = length // POOL_HALO - 1
    kern = functools.partial(_pool_kernel, tm=tm, n_tiles=n_tiles, length=length)
    return pl.pallas_call(
        kern,
        grid=(n_tiles,),
        in_specs=[pl.BlockSpec((tm, POOL_WIDTH), lambda i: (i, u_blk)),
                  pl.BlockSpec((POOL_HALO, POOL_WIDTH), lambda i: (jnp.maximum(i * hb - 1, 0), u_blk)),
                  pl.BlockSpec((POOL_HALO, POOL_WIDTH), lambda i: (jnp.minimum((i + 1) * hb, last_hb), u_blk)),
                  pl.BlockSpec((len(POOL_WINDOWS), POOL_GROUP, POOL_GROUP), lambda i: (0, 0, 0)),
                  pl.BlockSpec((1, POOL_WIDTH), lambda i: (0, 0))],
        out_specs=pl.BlockSpec((tm, POOL_WIDTH), lambda i: (i, 0)),
        out_shape=jax.ShapeDtypeStruct((length, POOL_WIDTH), BF16),
        scratch_shapes=[pltpu.VMEM((tm + 2 * POOL_HALO, POOL_WIDTH), F32)],
        compiler_params=_params("arbitrary"),
        name="pool",
    )(p, p, p, pool_w.astype(BF16), pool_scale.reshape(1, POOL_WIDTH))


SSD_HALO = BF16_SUBLANE
SSD_CONV_BLK = 1024
SSD_PAIRS = SSD_HEADS // 2


def _ssd_conv_kernel(u_ref, up_ref, un_ref, tab_ref, o_ref, ext_ref, *, tm, n_tiles):
    i = pl.program_id(0)
    prv, nxt = up_ref[...], un_ref[...]
    ext_ref[0:SSD_HALO, :] = jnp.where(i == 0, jnp.zeros_like(prv), prv).astype(F32)
    ext_ref[SSD_HALO:SSD_HALO + tm, :] = u_ref[...].astype(F32)
    ext_ref[SSD_HALO + tm:SSD_HALO + tm + SSD_HALO, :] = jnp.where(i == n_tiles - 1, jnp.zeros_like(nxt),
                                                                   nxt).astype(F32)
    tab = tab_ref[...]
    left = (SSD_CONV - 1) // 2
    acc = tab[SSD_CONV:SSD_CONV + 1]
    for k in range(SSD_CONV):
        acc = acc + ext_ref[pl.ds(SSD_HALO - left + k, tm), :] * tab[k:k + 1]
    o_ref[...] = (acc * jax.nn.sigmoid(acc)).astype(o_ref.dtype)


def _ssd_conv(p, blk0, conv_w, conv_b, tm=512):
    length = p.shape[0]
    tm = min(tm, length)
    n_tiles = length // tm
    hb = tm // SSD_HALO
    last_hb = length // SSD_HALO - 1
    cb = SSD_CONV_BLK
    tab = jnp.concatenate([conv_w, conv_b[None], jnp.zeros((8 - SSD_CONV - 1, SSD_XBC), F32)], axis=0)
    kern = functools.partial(_ssd_conv_kernel, tm=tm, n_tiles=n_tiles)
    return pl.pallas_call(
        kern,
        grid=(n_tiles, SSD_XBC // cb),
        in_specs=[pl.BlockSpec((tm, cb), lambda i, j: (i, blk0 + j)),
                  pl.BlockSpec((SSD_HALO, cb), lambda i, j: (jnp.maximum(i * hb - 1, 0), blk0 + j)),
                  pl.BlockSpec((SSD_HALO, cb), lambda i, j: (jnp.minimum((i + 1) * hb, last_hb), blk0 + j)),
                  pl.BlockSpec((8, cb), lambda i, j: (0, j))],
        out_specs=pl.BlockSpec((tm, cb), lambda i, j: (i, j)),
        out_shape=jax.ShapeDtypeStruct((length, SSD_XBC), BF16),
        scratch_shapes=[pltpu.VMEM((tm + 2 * SSD_HALO, cb), F32)],
        compiler_params=_params("arbitrary", "arbitrary"),
        name="ssd_conv",
    )(p, p, p, tab)


def _ssd_dt_kernel(raw_ref, bias_ref, alog_ref, dt_ref, da_ref):
    v = raw_ref[...] + bias_ref[...]
    dt = jnp.maximum(v, 0.0) + jnp.log1p(jnp.exp(-jnp.abs(v)))
    dt_ref[...] = dt
    da_ref[...] = dt * -jnp.exp(alog_ref[...])


def _ssd_dt(raw, dt_bias, a_log, tm=2048):
    length = raw.shape[0]
    tm = min(tm, length)
    pad = LANE - 2 * SSD_HEADS
    spec = pl.BlockSpec((tm, LANE), lambda i: (i, 0))
    vec = pl.BlockSpec((1, LANE), lambda i: (0, 0))
    out = jax.ShapeDtypeStruct((length, LANE), F32)
    return pl.pallas_call(
        _ssd_dt_kernel,
        grid=(length // tm,),
        in_specs=[spec, vec, vec],
        out_specs=[spec, spec],
        out_shape=[out, out],
        compiler_params=_params("arbitrary"),
        name="ssd_dt",
    )(raw, jnp.pad(dt_bias.reshape(1, -1), ((0, 0), (0, pad))), jnp.pad(a_log.reshape(1, -1), ((0, 0), (0, pad))))


def _pieces(a):
    out = []
    for _ in range(3):
        piece = a.astype(BF16)
        out.append(piece)
        a = a - piece.astype(F32)
    return out


def _ssd_scan_kernel(*refs, direction, reverse, finalize, chunks):
    if finalize:
        (x_ref, b_ref, c_ref, dac_ref, dtr_ref, dar_ref, h0_ref, yp_ref, z_ref, dsk_ref, nw_ref,
         y_ref, hn_ref, s_ref, yacc_ref) = refs
    else:
        x_ref, b_ref, c_ref, dac_ref, dtr_ref, dar_ref, h0_ref, y_ref, hn_ref, s_ref = refs
        yacc_ref = y_ref
    step = pl.program_id(0)
    nsteps = pl.num_programs(0)
    t_len = SSD_CHUNK

    @pl.when(step == 0)
    def _():
        s_ref[...] = h0_ref[...]

    row = lax.broadcasted_iota(jnp.int32, (t_len, t_len), 0)
    col = lax.broadcasted_iota(jnp.int32, (t_len, t_len), 1)
    mask = (col >= row) if reverse else (col <= row)
    mask_t = (row >= col) if reverse else (row <= col)
    ones = jnp.ones((t_len, t_len), BF16)
    tri = jnp.where(mask, 1.0, 0.0).astype(BF16)
    tri_t = jnp.where(mask_t, 1.0, 0.0).astype(BF16)
    hsl = slice(direction * SSD_HEADS, (direction + 1) * SSD_HEADS)
    lane = lax.broadcasted_iota(jnp.int32, (t_len, LANE), 1)
    first = lane < SSD_HEAD_DIM
    edge = slice(0, 1) if reverse else slice(t_len - 1, t_len)

    for u in range(chunks):
        q = chunks - 1 - u if reverse else u
        rs = slice(q * t_len, (q + 1) * t_len)
        _ssd_chunk(x_ref, b_ref, c_ref, dac_ref, dtr_ref, dar_ref, s_ref, yacc_ref, rs, hsl, direction,
                   mask, tri, tri_t, ones, first, edge)

    @pl.when(step == nsteps - 1)
    def _():
        hn_ref[...] = s_ref[...]

    if finalize:
        y = yacc_ref[...] + yp_ref[...] + x_ref[...].astype(F32) * dsk_ref[...]
        zz = z_ref[...].astype(F32)
        gated = y * (zz * jax.nn.sigmoid(zz))
        out = gated * lax.rsqrt(jnp.mean(gated * gated, axis=-1, keepdims=True) + RMS_EPS) * nw_ref[...]
        y_ref[...] = out.astype(y_ref.dtype)


def _ssd_chunk(x_ref, b_ref, c_ref, dac_ref, dtr_ref, dar_ref, s_ref, yacc_ref, rs, hsl, direction,
               mask, tri, tri_t, ones, first, edge):
    t_len = SSD_CHUNK
    g_col = sum(jnp.dot(tri, piece, preferred_element_type=F32) for piece in _pieces(dac_ref[rs, :]))
    da_row_p = _pieces(dar_ref[hsl, rs])
    g_row = sum(jnp.dot(piece, tri_t, preferred_element_type=F32) for piece in da_row_p)
    tot_row = sum(jnp.dot(piece, ones, preferred_element_type=F32) for piece in da_row_p)
    dt_row = dtr_ref[hsl, rs]
    wdt_row = jnp.exp(tot_row - g_row) * dt_row

    for grp in range(SSD_GROUPS):
        nsl = slice(grp * SSD_STATE, (grp + 1) * SSD_STATE)
        b_g, c_g = b_ref[rs, nsl], c_ref[rs, nsl]
        cb = lax.dot_general(c_g, b_g, _NT, preferred_element_type=F32)
        bt_g = b_g.astype(F32).T
        for k in range(grp * SSD_HPG // 2, (grp + 1) * SSD_HPG // 2):
            ms, es, bws = [], [], []
            for h in (2 * k, 2 * k + 1):
                j = direction * SSD_HEADS + h
                g_bc = jnp.broadcast_to(g_col[:, j:j + 1], (t_len, t_len))
                decay = jnp.where(mask, jnp.exp(g_bc - g_row[h:h + 1, :]), 0.0)
                ms.append((decay * cb * dt_row[h:h + 1, :]).astype(BF16))
                es.append(jnp.exp(g_bc))
                bws.append((bt_g * wdt_row[h:h + 1, :]).astype(BF16))
            csl = slice(k * LANE, (k + 1) * LANE)
            xp = x_ref[rs, csl]
            xa = jnp.where(first, xp, jnp.zeros_like(xp))
            xb = jnp.where(first, jnp.zeros_like(xp), xp)
            e_pair = jnp.where(first, es[0], es[1])
            s_old = s_ref[k]
            y = (jnp.dot(ms[0], xa, preferred_element_type=F32) + jnp.dot(ms[1], xb, preferred_element_type=F32)
                 + jnp.dot(c_g, s_old.astype(BF16), preferred_element_type=F32) * e_pair)
            s_ref[k] = (s_old * e_pair[edge] + jnp.dot(bws[0], xa, preferred_element_type=F32)
                        + jnp.dot(bws[1], xb, preferred_element_type=F32))
            yacc_ref[rs, csl] = y


def _ssd_scan(xbc, da_col, dt_row, da_row, h0, direction, final=None, chunks=4):
    length = xbc.shape[0]
    t_len = SSD_CHUNK
    n_chunks = length // t_len
    chunks = min(chunks, n_chunks)
    n_steps = n_chunks // chunks
    rows = chunks * t_len
    reverse = direction == 1
    pos = (lambda c: n_steps - 1 - c) if reverse else (lambda c: c)
    di, gn = SSD_D_INNER, SSD_GN
    state = jax.ShapeDtypeStruct((SSD_PAIRS, SSD_STATE, LANE), F32)
    state_spec = pl.BlockSpec((SSD_PAIRS, SSD_STATE, LANE), lambda c: (0, 0, 0))
    in_specs = [pl.BlockSpec((rows, di), lambda c: (pos(c), 0)),
                pl.BlockSpec((rows, gn), lambda c: (pos(c), di // gn)),
                pl.BlockSpec((rows, gn), lambda c: (pos(c), di // gn + 1)),
                pl.BlockSpec((rows, LANE), lambda c: (pos(c), 0)),
                pl.BlockSpec((2 * SSD_HEADS, rows), lambda c: (0, pos(c))),
                pl.BlockSpec((2 * SSD_HEADS, rows), lambda c: (0, pos(c))),
                state_spec]
    args = [xbc, xbc, xbc, da_col, dt_row, da_row, h0]
    scratch = [pltpu.VMEM((SSD_PAIRS, SSD_STATE, LANE), F32)]
    y_spec = pl.BlockSpec((rows, di), lambda c: (pos(c), 0))
    row_spec = pl.BlockSpec((1, di), lambda c: (0, 0))
    if final is not None:
        y_other, p, d_skip_row, norm_w_row = final
        in_specs += [y_spec, y_spec, row_spec, row_spec]
        args += [y_other, p, d_skip_row, norm_w_row]
        scratch.append(pltpu.VMEM((rows, di), F32))
    y_dtype = BF16 if final is not None else F32
    kern = functools.partial(_ssd_scan_kernel, direction=direction, reverse=reverse, finalize=final is not None,
                             chunks=chunks)
    return pl.pallas_call(
        kern,
        grid=(n_steps,),
        in_specs=in_specs,
        out_specs=[y_spec, state_spec],
        out_shape=[jax.ShapeDtypeStruct((length, di), y_dtype), state],
        scratch_shapes=scratch,
        compiler_params=_params("arbitrary"),
        name="ssd_scan",
    )(*args)


def _ssd_inputs(p, dt_raw, conv_w, conv_b, dt_bias, a_log):
    xbc = _ssd_conv(p, P_XBC // SSD_CONV_BLK, conv_w, conv_b)
    dt, da = _ssd_dt(dt_raw, dt_bias, a_log)
    nh = 2 * SSD_HEADS
    return xbc, da, dt[:, :nh].T, da[:, :nh].T


def _ssd_pallas(p_l, dt_raw_l, p_c, dt_raw_c, conv_w, conv_b, dt_bias, a_log, d_skip, norm_w, ctx_out):
    xbc_l, dac_l, dtr_l, dar_l = _ssd_inputs(p_l, dt_raw_l, conv_w, conv_b, dt_bias, a_log)
    xbc_c, dac_c, dtr_c, dar_c = _ssd_inputs(p_c, dt_raw_c, conv_w, conv_b, dt_bias, a_log)
    zero = jnp.zeros((SSD_PAIRS, SSD_STATE, LANE), F32)
    fin = (jnp.repeat(d_skip, SSD_HEAD_DIM).reshape(1, -1), norm_w.reshape(1, -1))
    yc0, h_fwd = _ssd_scan(xbc_c, dac_c, dtr_c, dar_c, zero, 0)
    if ctx_out:
        out_c, h_bwd = _ssd_scan(xbc_c, dac_c, dtr_c, dar_c, zero, 1, final=(yc0, p_c) + fin)
    else:
        out_c = None
        _, h_bwd = _ssd_scan(xbc_c, dac_c, dtr_c, dar_c, zero, 1)
    yl0, _ = _ssd_scan(xbc_l, dac_l, dtr_l, dar_l, h_fwd, 0)
    out_l, _ = _ssd_scan(xbc_l, dac_l, dtr_l, dar_l, h_bwd, 1, final=(yl0, p_l) + fin)
    return out_l, out_c


P_Z, P_Q, P_POOL, P_XBC = OFF_Z, OFF_Q, OFF_POOL, OFF_XBC
P_COLS = OFF_DT
KV_K, KV_V = 0, NA_WIDTH


def _layer(layer, x, ctx, mod_l, mod_c, norm1_w, w_in_all, w_kv_all, w_dt_all, ssd_conv_w, ssd_conv_b, ssd_dt_bias,
           ssd_a_log, ssd_d, ssd_norm_w, na_q_norm_w, na_k_norm_w, na_rpb, pool_w, pool_scale, w_out_all, norm2_w,
           w_up_all, mlp_conv_w, mlp_conv_b, w_down_all, rope_l, rope_c, ctx_out):
    sh1_l, sc1_l, g1_l, sh2_l, sc2_l, g2_l = jnp.split(mod_l, N_MOD)
    sh1_c, sc1_c, g1_c, sh2_c, sc2_c, g2_c = jnp.split(mod_c, N_MOD)
    conv_tab = jnp.concatenate([mlp_conv_w, mlp_conv_b[None], jnp.zeros((4, 2 * D_FF), F32)], axis=0)

    h_l = _norm_mod(x, norm1_w, sc1_l, sh1_l)
    h_c = _norm_mod(ctx, norm1_w, sc1_c, sh1_c)
    p_l = _matmul(h_l, w_in_all, layer, BF16, n_cols=P_COLS)
    kv_l = _matmul(h_l, w_kv_all, layer, BF16)
    dt_l = _matmul(h_l, w_dt_all, layer, F32)
    p_c = _matmul(h_c, w_in_all, layer, BF16, n_cols=P_COLS)
    kv_c = _matmul(h_c, w_kv_all, layer, BF16)
    dt_c = _matmul(h_c, w_dt_all, layer, F32)

    y_ssd_l, y_ssd_c = _ssd_pallas(p_l, dt_l, p_c, dt_c, ssd_conv_w, ssd_conv_b, ssd_dt_bias, ssd_a_log, ssd_d,
                                   ssd_norm_w, ctx_out)
    q_blk, k_blk, v_blk = P_Q // NA_WIDTH, KV_K // NA_WIDTH, KV_V // NA_HEAD_DIM
    q_r, k_r = _qk_prep(p_l, q_blk, kv_l, k_blk, na_q_norm_w, na_k_norm_w, rope_l)
    qc_r, kc_r = _qk_prep(p_c, q_blk, kv_c, k_blk, na_q_norm_w, na_k_norm_w, rope_c)
    y_na_l = _na_latent(q_r, k_r, kv_l, v_blk, kc_r, kv_c, v_blk, _na_bias_table(na_rpb))
    y_pool_l = _pool(p_l, P_POOL // POOL_WIDTH, pool_w, pool_scale)

    x_mid = _outproj([y_ssd_l, y_na_l, y_pool_l], w_out_all, layer, x, g1_l)
    h2_l = _norm_mod(x_mid, norm2_w, sc2_l, sh2_l)
    x_new = _conv_ffn(h2_l, w_up_all, conv_tab, w_down_all, layer, x_mid, g2_l)
    ctx_new = ctx
    if ctx_out:
        y_na_c = _na_context(qc_r, kc_r, kv_c, v_blk)
        y_pool_c = _pool(p_c, P_POOL // POOL_WIDTH, pool_w, pool_scale)
        c_mid = _outproj([y_ssd_c, y_na_c, y_pool_c], w_out_all, layer, ctx, g1_c)
        h2_c = _norm_mod(c_mid, norm2_w, sc2_c, sh2_c)
        ctx_new = _conv_ffn(h2_c, w_up_all, conv_tab, w_down_all, layer, c_mid, g2_c)
    return x_new, ctx_new


def kernel(x, c, ctx, c_ctx, ada_w, ada_b, norm1_w, w_in, ssd_conv_w, ssd_conv_b, ssd_dt_bias, ssd_a_log,
           ssd_d, ssd_norm_w, na_q_norm_w, na_k_norm_w, na_rpb, pool_w, pool_scale, w_out, norm2_w,
           mlp_w_up, mlp_conv_w, mlp_conv_b, mlp_w_down):
    depth = ada_w.shape[0]
    d = x.shape[-1]
    c_rows = jnp.concatenate([c.reshape(1, d), c_ctx.reshape(1, d), jnp.zeros((MOD_ROWS - 2, d), F32)], axis=0)
    mods = _modulation(c_rows, ada_w, ada_b)
    xs, cs = x[0], ctx[0]
    rope_l = _rope_tables(xs.shape[0])
    ctx_tab = (cs.shape[0], NA_HEAD_DIM)
    rope_c = (jnp.ones(ctx_tab, F32), jnp.zeros(ctx_tab, F32), jnp.zeros(ctx_tab, F32))
    w_in_all = w_in.astype(BF16)
    w_kv_all = w_in[:, :, OFF_K:].astype(BF16)
    w_dt_all = jnp.pad(w_in[:, :, OFF_DT:OFF_K], ((0, 0), (0, 0), (0, LANE - 2 * SSD_HEADS))).astype(BF16)
    w_out_all = w_out.astype(BF16)
    w_up_all = mlp_w_up.astype(BF16)
    for i in range(depth):
        xs, cs = _layer(i, xs, cs, mods[i, 0], mods[i, 1], norm1_w[i], w_in_all, w_kv_all, w_dt_all, ssd_conv_w[i],
                        ssd_conv_b[i], ssd_dt_bias[i], ssd_a_log[i], ssd_d[i], ssd_norm_w[i], na_q_norm_w[i],
                        na_k_norm_w[i], na_rpb[i], pool_w[i], pool_scale[i], w_out_all, norm2_w[i], w_up_all,
                        mlp_conv_w[i], mlp_conv_b[i], mlp_w_down, rope_l, rope_c, ctx_out=(i < depth - 1))
    return xs[None]
```

```python
import functools
import math

import jax
import jax.numpy as jnp
from jax import lax
from jax.experimental import pallas as pl
from jax.experimental.pallas import tpu as pltpu

F32 = jnp.float32
BF16 = jnp.bfloat16

D_MODEL = 4096
DEPTH = 2
GRID_W = 64
D_MIX = D_MODEL
SSD_D_INNER = D_MIX // 2
SSD_HEAD_DIM = 64
SSD_HEADS = SSD_D_INNER // SSD_HEAD_DIM
SSD_GROUPS = 4
SSD_HPG = SSD_HEADS // SSD_GROUPS
SSD_STATE = 128
SSD_GN = SSD_GROUPS * SSD_STATE
SSD_XBC = SSD_D_INNER + 2 * SSD_GN
SSD_CONV = 5
SSD_CHUNK = 128
NA_WIDTH = D_MIX // 4
NA_HEAD_DIM = 128
NA_HEADS = NA_WIDTH // NA_HEAD_DIM
NA_WIN_R = 8
NA_WIN_C = 16
ROPE_BASE = 10000.0
POOL_WIDTH = D_MIX - SSD_D_INNER - NA_WIDTH
POOL_WINDOWS = (2, 4, 8, 16)
POOL_GROUP = POOL_WIDTH // len(POOL_WINDOWS)
D_FF = 11008
N_MOD = 6
RMS_EPS = 1e-6
OFF_Z = 0
OFF_Q = OFF_Z + SSD_D_INNER
OFF_POOL = OFF_Q + NA_WIDTH
OFF_XBC = OFF_POOL + POOL_WIDTH
OFF_DT = OFF_XBC + SSD_XBC
OFF_K = OFF_DT + 2 * SSD_HEADS
OFF_V = OFF_K + NA_WIDTH
IN_COLS = OFF_V + NA_WIDTH

VMEM_LIMIT_BYTES = 58 * 1024 * 1024
LANE = 128
BF16_SUBLANE = 16
MOD_ROWS = 8


def _params(*sem):
    return pltpu.CompilerParams(dimension_semantics=sem, vmem_limit_bytes=VMEM_LIMIT_BYTES)


def _mod_kernel(c_ref, w_ref, b_ref, o_ref):
    c = c_ref[...]
    s = (c * jax.nn.sigmoid(c)).astype(BF16)
    o_ref[0] = jnp.dot(s, w_ref[0].astype(BF16), preferred_element_type=F32) + b_ref[0]


def _modulation(c_rows, ada_w, ada_b, tn=1024):
    depth, d, n = ada_w.shape
    return pl.pallas_call(
        _mod_kernel,
        grid=(depth, n // tn),
        in_specs=[pl.BlockSpec((MOD_ROWS, d), lambda l, j: (0, 0)),
                  pl.BlockSpec((1, d, tn), lambda l, j: (l, 0, j)),
                  pl.BlockSpec((1, 1, tn), lambda l, j: (l, 0, j))],
        out_specs=pl.BlockSpec((1, MOD_ROWS, tn), lambda l, j: (l, 0, j)),
        out_shape=jax.ShapeDtypeStruct((depth, MOD_ROWS, n), F32),
        compiler_params=_params("arbitrary", "arbitrary"),
        name="modulation",
    )(c_rows, ada_w, ada_b.reshape(depth, 1, n))


def _norm_kernel(x_ref, w_ref, sc_ref, sh_ref, o_ref):
    x = x_ref[...]
    y = x * lax.rsqrt(jnp.mean(x * x, axis=-1, keepdims=True) + RMS_EPS) * w_ref[...]
    o_ref[...] = (y * (1.0 + sc_ref[...]) + sh_ref[...]).astype(o_ref.dtype)


def _norm_mod(x, w, scale, shift, tm=512):
    m, d = x.shape
    tm = min(tm, m)
    row = pl.BlockSpec((1, d), lambda i: (0, 0))
    return pl.pallas_call(
        _norm_kernel,
        grid=(m // tm,),
        in_specs=[pl.BlockSpec((tm, d), lambda i: (i, 0)), row, row, row],
        out_specs=pl.BlockSpec((tm, d), lambda i: (i, 0)),
        out_shape=jax.ShapeDtypeStruct((m, d), BF16),
        compiler_params=_params("arbitrary"),
        name="norm_mod",
    )(x, w.reshape(1, d), scale.reshape(1, d), shift.reshape(1, d))


def _mm_kernel(a_ref, b_ref, o_ref):
    o_ref[...] = jnp.dot(a_ref[...], b_ref[0], preferred_element_type=F32).astype(o_ref.dtype)


def _matmul(a, b_all, layer, out_dtype, tm=1024, tn=1024):
    m, k = a.shape
    n = b_all.shape[2]
    tm, tn = min(tm, m), min(tn, n)
    assert n % tn == 0
    return pl.pallas_call(
        _mm_kernel,
        grid=(m // tm, n // tn),
        in_specs=[pl.BlockSpec((tm, k), lambda i, j: (i, 0)),
                  pl.BlockSpec((1, k, tn), lambda i, j: (layer, 0, j))],
        out_specs=pl.BlockSpec((tm, tn), lambda i, j: (i, j)),
        out_shape=jax.ShapeDtypeStruct((m, n), out_dtype),
        compiler_params=_params("arbitrary", "arbitrary"),
        name="matmul",
    )(a, b_all)


def _outproj_kernel(*refs, widths):
    a_refs, (w_ref, x_ref, g_ref, o_ref) = refs[:len(widths)], refs[len(widths):]
    acc, off = None, 0
    for a_ref, k in zip(a_refs, widths):
        part = jnp.dot(a_ref[...], w_ref[0, off:off + k, :], preferred_element_type=F32)
        acc = part if acc is None else acc + part
        off += k
    o_ref[...] = x_ref[...] + g_ref[...] * acc


def _outproj(parts, w_all, layer, x, gate, tm=1024, tn=1024):
    m = x.shape[0]
    _, k, n = w_all.shape
    tm = min(tm, m)
    widths = tuple(a.shape[1] for a in parts)
    assert sum(widths) == k
    return pl.pallas_call(
        functools.partial(_outproj_kernel, widths=widths),
        grid=(m // tm, n // tn),
        in_specs=[pl.BlockSpec((tm, kw), lambda i, j: (i, 0)) for kw in widths] + [
            pl.BlockSpec((1, k, tn), lambda i, j: (layer, 0, j)),
            pl.BlockSpec((tm, tn), lambda i, j: (i, j)),
            pl.BlockSpec((1, tn), lambda i, j: (0, j))],
        out_specs=pl.BlockSpec((tm, tn), lambda i, j: (i, j)),
        out_shape=jax.ShapeDtypeStruct((m, n), F32),
        compiler_params=_params("arbitrary", "arbitrary"),
        name="outproj",
    )(*parts, w_all, x, gate.reshape(1, n))


FFN_HALO = BF16_SUBLANE


def _ffn_kernel(h_ref, hp_ref, hn_ref, wg_ref, wv_ref, tab_ref, wd_ref, x_ref, g_ref,
                o_ref, ext_ref, ua_ref, ub_ref, *, tm, n_mtiles, nf, tn_d):
    i = pl.program_id(0)
    f = pl.program_id(1)
    d = o_ref.shape[1]
    u_refs = (ua_ref, ub_ref)

    @pl.when(f == 0)
    def _():
        ext_ref[0:tm, :] = h_ref[...]
        half = FFN_HALO // 2
        nxt = hn_ref[...].astype(F32)[0:half]
        prv = hp_ref[...].astype(F32)[half:FFN_HALO]
        nxt = jnp.where(i == n_mtiles - 1, jnp.zeros_like(nxt), nxt)
        prv = jnp.where(i == 0, jnp.zeros_like(prv), prv)
        ext_ref[tm:tm + FFN_HALO, :] = jnp.concatenate([nxt, prv], axis=0).astype(BF16)
        o_ref[...] = jnp.zeros_like(o_ref)
        ub_ref[...] = jnp.zeros_like(ub_ref)

    def conv(u_ref, br, c):
        um = u_ref[br, 0:tm]
        row = lax.broadcasted_iota(jnp.int32, um.shape, 0)
        before = tm + FFN_HALO - 1
        up = jnp.where(row == 0, u_ref[br, before:before + 1], pltpu.roll(um, 1, 0))
        un = jnp.where(row == tm - 1, u_ref[br, tm:tm + 1], pltpu.roll(um, tm - 1, 0))
        return up * c[0:1] + um * c[1:2] + un * c[2:3] + c[3:4]

    def step(produce, consume):
        h = ext_ref[...]
        p_ref, c_ref = u_refs[produce], u_refs[consume]
        p_ref[0] = jnp.dot(h, wg_ref[0], preferred_element_type=F32)
        p_ref[1] = jnp.dot(h, wv_ref[0], preferred_element_type=F32)
        tile = jnp.maximum(f - 1, 0)
        gate = conv(c_ref, 0, tab_ref[tile])
        val = conv(c_ref, 1, tab_ref[tile + nf])
        act = gate * jax.nn.sigmoid(gate) * val
        act = jnp.where(f > 0, act, jnp.zeros_like(act)).astype(BF16)
        for n in range(d // tn_d):
            sl = slice(n * tn_d, (n + 1) * tn_d)
            o_ref[:, sl] += jnp.dot(act, wd_ref[0, :, sl].astype(BF16), preferred_element_type=F32)

    @pl.when(f % 2 == 0)
    def _():
        step(0, 1)

    @pl.when(f % 2 == 1)
    def _():
        step(1, 0)

    @pl.when(f == nf)
    def _():
        o_ref[...] = x_ref[...] + g_ref[...] * o_ref[...]


def _conv_ffn(h, w_up_all, conv_tab, w_down_all, layer, x, gate, tm=512, tf=256, tn_d=1024):
    m, d = h.shape
    ff = w_down_all.shape[1]
    tm = min(tm, m)
    tn_d = min(tn_d, d)
    n_mtiles, nf = m // tm, ff // tf
    hb = tm // FFN_HALO
    last_hb = m // FFN_HALO - 1
    kern = functools.partial(_ffn_kernel, tm=tm, n_mtiles=n_mtiles, nf=nf, tn_d=tn_d)
    tab = conv_tab.reshape(8, 2 * nf, tf).transpose(1, 0, 2)

    def prod(f):
        return jnp.minimum(f, nf - 1)

    def cons(f):
        return jnp.maximum(f - 1, 0)

    return pl.pallas_call(
        kern,
        grid=(n_mtiles, nf + 1),
        in_specs=[
            pl.BlockSpec((tm, d), lambda i, f: (i, 0), pipeline_mode=pl.Buffered(1)),
            pl.BlockSpec((FFN_HALO, d), lambda i, f: (jnp.maximum(i * hb - 1, 0), 0)),
            pl.BlockSpec((FFN_HALO, d), lambda i, f: (jnp.minimum((i + 1) * hb, last_hb), 0)),
            pl.BlockSpec((1, d, tf), lambda i, f: (layer, 0, prod(f))),
            pl.BlockSpec((1, d, tf), lambda i, f: (layer, 0, prod(f) + nf)),
            pl.BlockSpec((2 * nf, 8, tf), lambda i, f: (0, 0, 0)),
            pl.BlockSpec((1, tf, d), lambda i, f: (layer, cons(f), 0)),
            pl.BlockSpec((tm, d), lambda i, f: (i, 0), pipeline_mode=pl.Buffered(1)),
            pl.BlockSpec((1, d), lambda i, f: (0, 0)),
        ],
        out_specs=pl.BlockSpec((tm, d), lambda i, f: (i, 0)),
        out_shape=jax.ShapeDtypeStruct((m, d), F32),
        scratch_shapes=[pltpu.VMEM((tm + FFN_HALO, d), BF16),
                        pltpu.VMEM((2, tm + FFN_HALO, tf), F32),
                        pltpu.VMEM((2, tm + FFN_HALO, tf), F32)],
        compiler_params=_params("arbitrary", "arbitrary"),
        name="conv_ffn",
    )(h, h, h, w_up_all, w_up_all, tab, w_down_all, x, gate.reshape(1, d))


def _rope_tables(length):
    t = jnp.arange(length)
    half = NA_HEAD_DIM // 2
    inv = ROPE_BASE ** (-jnp.arange(0, half, 2, dtype=F32) / half)
    ar = (t // GRID_W).astype(F32)[:, None] * inv
    ac = (t % GRID_W).astype(F32)[:, None] * inv
    ang = jnp.concatenate([ar, ar, ac, ac], axis=-1)
    cos, sin = jnp.cos(ang), jnp.sin(ang)
    first = (jnp.arange(NA_HEAD_DIM) % half) < half // 2
    return cos, jnp.where(first, -sin, 0.0), jnp.where(first, 0.0, sin)


def _qk_prep_kernel(q_ref, k_ref, qw_ref, kw_ref, cos_ref, sa_ref, sb_ref, qo_ref, ko_ref):
    cos, sa, sb = cos_ref[...], sa_ref[...], sb_ref[...]
    quarter = NA_HEAD_DIM // 4
    for h in range(NA_HEADS):
        sl = slice(h * NA_HEAD_DIM, (h + 1) * NA_HEAD_DIM)
        for src, w_ref, dst in ((q_ref, qw_ref, qo_ref), (k_ref, kw_ref, ko_ref)):
            x = src[:, sl].astype(F32)
            y = x * lax.rsqrt(jnp.mean(x * x, axis=-1, keepdims=True) + RMS_EPS) * w_ref[...]
            y = (y * cos + pltpu.roll(y, NA_HEAD_DIM - quarter, 1) * sa + pltpu.roll(y, quarter, 1) * sb)
            dst[:, sl] = y.astype(dst.dtype)


def _qk_prep(p_q, q_blk, p_k, k_blk, qw, kw, tables, tm=512):
    m = p_q.shape[0]
    tm = min(tm, m)
    tab = pl.BlockSpec((tm, NA_HEAD_DIM), lambda i: (i, 0))
    vec = pl.BlockSpec((1, NA_HEAD_DIM), lambda i: (0, 0))
    out = jax.ShapeDtypeStruct((m, NA_WIDTH), BF16)
    return pl.pallas_call(
        _qk_prep_kernel,
        grid=(m // tm,),
        in_specs=[pl.BlockSpec((tm, NA_WIDTH), lambda i: (i, q_blk)),
                  pl.BlockSpec((tm, NA_WIDTH), lambda i: (i, k_blk)), vec, vec, tab, tab, tab],
        out_specs=[pl.BlockSpec((tm, NA_WIDTH), lambda i: (i, 0))] * 2,
        out_shape=[out, out],
        compiler_params=_params("arbitrary"),
        name="qk_prep",
    )(p_q, p_k, qw.reshape(1, -1), kw.reshape(1, -1), *tables)


NA_DR = 2 * NA_WIN_R
NA_BIAS_CHUNK = 1024


def _na_bias_kernel(r0_ref, r1_ref, o_ref):
    def pieces(r):
        out = []
        for _ in range(3):
            piece = r.astype(BF16)
            out.append(piece)
            r = r - piece.astype(F32)
        return out

    p0, p1 = pieces(r0_ref[0]), pieces(r1_ref[0])
    for c in range(GRID_W * LANE // NA_BIAS_CHUNK):
        shape = (LANE, NA_BIAS_CHUNK)
        pos = c * NA_BIAS_CHUNK + lax.broadcasted_iota(jnp.int32, shape, 1)
        j = lax.broadcasted_iota(jnp.int32, shape, 0)
        qc = pos // LANE
        lane = pos % LANE
        kc = lane % GRID_W
        idx = jnp.clip(kc - qc + (NA_WIN_C - 1), 0, 2 * NA_WIN_C - 2)
        hit = idx == j
        second = lane >= GRID_W
        oh0 = jnp.where(hit, jnp.where(second, 0.0, 1.0), 0.0).astype(BF16)
        oh1 = jnp.where(hit, jnp.where(second, 1.0, 0.0), 0.0).astype(BF16)
        acc = jnp.zeros((NA_DR, NA_BIAS_CHUNK), F32)
        for a, b in zip(p0, p1):
            acc = acc + jnp.dot(a, oh0, preferred_element_type=F32) + jnp.dot(b, oh1, preferred_element_type=F32)
        c0 = jnp.clip(qc[0:1] - NA_WIN_C // 2, 0, GRID_W - NA_WIN_C)
        ok = (kc[0:1] >= c0) & (kc[0:1] < c0 + NA_WIN_C)
        o_ref[0, :, c * NA_BIAS_CHUNK:(c + 1) * NA_BIAS_CHUNK] = jnp.where(ok, acc, -jnp.inf)


def _na_bias_table(rpb):
    nh, ndr, ndc = rpb.shape
    r0 = jnp.pad(rpb, ((0, 0), (0, NA_DR - ndr), (0, LANE - ndc)))
    r1 = jnp.pad(rpb[:, 1:], ((0, 0), (0, NA_DR - ndr + 1), (0, LANE - ndc)))
    spec = pl.BlockSpec((1, NA_DR, LANE), lambda h: (h, 0, 0))
    out = pl.pallas_call(
        _na_bias_kernel,
        grid=(nh,),
        in_specs=[spec, spec],
        out_specs=pl.BlockSpec((1, NA_DR, GRID_W * LANE), lambda h: (h, 0, 0)),
        out_shape=jax.ShapeDtypeStruct((nh, NA_DR, GRID_W * LANE), F32),
        compiler_params=_params("arbitrary"),
        name="na_bias",
    )(r0, r1)
    return out.reshape(nh, NA_DR, GRID_W, LANE)


def _softmax_parts(scores):
    m = functools.reduce(jnp.maximum, [jnp.max(s, axis=1, keepdims=True) for s in scores])
    ps = [jnp.exp(s - m) for s in scores]
    den = functools.reduce(jnp.add, [jnp.sum(p, axis=1, keepdims=True) for p in ps])
    return [p.astype(BF16) for p in ps], den


def _pv(parts, values):
    ps, den = parts
    o = functools.reduce(jnp.add, [jnp.dot(p, v, preferred_element_type=F32) for p, v in zip(ps, values)])
    return o / den


def _softmax_pv(scores, values):
    return _pv(_softmax_parts(scores), values)


_NT = (((1,), (1,)), ((), ()))


def _na_kernel(q_ref, k_ref, v_ref, kc_ref, vc_ref, tz_ref, o_ref, *, rows_per_step, n_rows):
    t = pl.program_id(1)
    scale = NA_HEAD_DIM ** -0.5
    band = NA_WIN_R * GRID_W
    kc, vc = kc_ref[...], vc_ref[...]
    starts, scores = [], []
    for i in range(rows_per_step):
        qr = t * rows_per_step + i
        bs = jnp.clip(qr - NA_WIN_R // 2, 0, n_rows - NA_WIN_R)
        off = bs - qr + (NA_WIN_R - 1)
        start = pl.multiple_of(bs * GRID_W, GRID_W)
        q = q_ref[i * GRID_W:(i + 1) * GRID_W, :]
        kb = k_ref[pl.ds(start, band), :]
        bias = jnp.concatenate([tz_ref[0, off + 2 * w] for w in range(NA_WIN_R // 2)], axis=1)
        s_loc = lax.dot_general(q, kb, _NT, preferred_element_type=F32) * scale + bias
        s_ctx = lax.dot_general(q, kc, _NT, preferred_element_type=F32) * scale
        starts.append(start)
        scores.append((s_loc, s_ctx))
    probs = [_softmax_parts(list(s)) for s in scores]
    for i in range(rows_per_step):
        vb = v_ref[pl.ds(starts[i], band), :]
        o = _pv(probs[i], [vb, vc])
        o_ref[i * GRID_W:(i + 1) * GRID_W, :] = o.astype(o_ref.dtype)


def _na_latent(q_r, k_r, p_l, v_blk, kc_r, p_c, vc_blk, tz, rows_per_step=16):
    length = q_r.shape[0]
    lc = kc_r.shape[0]
    n_rows = length // GRID_W
    assert n_rows >= NA_WIN_R and n_rows % rows_per_step == 0
    hd = NA_HEAD_DIM
    kern = functools.partial(_na_kernel, rows_per_step=rows_per_step, n_rows=n_rows)
    return pl.pallas_call(
        kern,
        grid=(NA_HEADS, n_rows // rows_per_step),
        in_specs=[pl.BlockSpec((rows_per_step * GRID_W, hd), lambda h, t: (t, h)),
                  pl.BlockSpec((length, hd), lambda h, t: (0, h)),
                  pl.BlockSpec((length, hd), lambda h, t: (0, v_blk + h)),
                  pl.BlockSpec((lc, hd), lambda h, t: (0, h)),
                  pl.BlockSpec((lc, hd), lambda h, t: (0, vc_blk + h)),
                  pl.BlockSpec((1, NA_DR, GRID_W, LANE), lambda h, t: (h, 0, 0, 0))],
        out_specs=pl.BlockSpec((rows_per_step * GRID_W, hd), lambda h, t: (t, h)),
        out_shape=jax.ShapeDtypeStruct((length, NA_WIDTH), BF16),
        compiler_params=_params("arbitrary", "arbitrary"),
        name="na_latent",
    )(q_r, k_r, p_l, kc_r, p_c, tz)


def _ctx_attn_kernel(q_ref, k_ref, v_ref, o_ref):
    s = lax.dot_general(q_ref[...], k_ref[...], _NT, preferred_element_type=F32) * (NA_HEAD_DIM ** -0.5)
    o_ref[...] = _softmax_pv([s], [v_ref[...]]).astype(o_ref.dtype)


def _na_context(qc_r, kc_r, p_c, vc_blk):
    lc = qc_r.shape[0]
    hd = NA_HEAD_DIM
    spec = pl.BlockSpec((lc, hd), lambda h: (0, h))
    return pl.pallas_call(
        _ctx_attn_kernel,
        grid=(NA_HEADS,),
        in_specs=[spec, spec, pl.BlockSpec((lc, hd), lambda h: (0, vc_blk + h))],
        out_specs=spec,
        out_shape=jax.ShapeDtypeStruct((lc, NA_WIDTH), BF16),
        compiler_params=_params("arbitrary"),
        name="na_context",
    )(qc_r, kc_r, p_c)


POOL_HALO = BF16_SUBLANE


def _pool_kernel(u_ref, up_ref, un_ref, w_ref, sc_ref, o_ref, ext_ref, *, tm, n_tiles, length):
    i = pl.program_id(0)
    prv, nxt = up_ref[...], un_ref[...]
    ext_ref[0:POOL_HALO, :] = jnp.where(i == 0, jnp.zeros_like(prv), prv).astype(F32)
    ext_ref[POOL_HALO:POOL_HALO + tm, :] = u_ref[...].astype(F32)
    ext_ref[POOL_HALO + tm:POOL_HALO + tm + POOL_HALO, :] = jnp.where(i == n_tiles - 1, jnp.zeros_like(nxt),
                                                                     nxt).astype(F32)
    t = i * tm + lax.broadcasted_iota(jnp.int32, (tm, 1), 0)
    for g, w in enumerate(POOL_WINDOWS):
        cs = slice(g * POOL_GROUP, (g + 1) * POOL_GROUP)
        acc = ext_ref[pl.ds(POOL_HALO - w // 2, tm), cs]
        for k in range(1, w):
            acc = acc + ext_ref[pl.ds(POOL_HALO - w // 2 + k, tm), cs]
        cnt = (jnp.minimum(t + w // 2, length) - jnp.maximum(t - w // 2, 0)).astype(F32)
        pooled = acc / cnt - ext_ref[pl.ds(POOL_HALO, tm), cs]
        y = jnp.dot(pooled.astype(BF16), w_ref[g], preferred_element_type=F32) * sc_ref[:, cs]
        o_ref[:, cs] = y.astype(o_ref.dtype)


def _pool(p, u_blk, pool_w, pool_scale, tm=512):
    length = p.shape[0]
    tm = min(tm, length)
    n_tiles = length // tm
    hb = tm // POOL_HALO
    last_hb = length // POOL_HALO - 1
    kern = functools.partial(_pool_kernel, tm=tm, n_tiles=n_tiles, length=length)
    return pl.pallas_call(
        kern,
        grid=(n_tiles,),
        in_specs=[pl.BlockSpec((tm, POOL_WIDTH), lambda i: (i, u_blk)),
                  pl.BlockSpec((POOL_HALO, POOL_WIDTH), lambda i: (jnp.maximum(i * hb - 1, 0), u_blk)),
                  pl.BlockSpec((POOL_HALO, POOL_WIDTH), lambda i: (jnp.minimum((i + 1) * hb, last_hb), u_blk)),
                  pl.BlockSpec((len(POOL_WINDOWS), POOL_GROUP, POOL_GROUP), lambda i: (0, 0, 0)),
                  pl.BlockSpec((1, POOL_WIDTH), lambda i: (0, 0))],
        out_specs=pl.BlockSpec((tm, POOL_WIDTH), lambda i: (i, 0)),
        out_shape=jax.ShapeDtypeStruct((length, POOL_WIDTH), BF16),
        scratch_shapes=[pltpu.VMEM((tm + 2 * POOL_HALO, POOL_WIDTH), F32)],
        compiler_params=_params("arbitrary"),
        name="pool",
    )(p, p, p, pool_w.astype(BF16), pool_scale.reshape(1, POOL_WIDTH))


SSD_HALO = BF16_SUBLANE
SSD_CONV_BLK = 1024
SSD_PAIRS = SSD_HEADS // 2


def _ssd_conv_kernel(u_ref, up_ref, un_ref, tab_ref, o_ref, ext_ref, *, tm, n_tiles):
    i = pl.program_id(0)
    prv, nxt = up_ref[...], un_ref[...]
    ext_ref[0:SSD_HALO, :] = jnp.where(i == 0, jnp.zeros_like(prv), prv).astype(F32)
    ext_ref[SSD_HALO:SSD_HALO + tm, :] = u_ref[...].astype(F32)
    ext_ref[SSD_HALO + tm:SSD_HALO + tm + SSD_HALO, :] = jnp.where(i == n_tiles - 1, jnp.zeros_like(nxt),
                                                                   nxt).astype(F32)
    tab = tab_ref[...]
    left = (SSD_CONV - 1) // 2
    acc = tab[SSD_CONV:SSD_CONV + 1]
    for k in range(SSD_CONV):
        acc = acc + ext_ref[pl.ds(SSD_HALO - left + k, tm), :] * tab[k:k + 1]
    o_ref[...] = (acc * jax.nn.sigmoid(acc)).astype(o_ref.dtype)


def _ssd_conv(p, blk0, conv_w, conv_b, tm=512):
    length = p.shape[0]
    tm = min(tm, length)
    n_tiles = length // tm
    hb = tm // SSD_HALO
    last_hb = length // SSD_HALO - 1
    cb = SSD_CONV_BLK
    tab = jnp.concatenate([conv_w, conv_b[None], jnp.zeros((8 - SSD_CONV - 1, SSD_XBC), F32)], axis=0)
    kern = functools.partial(_ssd_conv_kernel, tm=tm, n_tiles=n_tiles)
    return pl.pallas_call(
        kern,
        grid=(n_tiles, SSD_XBC // cb),
        in_specs=[pl.BlockSpec((tm, cb), lambda i, j: (i, blk0 + j)),
                  pl.BlockSpec((SSD_HALO, cb), lambda i, j: (jnp.maximum(i * hb - 1, 0), blk0 + j)),
                  pl.BlockSpec((SSD_HALO, cb), lambda i, j: (jnp.minimum((i + 1) * hb, last_hb), blk0 + j)),
                  pl.BlockSpec((8, cb), lambda i, j: (0, j))],
        out_specs=pl.BlockSpec((tm, cb), lambda i, j: (i, j)),
        out_shape=jax.ShapeDtypeStruct((length, SSD_XBC), BF16),
        scratch_shapes=[pltpu.VMEM((tm + 2 * SSD_HALO, cb), F32)],
        compiler_params=_params("arbitrary", "arbitrary"),
        name="ssd_conv",
    )(p, p, p, tab)


def _ssd_dt_kernel(raw_ref, bias_ref, alog_ref, dt_ref, da_ref):
    v = raw_ref[...] + bias_ref[...]
    dt = jnp.maximum(v, 0.0) + jnp.log1p(jnp.exp(-jnp.abs(v)))
    dt_ref[...] = dt
    da_ref[...] = dt * -jnp.exp(alog_ref[...])


def _ssd_dt(raw, dt_bias, a_log, tm=2048):
    length = raw.shape[0]
    tm = min(tm, length)
    pad = LANE - 2 * SSD_HEADS
    spec = pl.BlockSpec((tm, LANE), lambda i: (i, 0))
    vec = pl.BlockSpec((1, LANE), lambda i: (0, 0))
    out = jax.ShapeDtypeStruct((length, LANE), F32)
    return pl.pallas_call(
        _ssd_dt_kernel,
        grid=(length // tm,),
        in_specs=[spec, vec, vec],
        out_specs=[spec, spec],
        out_shape=[out, out],
        compiler_params=_params("arbitrary"),
        name="ssd_dt",
    )(raw, jnp.pad(dt_bias.reshape(1, -1), ((0, 0), (0, pad))), jnp.pad(a_log.reshape(1, -1), ((0, 0), (0, pad))))


def _pieces(a):
    out = []
    for _ in range(3):
        piece = a.astype(BF16)
        out.append(piece)
        a = a - piece.astype(F32)
    return out


def _ssd_scan_kernel(*refs, direction, reverse, finalize, chunks):
    if finalize:
        (x_ref, b_ref, c_ref, dac_ref, dtr_ref, dar_ref, h0_ref, yp_ref, z_ref, dsk_ref, nw_ref,
         y_ref, hn_ref, s_ref, yacc_ref) = refs
    else:
        x_ref, b_ref, c_ref, dac_ref, dtr_ref, dar_ref, h0_ref, y_ref, hn_ref, s_ref = refs
        yacc_ref = y_ref
    step = pl.program_id(0)
    nsteps = pl.num_programs(0)
    t_len = SSD_CHUNK

    @pl.when(step == 0)
    def _():
        s_ref[...] = h0_ref[...]

    row = lax.broadcasted_iota(jnp.int32, (t_len, t_len), 0)
    col = lax.broadcasted_iota(jnp.int32, (t_len, t_len), 1)
    mask = (col >= row) if reverse else (col <= row)
    mask_t = (row >= col) if reverse else (row <= col)
    ones = jnp.ones((t_len, t_len), BF16)
    tri = jnp.where(mask, 1.0, 0.0).astype(BF16)
    tri_t = jnp.where(mask_t, 1.0, 0.0).astype(BF16)
    hsl = slice(direction * SSD_HEADS, (direction + 1) * SSD_HEADS)
    lane = lax.broadcasted_iota(jnp.int32, (t_len, LANE), 1)
    first = lane < SSD_HEAD_DIM
    edge = slice(0, 1) if reverse else slice(t_len - 1, t_len)

    for u in range(chunks):
        q = chunks - 1 - u if reverse else u
        rs = slice(q * t_len, (q + 1) * t_len)
        _ssd_chunk(x_ref, b_ref, c_ref, dac_ref, dtr_ref, dar_ref, s_ref, yacc_ref, rs, hsl, direction,
                   mask, tri, tri_t, ones, first, edge)

    @pl.when(step == nsteps - 1)
    def _():
        hn_ref[...] = s_ref[...]

    if finalize:
        y = yacc_ref[...] + yp_ref[...] + x_ref[...].astype(F32) * dsk_ref[...]
        zz = z_ref[...].astype(F32)
        gated = y * (zz * jax.nn.sigmoid(zz))
        out = gated * lax.rsqrt(jnp.mean(gated * gated, axis=-1, keepdims=True) + RMS_EPS) * nw_ref[...]
        y_ref[...] = out.astype(y_ref.dtype)


def _ssd_chunk(x_ref, b_ref, c_ref, dac_ref, dtr_ref, dar_ref, s_ref, yacc_ref, rs, hsl, direction,
               mask, tri, tri_t, ones, first, edge):
    t_len = SSD_CHUNK
    g_col = sum(jnp.dot(tri, piece, preferred_element_type=F32) for piece in _pieces(dac_ref[rs, :]))
    da_row_p = _pieces(dar_ref[hsl, rs])
    g_row = sum(jnp.dot(piece, tri_t, preferred_element_type=F32) for piece in da_row_p)
    tot_row = sum(jnp.dot(piece, ones, preferred_element_type=F32) for piece in da_row_p)
    dt_row = dtr_ref[hsl, rs]
    wdt_row = jnp.exp(tot_row - g_row) * dt_row

    for grp in range(SSD_GROUPS):
        nsl = slice(grp * SSD_STATE, (grp + 1) * SSD_STATE)
        b_g, c_g = b_ref[rs, nsl], c_ref[rs, nsl]
        cb = lax.dot_general(c_g, b_g, _NT, preferred_element_type=F32)
        bt_g = b_g.astype(F32).T
        for k in range(grp * SSD_HPG // 2, (grp + 1) * SSD_HPG // 2):
            ms, es, bws = [], [], []
            for h in (2 * k, 2 * k + 1):
                j = direction * SSD_HEADS + h
                g_bc = jnp.broadcast_to(g_col[:, j:j + 1], (t_len, t_len))
                decay = jnp.where(mask, jnp.exp(g_bc - g_row[h:h + 1, :]), 0.0)
                ms.append((decay * cb * dt_row[h:h + 1, :]).astype(BF16))
                es.append(jnp.exp(g_bc))
                bws.append((bt_g * wdt_row[h:h + 1, :]).astype(BF16))
            csl = slice(k * LANE, (k + 1) * LANE)
            xp = x_ref[rs, csl]
            xa = jnp.where(first, xp, jnp.zeros_like(xp))
            xb = jnp.where(first, jnp.zeros_like(xp), xp)
            e_pair = jnp.where(first, es[0], es[1])
            s_old = s_ref[k]
            y = (jnp.dot(ms[0], xa, preferred_element_type=F32) + jnp.dot(ms[1], xb, preferred_element_type=F32)
                 + jnp.dot(c_g, s_old.astype(BF16), preferred_element_type=F32) * e_pair)
            s_ref[k] = (s_old * e_pair[edge] + jnp.dot(bws[0], xa, preferred_element_type=F32)
                        + jnp.dot(bws[1], xb, preferred_element_type=F32))
            yacc_ref[rs, csl] = y


def _ssd_scan(xbc, da_col, dt_row, da_row, h0, direction, final=None, chunks=4):
    length = xbc.shape[0]
    t_len = SSD_CHUNK
    n_chunks = length // t_len
    chunks = min(chunks, n_chunks)
    n_steps = n_chunks // chunks
    rows = chunks * t_len
    reverse = direction == 1
    pos = (lambda c: n_steps - 1 - c) if reverse else (lambda c: c)
    di, gn = SSD_D_INNER, SSD_GN
    state = jax.ShapeDtypeStruct((SSD_PAIRS, SSD_STATE, LANE), F32)
    state_spec = pl.BlockSpec((SSD_PAIRS, SSD_STATE, LANE), lambda c: (0, 0, 0))
    in_specs = [pl.BlockSpec((rows, di), lambda c: (pos(c), 0)),
                pl.BlockSpec((rows, gn), lambda c: (pos(c), di // gn)),
                pl.BlockSpec((rows, gn), lambda c: (pos(c), di // gn + 1)),
                pl.BlockSpec((rows, LANE), lambda c: (pos(c), 0)),
                pl.BlockSpec((2 * SSD_HEADS, rows), lambda c: (0, pos(c))),
                pl.BlockSpec((2 * SSD_HEADS, rows), lambda c: (0, pos(c))),
                state_spec]
    args = [xbc, xbc, xbc, da_col, dt_row, da_row, h0]
    scratch = [pltpu.VMEM((SSD_PAIRS, SSD_STATE, LANE), F32)]
    y_spec = pl.BlockSpec((rows, di), lambda c: (pos(c), 0))
    row_spec = pl.BlockSpec((1, di), lambda c: (0, 0))
    if final is not None:
        y_other, p, d_skip_row, norm_w_row = final
        in_specs += [y_spec, y_spec, row_spec, row_spec]
        args += [y_other, p, d_skip_row, norm_w_row]
        scratch.append(pltpu.VMEM((rows, di), F32))
    y_dtype = BF16 if final is not None else F32
    kern = functools.partial(_ssd_scan_kernel, direction=direction, reverse=reverse, finalize=final is not None,
                             chunks=chunks)
    return pl.pallas_call(
        kern,
        grid=(n_steps,),
        in_specs=in_specs,
        out_specs=[y_spec, state_spec],
        out_shape=[jax.ShapeDtypeStruct((length, di), y_dtype), state],
        scratch_shapes=scratch,
        compiler_params=_params("arbitrary"),
        name="ssd_scan",
    )(*args)


def _ssd_inputs(p, dt_raw, conv_w, conv_b, dt_bias, a_log):
    xbc = _ssd_conv(p, P_XBC // SSD_CONV_BLK, conv_w, conv_b)
    dt, da = _ssd_dt(dt_raw, dt_bias, a_log)
    nh = 2 * SSD_HEADS
    return xbc, da, dt[:, :nh].T, da[:, :nh].T


def _ssd_pallas(p_l, dt_raw_l, p_c, dt_raw_c, conv_w, conv_b, dt_bias, a_log, d_skip, norm_w, ctx_out):
    xbc_l, dac_l, dtr_l, dar_l = _ssd_inputs(p_l, dt_raw_l, conv_w, conv_b, dt_bias, a_log)
    xbc_c, dac_c, dtr_c, dar_c = _ssd_inputs(p_c, dt_raw_c, conv_w, conv_b, dt_bias, a_log)
    zero = jnp.zeros((SSD_PAIRS, SSD_STATE, LANE), F32)
    fin = (jnp.repeat(d_skip, SSD_HEAD_DIM).reshape(1, -1), norm_w.reshape(1, -1))
    yc0, h_fwd = _ssd_scan(xbc_c, dac_c, dtr_c, dar_c, zero, 0)
    if ctx_out:
        out_c, h_bwd = _ssd_scan(xbc_c, dac_c, dtr_c, dar_c, zero, 1, final=(yc0, p_c) + fin)
    else:
        out_c = None
        _, h_bwd = _ssd_scan(xbc_c, dac_c, dtr_c, dar_c, zero, 1)
    yl0, _ = _ssd_scan(xbc_l, dac_l, dtr_l, dar_l, h_fwd, 0)
    out_l, _ = _ssd_scan(xbc_l, dac_l, dtr_l, dar_l, h_bwd, 1, final=(yl0, p_l) + fin)
    return out_l, out_c


P_Z, P_Q, P_POOL, P_XBC = OFF_Z, OFF_Q, OFF_POOL, OFF_XBC
P_COLS = OFF_DT
KV_K, KV_V = 0, NA_WIDTH


def _layer(layer, x, ctx, mod_l, mod_c, norm1_w, w_in_all, w_kv_all, w_dt_all, ssd_conv_w, ssd_conv_b, ssd_dt_bias,
           ssd_a_log, ssd_d, ssd_norm_w, na_q_norm_w, na_k_norm_w, na_rpb, pool_w, pool_scale, w_out_all, norm2_w,
           w_up_all, mlp_conv_w, mlp_conv_b, w_down_all, rope_l, rope_c, ctx_out):
    sh1_l, sc1_l, g1_l, sh2_l, sc2_l, g2_l = jnp.split(mod_l, N_MOD)
    sh1_c, sc1_c, g1_c, sh2_c, sc2_c, g2_c = jnp.split(mod_c, N_MOD)
    conv_tab = jnp.concatenate([mlp_conv_w, mlp_conv_b[None], jnp.zeros((4, 2 * D_FF), F32)], axis=0)

    h_l = _norm_mod(x, norm1_w, sc1_l, sh1_l)
    h_c = _norm_mod(ctx, norm1_w, sc1_c, sh1_c)
    p_l = _matmul(h_l, w_in_all, layer, BF16)
    kv_l = _matmul(h_l, w_kv_all, layer, BF16)
    dt_l = _matmul(h_l, w_dt_all, layer, F32)
    p_c = _matmul(h_c, w_in_all, layer, BF16)
    kv_c = _matmul(h_c, w_kv_all, layer, BF16)
    dt_c = _matmul(h_c, w_dt_all, layer, F32)

    y_ssd_l, y_ssd_c = _ssd_pallas(p_l, dt_l, p_c, dt_c, ssd_conv_w, ssd_conv_b, ssd_dt_bias, ssd_a_log, ssd_d,
                                   ssd_norm_w, ctx_out)
    q_blk, k_blk, v_blk = P_Q // NA_WIDTH, KV_K // NA_WIDTH, KV_V // NA_HEAD_DIM
    q_r, k_r = _qk_prep(p_l, q_blk, kv_l, k_blk, na_q_norm_w, na_k_norm_w, rope_l)
    qc_r, kc_r = _qk_prep(p_c, q_blk, kv_c, k_blk, na_q_norm_w, na_k_norm_w, rope_c)
    y_na_l = _na_latent(q_r, k_r, kv_l, v_blk, kc_r, kv_c, v_blk, _na_bias_table(na_rpb))
    y_pool_l = _pool(p_l, P_POOL // POOL_WIDTH, pool_w, pool_scale)

    x_mid = _outproj([y_ssd_l, y_na_l, y_pool_l], w_out_all, layer, x, g1_l)
    h2_l = _norm_mod(x_mid, norm2_w, sc2_l, sh2_l)
    x_new = _conv_ffn(h2_l, w_up_all, conv_tab, w_down_all, layer, x_mid, g2_l)
    ctx_new = ctx
    if ctx_out:
        y_na_c = _na_context(qc_r, kc_r, kv_c, v_blk)
        y_pool_c = _pool(p_c, P_POOL // POOL_WIDTH, pool_w, pool_scale)
        c_mid = _outproj([y_ssd_c, y_na_c, y_pool_c], w_out_all, layer, ctx, g1_c)
        h2_c = _norm_mod(c_mid, norm2_w, sc2_c, sh2_c)
        ctx_new = _conv_ffn(h2_c, w_up_all, conv_tab, w_down_all, layer, c_mid, g2_c)
    return x_new, ctx_new


def kernel(x, c, ctx, c_ctx, ada_w, ada_b, norm1_w, w_in, ssd_conv_w, ssd_conv_b, ssd_dt_bias, ssd_a_log,
           ssd_d, ssd_norm_w, na_q_norm_w, na_k_norm_w, na_rpb, pool_w, pool_scale, w_out, norm2_w,
           mlp_w_up, mlp_conv_w, mlp_conv_b, mlp_w_down):
    depth = ada_w.shape[0]
    d = x.shape[-1]
    c_rows = jnp.concatenate([c.reshape(1, d), c_ctx.reshape(1, d), jnp.zeros((MOD_ROWS - 2, d), F32)], axis=0)
    mods = _modulation(c_rows, ada_w, ada_b)
    xs, cs = x[0], ctx[0]
    rope_l = _rope_tables(xs.shape[0])
    ctx_tab = (cs.shape[0], NA_HEAD_DIM)
    rope_c = (jnp.ones(ctx_tab, F32), jnp.zeros(ctx_tab, F32), jnp.zeros(ctx_tab, F32))
    w_in_all = w_in[:, :, :P_COLS].astype(BF16)
    w_kv_all = w_in[:, :, OFF_K:].astype(BF16)
    w_dt_all = jnp.pad(w_in[:, :, OFF_DT:OFF_K], ((0, 0), (0, 0), (0, LANE - 2 * SSD_HEADS))).astype(BF16)
    w_out_all = w_out.astype(BF16)
    w_up_all = mlp_w_up.astype(BF16)
    for i in range(depth):
        xs, cs = _layer(i, xs, cs, mods[i, 0], mods[i, 1], norm1_w[i], w_in_all, w_kv_all, w_dt_all, ssd_conv_w[i],
                        ssd_conv_b[i], ssd_dt_bias[i], ssd_a_log[i], ssd_d[i], ssd_norm_w[i], na_q_norm_w[i],
                        na_k_norm_w[i], na_rpb[i], pool_w[i], pool_scale[i], w_out_all, norm2_w[i], w_up_all,
                        mlp_conv_w[i], mlp_conv_b[i], mlp_w_down, rope_l, rope_c, ctx_out=(i < depth - 1))
    return xs[None]
```

```python
import functools
import math

import jax
import jax.numpy as jnp
from jax import lax
from jax.experimental import pallas as pl
from jax.experimental.pallas import tpu as pltpu

F32 = jnp.float32
BF16 = jnp.bfloat16

D_MODEL = 4096
DEPTH = 2
GRID_W = 64
D_MIX = D_MODEL
SSD_D_INNER = D_MIX // 2
SSD_HEAD_DIM = 64
SSD_HEADS = SSD_D_INNER // SSD_HEAD_DIM
SSD_GROUPS = 4
SSD_HPG = SSD_HEADS // SSD_GROUPS
SSD_STATE = 128
SSD_GN = SSD_GROUPS * SSD_STATE
SSD_XBC = SSD_D_INNER + 2 * SSD_GN
SSD_CONV = 5
SSD_CHUNK = 128
NA_WIDTH = D_MIX // 4
NA_HEAD_DIM = 128
NA_HEADS = NA_WIDTH // NA_HEAD_DIM
NA_WIN_R = 8
NA_WIN_C = 16
ROPE_BASE = 10000.0
POOL_WIDTH = D_MIX - SSD_D_INNER - NA_WIDTH
POOL_WINDOWS = (2, 4, 8, 16)
POOL_GROUP = POOL_WIDTH // len(POOL_WINDOWS)
D_FF = 11008
N_MOD = 6
RMS_EPS = 1e-6
OFF_Z = 0
OFF_Q = OFF_Z + SSD_D_INNER
OFF_POOL = OFF_Q + NA_WIDTH
OFF_XBC = OFF_POOL + POOL_WIDTH
OFF_DT = OFF_XBC + SSD_XBC
OFF_K = OFF_DT + 2 * SSD_HEADS
OFF_V = OFF_K + NA_WIDTH
IN_COLS = OFF_V + NA_WIDTH

VMEM_LIMIT_BYTES = 58 * 1024 * 1024
LANE = 128
BF16_SUBLANE = 16
MOD_ROWS = 8


def _params(*sem):
    return pltpu.CompilerParams(dimension_semantics=sem, vmem_limit_bytes=VMEM_LIMIT_BYTES)


def _mod_kernel(c_ref, w_ref, b_ref, o_ref):
    c = c_ref[...]
    s = (c * jax.nn.sigmoid(c)).astype(BF16)
    o_ref[0] = jnp.dot(s, w_ref[0].astype(BF16), preferred_element_type=F32) + b_ref[0]


def _modulation(c_rows, ada_w, ada_b, tn=1024):
    depth, d, n = ada_w.shape
    return pl.pallas_call(
        _mod_kernel,
        grid=(depth, n // tn),
        in_specs=[pl.BlockSpec((MOD_ROWS, d), lambda l, j: (0, 0)),
                  pl.BlockSpec((1, d, tn), lambda l, j: (l, 0, j)),
                  pl.BlockSpec((1, 1, tn), lambda l, j: (l, 0, j))],
        out_specs=pl.BlockSpec((1, MOD_ROWS, tn), lambda l, j: (l, 0, j)),
        out_shape=jax.ShapeDtypeStruct((depth, MOD_ROWS, n), F32),
        compiler_params=_params("arbitrary", "arbitrary"),
        name="modulation",
    )(c_rows, ada_w, ada_b.reshape(depth, 1, n))


def _norm_kernel(x_ref, w_ref, sc_ref, sh_ref, o_ref):
    x = x_ref[...]
    y = x * lax.rsqrt(jnp.mean(x * x, axis=-1, keepdims=True) + RMS_EPS) * w_ref[...]
    o_ref[...] = (y * (1.0 + sc_ref[...]) + sh_ref[...]).astype(o_ref.dtype)


def _norm_mod(x, w, scale, shift, tm=512):
    m, d = x.shape
    tm = min(tm, m)
    row = pl.BlockSpec((1, d), lambda i: (0, 0))
    return pl.pallas_call(
        _norm_kernel,
        grid=(m // tm,),
        in_specs=[pl.BlockSpec((tm, d), lambda i: (i, 0)), row, row, row],
        out_specs=pl.BlockSpec((tm, d), lambda i: (i, 0)),
        out_shape=jax.ShapeDtypeStruct((m, d), BF16),
        compiler_params=_params("arbitrary"),
        name="norm_mod",
    )(x, w.reshape(1, d), scale.reshape(1, d), shift.reshape(1, d))


def _mm_kernel(a_ref, b_ref, o_ref):
    o_ref[...] = jnp.dot(a_ref[...], b_ref[0], preferred_element_type=F32).astype(o_ref.dtype)


def _matmul(a, b_all, layer, out_dtype, n_cols=None, tm=1024, tn=1024):
    m, k = a.shape
    n = b_all.shape[2] if n_cols is None else n_cols
    tm, tn = min(tm, m), min(tn, n)
    assert n % tn == 0
    return pl.pallas_call(
        _mm_kernel,
        grid=(m // tm, n // tn),
        in_specs=[pl.BlockSpec((tm, k), lambda i, j: (i, 0)),
                  pl.BlockSpec((1, k, tn), lambda i, j: (layer, 0, j))],
        out_specs=pl.BlockSpec((tm, tn), lambda i, j: (i, j)),
        out_shape=jax.ShapeDtypeStruct((m, n), out_dtype),
        compiler_params=_params("arbitrary", "arbitrary"),
        name="matmul",
    )(a, b_all)


def _outproj_kernel(*refs, widths):
    a_refs, (w_ref, x_ref, g_ref, o_ref) = refs[:len(widths)], refs[len(widths):]
    acc, off = None, 0
    for a_ref, k in zip(a_refs, widths):
        part = jnp.dot(a_ref[...], w_ref[0, off:off + k, :], preferred_element_type=F32)
        acc = part if acc is None else acc + part
        off += k
    o_ref[...] = x_ref[...] + g_ref[...] * acc


def _outproj(parts, w_all, layer, x, gate, tm=1024, tn=1024):
    m = x.shape[0]
    _, k, n = w_all.shape
    tm = min(tm, m)
    widths = tuple(a.shape[1] for a in parts)
    assert sum(widths) == k
    return pl.pallas_call(
        functools.partial(_outproj_kernel, widths=widths),
        grid=(m // tm, n // tn),
        in_specs=[pl.BlockSpec((tm, kw), lambda i, j: (i, 0)) for kw in widths] + [
            pl.BlockSpec((1, k, tn), lambda i, j: (layer, 0, j)),
            pl.BlockSpec((tm, tn), lambda i, j: (i, j)),
            pl.BlockSpec((1, tn), lambda i, j: (0, j))],
        out_specs=pl.BlockSpec((tm, tn), lambda i, j: (i, j)),
        out_shape=jax.ShapeDtypeStruct((m, n), F32),
        compiler_params=_params("arbitrary", "arbitrary"),
        name="outproj",
    )(*parts, w_all, x, gate.reshape(1, n))


FFN_HALO = BF16_SUBLANE


def _ffn_kernel(h_ref, hp_ref, hn_ref, wg_ref, wv_ref, tab_ref, wd_ref, x_ref, g_ref,
                o_ref, ext_ref, ua_ref, ub_ref, *, tm, n_mtiles, nf, tn_d):
    i = pl.program_id(0)
    f = pl.program_id(1)
    d = o_ref.shape[1]
    u_refs = (ua_ref, ub_ref)

    @pl.when(f == 0)
    def _():
        ext_ref[0:tm, :] = h_ref[...]
        half = FFN_HALO // 2
        nxt = hn_ref[...].astype(F32)[0:half]
        prv = hp_ref[...].astype(F32)[half:FFN_HALO]
        nxt = jnp.where(i == n_mtiles - 1, jnp.zeros_like(nxt), nxt)
        prv = jnp.where(i == 0, jnp.zeros_like(prv), prv)
        ext_ref[tm:tm + FFN_HALO, :] = jnp.concatenate([nxt, prv], axis=0).astype(BF16)

    def conv(u_ref, br, c):
        um = u_ref[br, 0:tm]
        row = lax.broadcasted_iota(jnp.int32, um.shape, 0)
        before = tm + FFN_HALO - 1
        up = jnp.where(row == 0, u_ref[br, before:before + 1], pltpu.roll(um, 1, 0))
        un = jnp.where(row == tm - 1, u_ref[br, tm:tm + 1], pltpu.roll(um, tm - 1, 0))
        return up * c[0:1] + um * c[1:2] + un * c[2:3] + c[3:4]

    def produce(slot):
        h = ext_ref[...]
        u_refs[slot][0] = jnp.dot(h, wg_ref[0], preferred_element_type=F32)
        u_refs[slot][1] = jnp.dot(h, wv_ref[0], preferred_element_type=F32)

    def consume(slot, first):
        gate = conv(u_refs[slot], 0, tab_ref[f - 1])
        val = conv(u_refs[slot], 1, tab_ref[f - 1 + nf])
        act = (gate * jax.nn.sigmoid(gate) * val).astype(BF16)
        for n in range(d // tn_d):
            sl = slice(n * tn_d, (n + 1) * tn_d)
            part = jnp.dot(act, wd_ref[0, :, sl].astype(BF16), preferred_element_type=F32)
            if first:
                o_ref[:, sl] = part
            else:
                o_ref[:, sl] += part

    @pl.when(f == 0)
    def _():
        produce(0)

    @pl.when(f == 1)
    def _():
        produce(1)
        consume(0, first=True)

    for parity in (0, 1):
        @pl.when((f >= 2) & (f < nf) & (f % 2 == parity))
        def _():
            produce(parity)
            consume(1 - parity, first=False)

    @pl.when(f == nf)
    def _():
        consume((nf - 1) % 2, first=False)
        o_ref[...] = x_ref[...] + g_ref[...] * o_ref[...]


def _conv_ffn(h, w_up_all, conv_tab, w_down_all, layer, x, gate, tm=512, tf=256, tn_d=1024):
    m, d = h.shape
    ff = w_down_all.shape[1]
    tm = min(tm, m)
    tn_d = min(tn_d, d)
    n_mtiles, nf = m // tm, ff // tf
    assert nf >= 2
    hb = tm // FFN_HALO
    last_hb = m // FFN_HALO - 1
    kern = functools.partial(_ffn_kernel, tm=tm, n_mtiles=n_mtiles, nf=nf, tn_d=tn_d)
    tab = conv_tab.reshape(8, 2 * nf, tf).transpose(1, 0, 2)

    def prod(f):
        return jnp.minimum(f, nf - 1)

    def cons(f):
        return jnp.maximum(f - 1, 0)

    return pl.pallas_call(
        kern,
        grid=(n_mtiles, nf + 1),
        in_specs=[
            pl.BlockSpec((tm, d), lambda i, f: (i, 0), pipeline_mode=pl.Buffered(1)),
            pl.BlockSpec((FFN_HALO, d), lambda i, f: (jnp.maximum(i * hb - 1, 0), 0)),
            pl.BlockSpec((FFN_HALO, d), lambda i, f: (jnp.minimum((i + 1) * hb, last_hb), 0)),
            pl.BlockSpec((1, d, tf), lambda i, f: (layer, 0, prod(f))),
            pl.BlockSpec((1, d, tf), lambda i, f: (layer, 0, prod(f) + nf)),
            pl.BlockSpec((2 * nf, 8, tf), lambda i, f: (0, 0, 0)),
            pl.BlockSpec((1, tf, d), lambda i, f: (layer, cons(f), 0)),
            pl.BlockSpec((tm, d), lambda i, f: (i, 0), pipeline_mode=pl.Buffered(1)),
            pl.BlockSpec((1, d), lambda i, f: (0, 0)),
        ],
        out_specs=pl.BlockSpec((tm, d), lambda i, f: (i, 0)),
        out_shape=jax.ShapeDtypeStruct((m, d), F32),
        scratch_shapes=[pltpu.VMEM((tm + FFN_HALO, d), BF16),
                        pltpu.VMEM((2, tm + FFN_HALO, tf), F32),
                        pltpu.VMEM((2, tm + FFN_HALO, tf), F32)],
        compiler_params=_params("arbitrary", "arbitrary"),
        name="conv_ffn",
    )(h, h, h, w_up_all, w_up_all, tab, w_down_all, x, gate.reshape(1, d))


def _rope_tables(length):
    t = jnp.arange(length)
    half = NA_HEAD_DIM // 2
    inv = ROPE_BASE ** (-jnp.arange(0, half, 2, dtype=F32) / half)
    ar = (t // GRID_W).astype(F32)[:, None] * inv
    ac = (t % GRID_W).astype(F32)[:, None] * inv
    ang = jnp.concatenate([ar, ar, ac, ac], axis=-1)
    cos, sin = jnp.cos(ang), jnp.sin(ang)
    first = (jnp.arange(NA_HEAD_DIM) % half) < half // 2
    return cos, jnp.where(first, -sin, 0.0), jnp.where(first, 0.0, sin)


def _qk_prep_kernel(q_ref, k_ref, qw_ref, kw_ref, cos_ref, sa_ref, sb_ref, qo_ref, ko_ref):
    cos, sa, sb = cos_ref[...], sa_ref[...], sb_ref[...]
    quarter = NA_HEAD_DIM // 4
    for h in range(NA_HEADS):
        sl = slice(h * NA_HEAD_DIM, (h + 1) * NA_HEAD_DIM)
        for src, w_ref, dst in ((q_ref, qw_ref, qo_ref), (k_ref, kw_ref, ko_ref)):
            x = src[:, sl].astype(F32)
            y = x * lax.rsqrt(jnp.mean(x * x, axis=-1, keepdims=True) + RMS_EPS) * w_ref[...]
            y = (y * cos + pltpu.roll(y, NA_HEAD_DIM - quarter, 1) * sa + pltpu.roll(y, quarter, 1) * sb)
            dst[:, sl] = y.astype(dst.dtype)


def _qk_prep(p_q, q_blk, p_k, k_blk, qw, kw, tables, tm=512):
    m = p_q.shape[0]
    tm = min(tm, m)
    tab = pl.BlockSpec((tm, NA_HEAD_DIM), lambda i: (i, 0))
    vec = pl.BlockSpec((1, NA_HEAD_DIM), lambda i: (0, 0))
    out = jax.ShapeDtypeStruct((m, NA_WIDTH), BF16)
    return pl.pallas_call(
        _qk_prep_kernel,
        grid=(m // tm,),
        in_specs=[pl.BlockSpec((tm, NA_WIDTH), lambda i: (i, q_blk)),
                  pl.BlockSpec((tm, NA_WIDTH), lambda i: (i, k_blk)), vec, vec, tab, tab, tab],
        out_specs=[pl.BlockSpec((tm, NA_WIDTH), lambda i: (i, 0))] * 2,
        out_shape=[out, out],
        compiler_params=_params("arbitrary"),
        name="qk_prep",
    )(p_q, p_k, qw.reshape(1, -1), kw.reshape(1, -1), *tables)


NA_DR = 2 * NA_WIN_R
NA_BIAS_CHUNK = 1024


def _na_bias_kernel(r0_ref, r1_ref, o_ref):
    def pieces(r):
        out = []
        for _ in range(3):
            piece = r.astype(BF16)
            out.append(piece)
            r = r - piece.astype(F32)
        return out

    p0, p1 = pieces(r0_ref[0]), pieces(r1_ref[0])
    for c in range(GRID_W * LANE // NA_BIAS_CHUNK):
        shape = (LANE, NA_BIAS_CHUNK)
        pos = c * NA_BIAS_CHUNK + lax.broadcasted_iota(jnp.int32, shape, 1)
        j = lax.broadcasted_iota(jnp.int32, shape, 0)
        qc = pos // LANE
        lane = pos % LANE
        kc = lane % GRID_W
        idx = jnp.clip(kc - qc + (NA_WIN_C - 1), 0, 2 * NA_WIN_C - 2)
        hit = idx == j
        second = lane >= GRID_W
        oh0 = jnp.where(hit, jnp.where(second, 0.0, 1.0), 0.0).astype(BF16)
        oh1 = jnp.where(hit, jnp.where(second, 1.0, 0.0), 0.0).astype(BF16)
        acc = jnp.zeros((NA_DR, NA_BIAS_CHUNK), F32)
        for a, b in zip(p0, p1):
            acc = acc + jnp.dot(a, oh0, preferred_element_type=F32) + jnp.dot(b, oh1, preferred_element_type=F32)
        c0 = jnp.clip(qc[0:1] - NA_WIN_C // 2, 0, GRID_W - NA_WIN_C)
        ok = (kc[0:1] >= c0) & (kc[0:1] < c0 + NA_WIN_C)
        o_ref[0, :, c * NA_BIAS_CHUNK:(c + 1) * NA_BIAS_CHUNK] = jnp.where(ok, acc, -jnp.inf)


def _na_bias_table(rpb):
    nh, ndr, ndc = rpb.shape
    r0 = jnp.pad(rpb, ((0, 0), (0, NA_DR - ndr), (0, LANE - ndc)))
    r1 = jnp.pad(rpb[:, 1:], ((0, 0), (0, NA_DR - ndr + 1), (0, LANE - ndc)))
    spec = pl.BlockSpec((1, NA_DR, LANE), lambda h: (h, 0, 0))
    out = pl.pallas_call(
        _na_bias_kernel,
        grid=(nh,),
        in_specs=[spec, spec],
        out_specs=pl.BlockSpec((1, NA_DR, GRID_W * LANE), lambda h: (h, 0, 0)),
        out_shape=jax.ShapeDtypeStruct((nh, NA_DR, GRID_W * LANE), F32),
        compiler_params=_params("arbitrary"),
        name="na_bias",
    )(r0, r1)
    return out.reshape(nh, NA_DR, GRID_W, LANE)


def _softmax_parts(scores):
    m = functools.reduce(jnp.maximum, [jnp.max(s, axis=1, keepdims=True) for s in scores])
    ps = [jnp.exp(s - m) for s in scores]
    den = functools.reduce(jnp.add, [jnp.sum(p, axis=1, keepdims=True) for p in ps])
    return [p.astype(BF16) for p in ps], den


def _pv(parts, values):
    ps, den = parts
    o = functools.reduce(jnp.add, [jnp.dot(p, v, preferred_element_type=F32) for p, v in zip(ps, values)])
    return o / den


def _softmax_pv(scores, values):
    return _pv(_softmax_parts(scores), values)


_NT = (((1,), (1,)), ((), ()))


def _na_kernel(q_ref, k_ref, v_ref, kc_ref, vc_ref, tz_ref, o_ref, *, rows_per_step, n_rows):
    t = pl.program_id(1)
    scale = NA_HEAD_DIM ** -0.5
    band = NA_WIN_R * GRID_W
    kc, vc = kc_ref[...], vc_ref[...]
    starts, scores = [], []
    for i in range(rows_per_step):
        qr = t * rows_per_step + i
        bs = jnp.clip(qr - NA_WIN_R // 2, 0, n_rows - NA_WIN_R)
        off = bs - qr + (NA_WIN_R - 1)
        start = pl.multiple_of(bs * GRID_W, GRID_W)
        q = q_ref[i * GRID_W:(i + 1) * GRID_W, :]
        kb = k_ref[pl.ds(start, band), :]
        bias = jnp.concatenate([tz_ref[0, off + 2 * w] for w in range(NA_WIN_R // 2)], axis=1)
        s_loc = lax.dot_general(q, kb, _NT, preferred_element_type=F32) * scale + bias
        s_ctx = lax.dot_general(q, kc, _NT, preferred_element_type=F32) * scale
        starts.append(start)
        scores.append((s_loc, s_ctx))
    probs = [_softmax_parts(list(s)) for s in scores]
    for i in range(rows_per_step):
        vb = v_ref[pl.ds(starts[i], band), :]
        o = _pv(probs[i], [vb, vc])
        o_ref[i * GRID_W:(i + 1) * GRID_W, :] = o.astype(o_ref.dtype)


def _na_latent(q_r, k_r, p_l, v_blk, kc_r, p_c, vc_blk, tz, rows_per_step=16):
    length = q_r.shape[0]
    lc = kc_r.shape[0]
    n_rows = length // GRID_W
    assert n_rows >= NA_WIN_R and n_rows % rows_per_step == 0
    hd = NA_HEAD_DIM
    kern = functools.partial(_na_kernel, rows_per_step=rows_per_step, n_rows=n_rows)
    return pl.pallas_call(
        kern,
        grid=(NA_HEADS, n_rows // rows_per_step),
        in_specs=[pl.BlockSpec((rows_per_step * GRID_W, hd), lambda h, t: (t, h)),
                  pl.BlockSpec((length, hd), lambda h, t: (0, h)),
                  pl.BlockSpec((length, hd), lambda h, t: (0, v_blk + h)),
                  pl.BlockSpec((lc, hd), lambda h, t: (0, h)),
                  pl.BlockSpec((lc, hd), lambda h, t: (0, vc_blk + h)),
                  pl.BlockSpec((1, NA_DR, GRID_W, LANE), lambda h, t: (h, 0, 0, 0))],
        out_specs=pl.BlockSpec((rows_per_step * GRID_W, hd), lambda h, t: (t, h)),
        out_shape=jax.ShapeDtypeStruct((length, NA_WIDTH), BF16),
        compiler_params=_params("arbitrary", "arbitrary"),
        name="na_latent",
    )(q_r, k_r, p_l, kc_r, p_c, tz)


def _ctx_attn_kernel(q_ref, k_ref, v_ref, o_ref):
    s = lax.dot_general(q_ref[...], k_ref[...], _NT, preferred_element_type=F32) * (NA_HEAD_DIM ** -0.5)
    o_ref[...] = _softmax_pv([s], [v_ref[...]]).astype(o_ref.dtype)


def _na_context(qc_r, kc_r, p_c, vc_blk):
    lc = qc_r.shape[0]
    hd = NA_HEAD_DIM
    spec = pl.BlockSpec((lc, hd), lambda h: (0, h))
    return pl.pallas_call(
        _ctx_attn_kernel,
        grid=(NA_HEADS,),
        in_specs=[spec, spec, pl.BlockSpec((lc, hd), lambda h: (0, vc_blk + h))],
        out_specs=spec,
        out_shape=jax.ShapeDtypeStruct((lc, NA_WIDTH), BF16),
        compiler_params=_params("arbitrary"),
        name="na_context",
    )(qc_r, kc_r, p_c)


POOL_HALO = BF16_SUBLANE


def _pool_kernel(u_ref, up_ref, un_ref, w_ref, sc_ref, o_ref, ext_ref, *, tm, n_tiles, length):
    i = pl.program_id(0)
    prv, nxt = up_ref[...], un_ref[...]
    ext_ref[0:POOL_HALO, :] = jnp.where(i == 0, jnp.zeros_like(prv), prv).astype(F32)
    ext_ref[POOL_HALO:POOL_HALO + tm, :] = u_ref[...].astype(F32)
    ext_ref[POOL_HALO + tm:POOL_HALO + tm + POOL_HALO, :] = jnp.where(i == n_tiles - 1, jnp.zeros_like(nxt),
                                                                     nxt).astype(F32)
    t = i * tm + lax.broadcasted_iota(jnp.int32, (tm, 1), 0)
    for g, w in enumerate(POOL_WINDOWS):
        cs = slice(g * POOL_GROUP, (g + 1) * POOL_GROUP)
        acc = ext_ref[pl.ds(POOL_HALO - w // 2, tm), cs]
        for k in range(1, w):
            acc = acc + ext_ref[pl.ds(POOL_HALO - w // 2 + k, tm), cs]
        cnt = (jnp.minimum(t + w // 2, length) - jnp.maximum(t - w // 2, 0)).astype(F32)
        pooled = acc / cnt - ext_ref[pl.ds(POOL_HALO, tm), cs]
        y = jnp.dot(pooled.astype(BF16), w_ref[g], preferred_element_type=F32) * sc_ref[:, cs]
        o_ref[:, cs] = y.astype(o_ref.dtype)


def _pool(p, u_blk, pool_w, pool_scale, tm=512):
    length = p.shape[0]
    tm = min(tm, length)
    n_tiles = length // tm
    hb = tm // POOL_HALO
    last_hb = length // POOL_HALO - 1
    kern = functools.partial(_pool_kernel, tm=tm, n_tiles=n_tiles, length=length)
    return pl.pallas_call(
        kern,
        grid=(n_tiles,),
        in_specs=[pl.BlockSpec((tm, POOL_WIDTH), lambda i: (i, u_blk)),
                  pl.BlockSpec((POOL_HALO, POOL_WIDTH), lambda i: (jnp.maximum(i * hb - 1, 0), u_blk)),
                  pl.BlockSpec((POOL_HALO, POOL_WIDTH), lambda i: (jnp.minimum((i + 1) * hb, last_hb), u_blk)),
                  pl.BlockSpec((len(POOL_WINDOWS), POOL_GROUP, POOL_GROUP), lambda i: (0, 0, 0)),
                  pl.BlockSpec((1, POOL_WIDTH), lambda i: (0, 0))],
        out_specs=pl.BlockSpec((tm, POOL_WIDTH), lambda i: (i, 0)),
        out_shape=jax.ShapeDtypeStruct((length, POOL_WIDTH), BF16),
        scratch_shapes=[pltpu.VMEM((tm + 2 * POOL_HALO, POOL_WIDTH), F32)],
        compiler_params=_params("arbitrary"),
        name="pool",
    )(p, p, p, pool_w.astype(BF16), pool_scale.reshape(1, POOL_WIDTH))


SSD_HALO = BF16_SUBLANE
SSD_CONV_BLK = 1024
SSD_PAIRS = SSD_HEADS // 2


def _ssd_conv_kernel(u_ref, up_ref, un_ref, tab_ref, o_ref, ext_ref, *, tm, n_tiles):
    i = pl.program_id(0)
    prv, nxt = up_ref[...], un_ref[...]
    ext_ref[0:SSD_HALO, :] = jnp.where(i == 0, jnp.zeros_like(prv), prv).astype(F32)
    ext_ref[SSD_HALO:SSD_HALO + tm, :] = u_ref[...].astype(F32)
    ext_ref[SSD_HALO + tm:SSD_HALO + tm + SSD_HALO, :] = jnp.where(i == n_tiles - 1, jnp.zeros_like(nxt),
                                                                   nxt).astype(F32)
    tab = tab_ref[...]
    left = (SSD_CONV - 1) // 2
    acc = tab[SSD_CONV:SSD_CONV + 1]
    for k in range(SSD_CONV):
        acc = acc + ext_ref[pl.ds(SSD_HALO - left + k, tm), :] * tab[k:k + 1]
    o_ref[...] = (acc * jax.nn.sigmoid(acc)).astype(o_ref.dtype)


def _ssd_conv(p, blk0, conv_w, conv_b, tm=512):
    length = p.shape[0]
    tm = min(tm, length)
    n_tiles = length // tm
    hb = tm // SSD_HALO
    last_hb = length // SSD_HALO - 1
    cb = SSD_CONV_BLK
    tab = jnp.concatenate([conv_w, conv_b[None], jnp.zeros((8 - SSD_CONV - 1, SSD_XBC), F32)], axis=0)
    kern = functools.partial(_ssd_conv_kernel, tm=tm, n_tiles=n_tiles)
    return pl.pallas_call(
        kern,
        grid=(n_tiles, SSD_XBC // cb),
        in_specs=[pl.BlockSpec((tm, cb), lambda i, j: (i, blk0 + j)),
                  pl.BlockSpec((SSD_HALO, cb), lambda i, j: (jnp.maximum(i * hb - 1, 0), blk0 + j)),
                  pl.BlockSpec((SSD_HALO, cb), lambda i, j: (jnp.minimum((i + 1) * hb, last_hb), blk0 + j)),
                  pl.BlockSpec((8, cb), lambda i, j: (0, j))],
        out_specs=pl.BlockSpec((tm, cb), lambda i, j: (i, j)),
        out_shape=jax.ShapeDtypeStruct((length, SSD_XBC), BF16),
        scratch_shapes=[pltpu.VMEM((tm + 2 * SSD_HALO, cb), F32)],
        compiler_params=_params("arbitrary", "arbitrary"),
        name="ssd_conv",
    )(p, p, p, tab)


def _ssd_dt_kernel(raw_ref, bias_ref, alog_ref, dt_ref, da_ref):
    v = raw_ref[...] + bias_ref[...]
    dt = jnp.maximum(v, 0.0) + jnp.log1p(jnp.exp(-jnp.abs(v)))
    dt_ref[...] = dt
    da_ref[...] = dt * -jnp.exp(alog_ref[...])


def _ssd_dt(raw, dt_bias, a_log, tm=2048):
    length = raw.shape[0]
    tm = min(tm, length)
    pad = LANE - 2 * SSD_HEADS
    spec = pl.BlockSpec((tm, LANE), lambda i: (i, 0))
    vec = pl.BlockSpec((1, LANE), lambda i: (0, 0))
    out = jax.ShapeDtypeStruct((length, LANE), F32)
    return pl.pallas_call(
        _ssd_dt_kernel,
        grid=(length // tm,),
        in_specs=[spec, vec, vec],
        out_specs=[spec, spec],
        out_shape=[out, out],
        compiler_params=_params("arbitrary"),
        name="ssd_dt",
    )(raw, jnp.pad(dt_bias.reshape(1, -1), ((0, 0), (0, pad))), jnp.pad(a_log.reshape(1, -1), ((0, 0), (0, pad))))


def _pieces(a):
    out = []
    for _ in range(3):
        piece = a.astype(BF16)
        out.append(piece)
        a = a - piece.astype(F32)
    return out


def _ssd_scan_kernel(*refs, direction, reverse, finalize, chunks):
    if finalize:
        (x_ref, b_ref, c_ref, dac_ref, dtr_ref, dar_ref, h0_ref, yp_ref, z_ref, dsk_ref, nw_ref,
         y_ref, hn_ref, s_ref, yacc_ref) = refs
    else:
        x_ref, b_ref, c_ref, dac_ref, dtr_ref, dar_ref, h0_ref, y_ref, hn_ref, s_ref = refs
        yacc_ref = y_ref
    step = pl.program_id(0)
    nsteps = pl.num_programs(0)
    t_len = SSD_CHUNK

    @pl.when(step == 0)
    def _():
        s_ref[...] = h0_ref[...]

    row = lax.broadcasted_iota(jnp.int32, (t_len, t_len), 0)
    col = lax.broadcasted_iota(jnp.int32, (t_len, t_len), 1)
    mask = (col >= row) if reverse else (col <= row)
    mask_t = (row >= col) if reverse else (row <= col)
    ones = jnp.ones((t_len, t_len), BF16)
    tri = jnp.where(mask, 1.0, 0.0).astype(BF16)
    tri_t = jnp.where(mask_t, 1.0, 0.0).astype(BF16)
    hsl = slice(direction * SSD_HEADS, (direction + 1) * SSD_HEADS)
    lane = lax.broadcasted_iota(jnp.int32, (t_len, LANE), 1)
    first = lane < SSD_HEAD_DIM
    edge = slice(0, 1) if reverse else slice(t_len - 1, t_len)

    for u in range(chunks):
        q = chunks - 1 - u if reverse else u
        rs = slice(q * t_len, (q + 1) * t_len)
        _ssd_chunk(x_ref, b_ref, c_ref, dac_ref, dtr_ref, dar_ref, s_ref, yacc_ref, rs, hsl, direction,
                   mask, tri, tri_t, ones, first, edge)

    @pl.when(step == nsteps - 1)
    def _():
        hn_ref[...] = s_ref[...]

    if finalize:
        y = yacc_ref[...] + yp_ref[...] + x_ref[...].astype(F32) * dsk_ref[...]
        zz = z_ref[...].astype(F32)
        gated = y * (zz * jax.nn.sigmoid(zz))
        out = gated * lax.rsqrt(jnp.mean(gated * gated, axis=-1, keepdims=True) + RMS_EPS) * nw_ref[...]
        y_ref[...] = out.astype(y_ref.dtype)


def _ssd_chunk(x_ref, b_ref, c_ref, dac_ref, dtr_ref, dar_ref, s_ref, yacc_ref, rs, hsl, direction,
               mask, tri, tri_t, ones, first, edge):
    t_len = SSD_CHUNK
    g_col = sum(jnp.dot(tri, piece, preferred_element_type=F32) for piece in _pieces(dac_ref[rs, :]))
    da_row_p = _pieces(dar_ref[hsl, rs])
    g_row = sum(jnp.dot(piece, tri_t, preferred_element_type=F32) for piece in da_row_p)
    tot_row = sum(jnp.dot(piece, ones, preferred_element_type=F32) for piece in da_row_p)
    dt_row = dtr_ref[hsl, rs]
    wdt_row = jnp.exp(tot_row - g_row) * dt_row

    for grp in range(SSD_GROUPS):
        nsl = slice(grp * SSD_STATE, (grp + 1) * SSD_STATE)
        b_g, c_g = b_ref[rs, nsl], c_ref[rs, nsl]
        cb = lax.dot_general(c_g, b_g, _NT, preferred_element_type=F32)
        bt_g = b_g.astype(F32).T
        for k in range(grp * SSD_HPG // 2, (grp + 1) * SSD_HPG // 2):
            ms, es, bws = [], [], []
            for h in (2 * k, 2 * k + 1):
                j = direction * SSD_HEADS + h
                g_bc = jnp.broadcast_to(g_col[:, j:j + 1], (t_len, t_len))
                decay = jnp.where(mask, jnp.exp(g_bc - g_row[h:h + 1, :]), 0.0)
                ms.append((decay * cb * dt_row[h:h + 1, :]).astype(BF16))
                es.append(jnp.exp(g_bc))
                bws.append((bt_g * wdt_row[h:h + 1, :]).astype(BF16))
            csl = slice(k * LANE, (k + 1) * LANE)
            xp = x_ref[rs, csl]
            xa = jnp.where(first, xp, jnp.zeros_like(xp))
            xb = jnp.where(first, jnp.zeros_like(xp), xp)
            e_pair = jnp.where(first, es[0], es[1])
            s_old = s_ref[k]
            y = (jnp.dot(ms[0], xa, preferred_element_type=F32) + jnp.dot(ms[1], xb, preferred_element_type=F32)
                 + jnp.dot(c_g, s_old.astype(BF16), preferred_element_type=F32) * e_pair)
            s_ref[k] = (s_old * e_pair[edge] + jnp.dot(bws[0], xa, preferred_element_type=F32)
                        + jnp.dot(bws[1], xb, preferred_element_type=F32))
            yacc_ref[rs, csl] = y


def _ssd_scan(xbc, da_col, dt_row, da_row, h0, direction, final=None, chunks=4):
    length = xbc.shape[0]
    t_len = SSD_CHUNK
    n_chunks = length // t_len
    chunks = min(chunks, n_chunks)
    n_steps = n_chunks // chunks
    rows = chunks * t_len
    reverse = direction == 1
    pos = (lambda c: n_steps - 1 - c) if reverse else (lambda c: c)
    di, gn = SSD_D_INNER, SSD_GN
    state = jax.ShapeDtypeStruct((SSD_PAIRS, SSD_STATE, LANE), F32)
    state_spec = pl.BlockSpec((SSD_PAIRS, SSD_STATE, LANE), lambda c: (0, 0, 0))
    in_specs = [pl.BlockSpec((rows, di), lambda c: (pos(c), 0)),
                pl.BlockSpec((rows, gn), lambda c: (pos(c), di // gn)),
                pl.BlockSpec((rows, gn), lambda c: (pos(c), di // gn + 1)),
                pl.BlockSpec((rows, LANE), lambda c: (pos(c), 0)),
                pl.BlockSpec((2 * SSD_HEADS, rows), lambda c: (0, pos(c))),
                pl.BlockSpec((2 * SSD_HEADS, rows), lambda c: (0, pos(c))),
                state_spec]
    args = [xbc, xbc, xbc, da_col, dt_row, da_row, h0]
    scratch = [pltpu.VMEM((SSD_PAIRS, SSD_STATE, LANE), F32)]
    y_spec = pl.BlockSpec((rows, di), lambda c: (pos(c), 0))
    row_spec = pl.BlockSpec((1, di), lambda c: (0, 0))
    if final is not None:
        y_other, p, d_skip_row, norm_w_row = final
        in_specs += [y_spec, y_spec, row_spec, row_spec]
        args += [y_other, p, d_skip_row, norm_w_row]
        scratch.append(pltpu.VMEM((rows, di), F32))
    y_dtype = BF16 if final is not None else F32
    kern = functools.partial(_ssd_scan_kernel, direction=direction, reverse=reverse, finalize=final is not None,
                             chunks=chunks)
    return pl.pallas_call(
        kern,
        grid=(n_steps,),
        in_specs=in_specs,
        out_specs=[y_spec, state_spec],
        out_shape=[jax.ShapeDtypeStruct((length, di), y_dtype), state],
        scratch_shapes=scratch,
        compiler_params=_params("arbitrary"),
        name="ssd_scan",
    )(*args)


def _ssd_inputs(p, dt_raw, conv_w, conv_b, dt_bias, a_log):
    xbc = _ssd_conv(p, P_XBC // SSD_CONV_BLK, conv_w, conv_b)
    dt, da = _ssd_dt(dt_raw, dt_bias, a_log)
    nh = 2 * SSD_HEADS
    return xbc, da, dt[:, :nh].T, da[:, :nh].T


def _ssd_pallas(p_l, dt_raw_l, p_c, dt_raw_c, conv_w, conv_b, dt_bias, a_log, d_skip, norm_w, ctx_out):
    xbc_l, dac_l, dtr_l, dar_l = _ssd_inputs(p_l, dt_raw_l, conv_w, conv_b, dt_bias, a_log)
    xbc_c, dac_c, dtr_c, dar_c = _ssd_inputs(p_c, dt_raw_c, conv_w, conv_b, dt_bias, a_log)
    zero = jnp.zeros((SSD_PAIRS, SSD_STATE, LANE), F32)
    fin = (jnp.repeat(d_skip, SSD_HEAD_DIM).reshape(1, -1), norm_w.reshape(1, -1))
    yc0, h_fwd = _ssd_scan(xbc_c, dac_c, dtr_c, dar_c, zero, 0)
    if ctx_out:
        out_c, h_bwd = _ssd_scan(xbc_c, dac_c, dtr_c, dar_c, zero, 1, final=(yc0, p_c) + fin)
    else:
        out_c = None
        _, h_bwd = _ssd_scan(xbc_c, dac_c, dtr_c, dar_c, zero, 1)
    yl0, _ = _ssd_scan(xbc_l, dac_l, dtr_l, dar_l, h_fwd, 0)
    out_l, _ = _ssd_scan(xbc_l, dac_l, dtr_l, dar_l, h_bwd, 1, final=(yl0, p_l) + fin)
    return out_l, out_c


P_Z, P_Q, P_POOL, P_XBC = OFF_Z, OFF_Q, OFF_POOL, OFF_XBC
P_COLS = OFF_DT
KV_K, KV_V = 0, NA_WIDTH


def _layer(layer, x, ctx, mod_l, mod_c, norm1_w, w_in_all, w_kv_all, w_dt_all, ssd_conv_w, ssd_conv_b, ssd_dt_bias,
           ssd_a_log, ssd_d, ssd_norm_w, na_q_norm_w, na_k_norm_w, na_rpb, pool_w, pool_scale, w_out_all, norm2_w,
           w_up_all, mlp_conv_w, mlp_conv_b, w_down_all, rope_l, rope_c, ctx_out):
    sh1_l, sc1_l, g1_l, sh2_l, sc2_l, g2_l = jnp.split(mod_l, N_MOD)
    sh1_c, sc1_c, g1_c, sh2_c, sc2_c, g2_c = jnp.split(mod_c, N_MOD)
    conv_tab = jnp.concatenate([mlp_conv_w, mlp_conv_b[None], jnp.zeros((4, 2 * D_FF), F32)], axis=0)

    h_l = _norm_mod(x, norm1_w, sc1_l, sh1_l)
    h_c = _norm_mod(ctx, norm1_w, sc1_c, sh1_c)
    p_l = _matmul(h_l, w_in_all, layer, BF16, n_cols=P_COLS)
    kv_l = _matmul(h_l, w_kv_all, layer, BF16)
    dt_l = _matmul(h_l, w_dt_all, layer, F32)
    p_c = _matmul(h_c, w_in_all, layer, BF16, n_cols=P_COLS)
    kv_c = _matmul(h_c, w_kv_all, layer, BF16)
    dt_c = _matmul(h_c, w_dt_all, layer, F32)

    y_ssd_l, y_ssd_c = _ssd_pallas(p_l, dt_l, p_c, dt_c, ssd_conv_w, ssd_conv_b, ssd_dt_bias, ssd_a_log, ssd_d,
                                   ssd_norm_w, ctx_out)
    q_blk, k_blk, v_blk = P_Q // NA_WIDTH, KV_K // NA_WIDTH, KV_V // NA_HEAD_DIM
    q_r, k_r = _qk_prep(p_l, q_blk, kv_l, k_blk, na_q_norm_w, na_k_norm_w, rope_l)
    qc_r, kc_r = _qk_prep(p_c, q_blk, kv_c, k_blk, na_q_norm_w, na_k_norm_w, rope_c)
    y_na_l = _na_latent(q_r, k_r, kv_l, v_blk, kc_r, kv_c, v_blk, _na_bias_table(na_rpb))
    y_pool_l = _pool(p_l, P_POOL // POOL_WIDTH, pool_w, pool_scale)

    x_mid = _outproj([y_ssd_l, y_na_l, y_pool_l], w_out_all, layer, x, g1_l)
    h2_l = _norm_mod(x_mid, norm2_w, sc2_l, sh2_l)
    x_new = _conv_ffn(h2_l, w_up_all, conv_tab, w_down_all, layer, x_mid, g2_l)
    ctx_new = ctx
    if ctx_out:
        y_na_c = _na_context(qc_r, kc_r, kv_c, v_blk)
        y_pool_c = _pool(p_c, P_POOL // POOL_WIDTH, pool_w, pool_scale)
        c_mid = _outproj([y_ssd_c, y_na_c, y_pool_c], w_out_all, layer, ctx, g1_c)
        h2_c = _norm_mod(c_mid, norm2_w, sc2_c, sh2_c)
        ctx_new = _conv_ffn(h2_c, w_up_all, conv_tab, w_down_all, layer, c_mid, g2_c)
    return x_new, ctx_new


def kernel(x, c, ctx, c_ctx, ada_w, ada_b, norm1_w, w_in, ssd_conv_w, ssd_conv_b, ssd_dt_bias, ssd_a_log,
           ssd_d, ssd_norm_w, na_q_norm_w, na_k_norm_w, na_rpb, pool_w, pool_scale, w_out, norm2_w,
           mlp_w_up, mlp_conv_w, mlp_conv_b, mlp_w_down):
    depth = ada_w.shape[0]
    d = x.shape[-1]
    c_rows = jnp.concatenate([c.reshape(1, d), c_ctx.reshape(1, d), jnp.zeros((MOD_ROWS - 2, d), F32)], axis=0)
    mods = _modulation(c_rows, ada_w, ada_b)
    xs, cs = x[0], ctx[0]
    rope_l = _rope_tables(xs.shape[0])
    ctx_tab = (cs.shape[0], NA_HEAD_DIM)
    rope_c = (jnp.ones(ctx_tab, F32), jnp.zeros(ctx_tab, F32), jnp.zeros(ctx_tab, F32))
    w_in_all = w_in.astype(BF16)
    w_kv_all = w_in[:, :, OFF_K:].astype(BF16)
    w_dt_all = jnp.pad(w_in[:, :, OFF_DT:OFF_K], ((0, 0), (0, 0), (0, LANE - 2 * SSD_HEADS))).astype(BF16)
    w_out_all = w_out.astype(BF16)
    w_up_all = mlp_w_up.astype(BF16)
    for i in range(depth):
        xs, cs = _layer(i, xs, cs, mods[i, 0], mods[i, 1], norm1_w[i], w_in_all, w_kv_all, w_dt_all, ssd_conv_w[i],
                        ssd_conv_b[i], ssd_dt_bias[i], ssd_a_log[i], ssd_d[i], ssd_norm_w[i], na_q_norm_w[i],
                        na_k_norm_w[i], na_rpb[i], pool_w[i], pool_scale[i], w_out_all, norm2_w[i], w_up_all,
                        mlp_conv_w[i], mlp_conv_b[i], mlp_w_down, rope_l, rope_c, ctx_out=(i < depth - 1))
    return xs[None]
```

```python
import functools
import math

import jax
import jax.numpy as jnp
from jax import lax
from jax.experimental import pallas as pl
from jax.experimental.pallas import tpu as pltpu

F32 = jnp.float32
BF16 = jnp.bfloat16

D_MODEL = 4096
DEPTH = 2
GRID_W = 64
D_MIX = D_MODEL
SSD_D_INNER = D_MIX // 2
SSD_HEAD_DIM = 64
SSD_HEADS = SSD_D_INNER // SSD_HEAD_DIM
SSD_GROUPS = 4
SSD_HPG = SSD_HEADS // SSD_GROUPS
SSD_STATE = 128
SSD_GN = SSD_GROUPS * SSD_STATE
SSD_XBC = SSD_D_INNER + 2 * SSD_GN
SSD_CONV = 5
SSD_CHUNK = 128
NA_WIDTH = D_MIX // 4
NA_HEAD_DIM = 128
NA_HEADS = NA_WIDTH // NA_HEAD_DIM
NA_WIN_R = 8
NA_WIN_C = 16
ROPE_BASE = 10000.0
POOL_WIDTH = D_MIX - SSD_D_INNER - NA_WIDTH
POOL_WINDOWS = (2, 4, 8, 16)
POOL_GROUP = POOL_WIDTH // len(POOL_WINDOWS)
D_FF = 11008
N_MOD = 6
RMS_EPS = 1e-6
OFF_Z = 0
OFF_Q = OFF_Z + SSD_D_INNER
OFF_POOL = OFF_Q + NA_WIDTH
OFF_XBC = OFF_POOL + POOL_WIDTH
OFF_DT = OFF_XBC + SSD_XBC
OFF_K = OFF_DT + 2 * SSD_HEADS
OFF_V = OFF_K + NA_WIDTH
IN_COLS = OFF_V + NA_WIDTH

VMEM_LIMIT_BYTES = 58 * 1024 * 1024
LANE = 128
BF16_SUBLANE = 16
MOD_ROWS = 8


def _params(*sem):
    return pltpu.CompilerParams(dimension_semantics=sem, vmem_limit_bytes=VMEM_LIMIT_BYTES)


def _mod_kernel(c_ref, w_ref, b_ref, o_ref):
    c = c_ref[...]
    s = (c * jax.nn.sigmoid(c)).astype(BF16)
    o_ref[0] = jnp.dot(s, w_ref[0].astype(BF16), preferred_element_type=F32) + b_ref[0]


def _modulation(c_rows, ada_w, ada_b, tn=1024):
    depth, d, n = ada_w.shape
    return pl.pallas_call(
        _mod_kernel,
        grid=(depth, n // tn),
        in_specs=[pl.BlockSpec((MOD_ROWS, d), lambda l, j: (0, 0)),
                  pl.BlockSpec((1, d, tn), lambda l, j: (l, 0, j)),
                  pl.BlockSpec((1, 1, tn), lambda l, j: (l, 0, j))],
        out_specs=pl.BlockSpec((1, MOD_ROWS, tn), lambda l, j: (l, 0, j)),
        out_shape=jax.ShapeDtypeStruct((depth, MOD_ROWS, n), F32),
        compiler_params=_params("arbitrary", "arbitrary"),
        name="modulation",
    )(c_rows, ada_w, ada_b.reshape(depth, 1, n))


def _norm_kernel(x_ref, w_ref, sc_ref, sh_ref, o_ref):
    x = x_ref[...]
    y = x * lax.rsqrt(jnp.mean(x * x, axis=-1, keepdims=True) + RMS_EPS) * w_ref[...]
    o_ref[...] = (y * (1.0 + sc_ref[...]) + sh_ref[...]).astype(o_ref.dtype)


def _norm_mod(x, w, scale, shift, tm=512):
    m, d = x.shape
    tm = min(tm, m)
    row = pl.BlockSpec((1, d), lambda i: (0, 0))
    return pl.pallas_call(
        _norm_kernel,
        grid=(m // tm,),
        in_specs=[pl.BlockSpec((tm, d), lambda i: (i, 0)), row, row, row],
        out_specs=pl.BlockSpec((tm, d), lambda i: (i, 0)),
        out_shape=jax.ShapeDtypeStruct((m, d), BF16),
        compiler_params=_params("arbitrary"),
        name="norm_mod",
    )(x, w.reshape(1, d), scale.reshape(1, d), shift.reshape(1, d))


def _mm_kernel(a_ref, b_ref, o_ref):
    o_ref[...] = jnp.dot(a_ref[...], b_ref[0], preferred_element_type=F32).astype(o_ref.dtype)


def _matmul(a, b_all, layer, out_dtype, n_cols=None, tm=1024, tn=1024):
    m, k = a.shape
    n = b_all.shape[2] if n_cols is None else n_cols
    tm, tn = min(tm, m), min(tn, n)
    assert n % tn == 0
    return pl.pallas_call(
        _mm_kernel,
        grid=(m // tm, n // tn),
        in_specs=[pl.BlockSpec((tm, k), lambda i, j: (i, 0)),
                  pl.BlockSpec((1, k, tn), lambda i, j: (layer, 0, j))],
        out_specs=pl.BlockSpec((tm, tn), lambda i, j: (i, j)),
        out_shape=jax.ShapeDtypeStruct((m, n), out_dtype),
        compiler_params=_params("arbitrary", "arbitrary"),
        name="matmul",
    )(a, b_all)


def _outproj_kernel(*refs, widths):
    a_refs, (w_ref, x_ref, g_ref, o_ref) = refs[:len(widths)], refs[len(widths):]
    acc, off = None, 0
    for a_ref, k in zip(a_refs, widths):
        part = jnp.dot(a_ref[...], w_ref[0, off:off + k, :], preferred_element_type=F32)
        acc = part if acc is None else acc + part
        off += k
    o_ref[...] = x_ref[...] + g_ref[...] * acc


def _outproj(parts, w_all, layer, x, gate, tm=1024, tn=1024):
    m = x.shape[0]
    _, k, n = w_all.shape
    tm = min(tm, m)
    widths = tuple(a.shape[1] for a in parts)
    assert sum(widths) == k
    return pl.pallas_call(
        functools.partial(_outproj_kernel, widths=widths),
        grid=(m // tm, n // tn),
        in_specs=[pl.BlockSpec((tm, kw), lambda i, j: (i, 0)) for kw in widths] + [
            pl.BlockSpec((1, k, tn), lambda i, j: (layer, 0, j)),
            pl.BlockSpec((tm, tn), lambda i, j: (i, j)),
            pl.BlockSpec((1, tn), lambda i, j: (0, j))],
        out_specs=pl.BlockSpec((tm, tn), lambda i, j: (i, j)),
        out_shape=jax.ShapeDtypeStruct((m, n), F32),
        compiler_params=_params("arbitrary", "arbitrary"),
        name="outproj",
    )(*parts, w_all, x, gate.reshape(1, n))


FFN_HALO = BF16_SUBLANE


def _ffn_kernel(h_ref, hp_ref, hn_ref, wg_ref, wv_ref, tab_ref, wd_ref, x_ref, g_ref,
                o_ref, ext_ref, ua_ref, ub_ref, *, tm, n_mtiles, nf, tn_d):
    i = pl.program_id(0)
    f = pl.program_id(1)
    d = o_ref.shape[1]
    u_refs = (ua_ref, ub_ref)

    @pl.when(f == 0)
    def _():
        ext_ref[0:tm, :] = h_ref[...]
        half = FFN_HALO // 2
        nxt = hn_ref[...].astype(F32)[0:half]
        prv = hp_ref[...].astype(F32)[half:FFN_HALO]
        nxt = jnp.where(i == n_mtiles - 1, jnp.zeros_like(nxt), nxt)
        prv = jnp.where(i == 0, jnp.zeros_like(prv), prv)
        ext_ref[tm:tm + FFN_HALO, :] = jnp.concatenate([nxt, prv], axis=0).astype(BF16)

    def conv(u_ref, br, c):
        um = u_ref[br, 0:tm]
        row = lax.broadcasted_iota(jnp.int32, um.shape, 0)
        before = tm + FFN_HALO - 1
        up = jnp.where(row == 0, u_ref[br, before:before + 1], pltpu.roll(um, 1, 0))
        un = jnp.where(row == tm - 1, u_ref[br, tm:tm + 1], pltpu.roll(um, tm - 1, 0))
        return up * c[0:1] + um * c[1:2] + un * c[2:3] + c[3:4]

    def produce(slot):
        h = ext_ref[...]
        u_refs[slot][0] = jnp.dot(h, wg_ref[0], preferred_element_type=F32)
        u_refs[slot][1] = jnp.dot(h, wv_ref[0], preferred_element_type=F32)

    def consume(slot, first=False, last=False):
        gate = conv(u_refs[slot], 0, tab_ref[f - 1])
        val = conv(u_refs[slot], 1, tab_ref[f - 1 + nf])
        act = (gate * jax.nn.sigmoid(gate) * val).astype(BF16)
        for n in range(d // tn_d):
            sl = slice(n * tn_d, (n + 1) * tn_d)
            part = jnp.dot(act, wd_ref[0, :, sl].astype(BF16), preferred_element_type=F32)
            if first:
                o_ref[:, sl] = part
            elif last:
                o_ref[:, sl] = x_ref[:, sl] + g_ref[:, sl] * (o_ref[:, sl] + part)
            else:
                o_ref[:, sl] += part

    @pl.when(f == 0)
    def _():
        produce(0)

    @pl.when(f == 1)
    def _():
        produce(1)
        consume(0, first=True)

    for parity in (0, 1):
        @pl.when((f >= 2) & (f < nf) & (f % 2 == parity))
        def _():
            produce(parity)
            consume(1 - parity)

    @pl.when(f == nf)
    def _():
        consume((nf - 1) % 2, last=True)


def _conv_ffn(h, w_up_all, conv_tab, w_down_all, layer, x, gate, tm=512, tf=256, tn_d=1024):
    m, d = h.shape
    ff = w_down_all.shape[1]
    tm = min(tm, m)
    tn_d = min(tn_d, d)
    n_mtiles, nf = m // tm, ff // tf
    assert nf >= 2
    hb = tm // FFN_HALO
    last_hb = m // FFN_HALO - 1
    kern = functools.partial(_ffn_kernel, tm=tm, n_mtiles=n_mtiles, nf=nf, tn_d=tn_d)
    tab = conv_tab.reshape(8, 2 * nf, tf).transpose(1, 0, 2)

    def prod(f):
        return jnp.minimum(f, nf - 1)

    def cons(f):
        return jnp.maximum(f - 1, 0)

    return pl.pallas_call(
        kern,
        grid=(n_mtiles, nf + 1),
        in_specs=[
            pl.BlockSpec((tm, d), lambda i, f: (i, 0), pipeline_mode=pl.Buffered(1)),
            pl.BlockSpec((FFN_HALO, d), lambda i, f: (jnp.maximum(i * hb - 1, 0), 0)),
            pl.BlockSpec((FFN_HALO, d), lambda i, f: (jnp.minimum((i + 1) * hb, last_hb), 0)),
            pl.BlockSpec((1, d, tf), lambda i, f: (layer, 0, prod(f))),
            pl.BlockSpec((1, d, tf), lambda i, f: (layer, 0, prod(f) + nf)),
            pl.BlockSpec((2 * nf, 8, tf), lambda i, f: (0, 0, 0)),
            pl.BlockSpec((1, tf, d), lambda i, f: (layer, cons(f), 0)),
            pl.BlockSpec((tm, d), lambda i, f: (i, 0), pipeline_mode=pl.Buffered(1)),
            pl.BlockSpec((1, d), lambda i, f: (0, 0)),
        ],
        out_specs=pl.BlockSpec((tm, d), lambda i, f: (i, 0)),
        out_shape=jax.ShapeDtypeStruct((m, d), F32),
        scratch_shapes=[pltpu.VMEM((tm + FFN_HALO, d), BF16),
                        pltpu.VMEM((2, tm + FFN_HALO, tf), F32),
                        pltpu.VMEM((2, tm + FFN_HALO, tf), F32)],
        compiler_params=_params("arbitrary", "arbitrary"),
        name="conv_ffn",
    )(h, h, h, w_up_all, w_up_all, tab, w_down_all, x, gate.reshape(1, d))


def _rope_tables(length):
    t = jnp.arange(length)
    half = NA_HEAD_DIM // 2
    inv = ROPE_BASE ** (-jnp.arange(0, half, 2, dtype=F32) / half)
    ar = (t // GRID_W).astype(F32)[:, None] * inv
    ac = (t % GRID_W).astype(F32)[:, None] * inv
    ang = jnp.concatenate([ar, ar, ac, ac], axis=-1)
    cos, sin = jnp.cos(ang), jnp.sin(ang)
    first = (jnp.arange(NA_HEAD_DIM) % half) < half // 2
    return cos, jnp.where(first, -sin, 0.0), jnp.where(first, 0.0, sin)


def _qk_prep_kernel(q_ref, k_ref, qw_ref, kw_ref, cos_ref, sa_ref, sb_ref, qo_ref, ko_ref):
    cos, sa, sb = cos_ref[...], sa_ref[...], sb_ref[...]
    quarter = NA_HEAD_DIM // 4
    for h in range(NA_HEADS):
        sl = slice(h * NA_HEAD_DIM, (h + 1) * NA_HEAD_DIM)
        for src, w_ref, dst in ((q_ref, qw_ref, qo_ref), (k_ref, kw_ref, ko_ref)):
            x = src[:, sl].astype(F32)
            y = x * lax.rsqrt(jnp.mean(x * x, axis=-1, keepdims=True) + RMS_EPS) * w_ref[...]
            y = (y * cos + pltpu.roll(y, NA_HEAD_DIM - quarter, 1) * sa + pltpu.roll(y, quarter, 1) * sb)
            dst[:, sl] = y.astype(dst.dtype)


def _qk_prep(p_q, q_blk, p_k, k_blk, qw, kw, tables, tm=512):
    m = p_q.shape[0]
    tm = min(tm, m)
    tab = pl.BlockSpec((tm, NA_HEAD_DIM), lambda i: (i, 0))
    vec = pl.BlockSpec((1, NA_HEAD_DIM), lambda i: (0, 0))
    out = jax.ShapeDtypeStruct((m, NA_WIDTH), BF16)
    return pl.pallas_call(
        _qk_prep_kernel,
        grid=(m // tm,),
        in_specs=[pl.BlockSpec((tm, NA_WIDTH), lambda i: (i, q_blk)),
                  pl.BlockSpec((tm, NA_WIDTH), lambda i: (i, k_blk)), vec, vec, tab, tab, tab],
        out_specs=[pl.BlockSpec((tm, NA_WIDTH), lambda i: (i, 0))] * 2,
        out_shape=[out, out],
        compiler_params=_params("arbitrary"),
        name="qk_prep",
    )(p_q, p_k, qw.reshape(1, -1), kw.reshape(1, -1), *tables)


NA_DR = 2 * NA_WIN_R
NA_BIAS_CHUNK = 1024


def _na_bias_kernel(r0_ref, r1_ref, o_ref):
    def pieces(r):
        out = []
        for _ in range(3):
            piece = r.astype(BF16)
            out.append(piece)
            r = r - piece.astype(F32)
        return out

    p0, p1 = pieces(r0_ref[0]), pieces(r1_ref[0])
    for c in range(GRID_W * LANE // NA_BIAS_CHUNK):
        shape = (LANE, NA_BIAS_CHUNK)
        pos = c * NA_BIAS_CHUNK + lax.broadcasted_iota(jnp.int32, shape, 1)
        j = lax.broadcasted_iota(jnp.int32, shape, 0)
        qc = pos // LANE
        lane = pos % LANE
        kc = lane % GRID_W
        idx = jnp.clip(kc - qc + (NA_WIN_C - 1), 0, 2 * NA_WIN_C - 2)
        hit = idx == j
        second = lane >= GRID_W
        oh0 = jnp.where(hit, jnp.where(second, 0.0, 1.0), 0.0).astype(BF16)
        oh1 = jnp.where(hit, jnp.where(second, 1.0, 0.0), 0.0).astype(BF16)
        acc = jnp.zeros((NA_DR, NA_BIAS_CHUNK), F32)
        for a, b in zip(p0, p1):
            acc = acc + jnp.dot(a, oh0, preferred_element_type=F32) + jnp.dot(b, oh1, preferred_element_type=F32)
        c0 = jnp.clip(qc[0:1] - NA_WIN_C // 2, 0, GRID_W - NA_WIN_C)
        ok = (kc[0:1] >= c0) & (kc[0:1] < c0 + NA_WIN_C)
        o_ref[0, :, c * NA_BIAS_CHUNK:(c + 1) * NA_BIAS_CHUNK] = jnp.where(ok, acc, -jnp.inf)


def _na_bias_table(rpb):
    nh, ndr, ndc = rpb.shape
    r0 = jnp.pad(rpb, ((0, 0), (0, NA_DR - ndr), (0, LANE - ndc)))
    r1 = jnp.pad(rpb[:, 1:], ((0, 0), (0, NA_DR - ndr + 1), (0, LANE - ndc)))
    spec = pl.BlockSpec((1, NA_DR, LANE), lambda h: (h, 0, 0))
    out = pl.pallas_call(
        _na_bias_kernel,
        grid=(nh,),
        in_specs=[spec, spec],
        out_specs=pl.BlockSpec((1, NA_DR, GRID_W * LANE), lambda h: (h, 0, 0)),
        out_shape=jax.ShapeDtypeStruct((nh, NA_DR, GRID_W * LANE), F32),
        compiler_params=_params("arbitrary"),
        name="na_bias",
    )(r0, r1)
    return out.reshape(nh, NA_DR, GRID_W, LANE)


def _softmax_parts(scores):
    m = functools.reduce(jnp.maximum, [jnp.max(s, axis=1, keepdims=True) for s in scores])
    ps = [jnp.exp(s - m) for s in scores]
    den = functools.reduce(jnp.add, [jnp.sum(p, axis=1, keepdims=True) for p in ps])
    return [p.astype(BF16) for p in ps], den


def _pv(parts, values):
    ps, den = parts
    o = functools.reduce(jnp.add, [jnp.dot(p, v, preferred_element_type=F32) for p, v in zip(ps, values)])
    return o / den


def _softmax_pv(scores, values):
    return _pv(_softmax_parts(scores), values)


_NT = (((1,), (1,)), ((), ()))


def _na_kernel(q_ref, k_ref, v_ref, kc_ref, vc_ref, tz_ref, o_ref, *, rows_per_step, n_rows):
    t = pl.program_id(1)
    scale = NA_HEAD_DIM ** -0.5
    band = NA_WIN_R * GRID_W
    kc, vc = kc_ref[...], vc_ref[...]
    starts, scores = [], []
    for i in range(rows_per_step):
        qr = t * rows_per_step + i
        bs = jnp.clip(qr - NA_WIN_R // 2, 0, n_rows - NA_WIN_R)
        off = bs - qr + (NA_WIN_R - 1)
        start = pl.multiple_of(bs * GRID_W, GRID_W)
        q = q_ref[i * GRID_W:(i + 1) * GRID_W, :]
        kb = k_ref[pl.ds(start, band), :]
        bias = jnp.concatenate([tz_ref[0, off + 2 * w] for w in range(NA_WIN_R // 2)], axis=1)
        s_loc = lax.dot_general(q, kb, _NT, preferred_element_type=F32) * scale + bias
        s_ctx = lax.dot_general(q, kc, _NT, preferred_element_type=F32) * scale
        starts.append(start)
        scores.append((s_loc, s_ctx))
    probs = [_softmax_parts(list(s)) for s in scores]
    for i in range(rows_per_step):
        vb = v_ref[pl.ds(starts[i], band), :]
        o = _pv(probs[i], [vb, vc])
        o_ref[i * GRID_W:(i + 1) * GRID_W, :] = o.astype(o_ref.dtype)


def _na_latent(q_r, k_r, p_l, v_blk, kc_r, p_c, vc_blk, tz, rows_per_step=32):
    length = q_r.shape[0]
    lc = kc_r.shape[0]
    n_rows = length // GRID_W
    assert n_rows >= NA_WIN_R and n_rows % rows_per_step == 0
    hd = NA_HEAD_DIM
    kern = functools.partial(_na_kernel, rows_per_step=rows_per_step, n_rows=n_rows)
    return pl.pallas_call(
        kern,
        grid=(NA_HEADS, n_rows // rows_per_step),
        in_specs=[pl.BlockSpec((rows_per_step * GRID_W, hd), lambda h, t: (t, h)),
                  pl.BlockSpec((length, hd), lambda h, t: (0, h)),
                  pl.BlockSpec((length, hd), lambda h, t: (0, v_blk + h)),
                  pl.BlockSpec((lc, hd), lambda h, t: (0, h)),
                  pl.BlockSpec((lc, hd), lambda h, t: (0, vc_blk + h)),
                  pl.BlockSpec((1, NA_DR, GRID_W, LANE), lambda h, t: (h, 0, 0, 0))],
        out_specs=pl.BlockSpec((rows_per_step * GRID_W, hd), lambda h, t: (t, h)),
        out_shape=jax.ShapeDtypeStruct((length, NA_WIDTH), BF16),
        compiler_params=_params("arbitrary", "arbitrary"),
        name="na_latent",
    )(q_r, k_r, p_l, kc_r, p_c, tz)


def _ctx_attn_kernel(q_ref, k_ref, v_ref, o_ref):
    s = lax.dot_general(q_ref[...], k_ref[...], _NT, preferred_element_type=F32) * (NA_HEAD_DIM ** -0.5)
    o_ref[...] = _softmax_pv([s], [v_ref[...]]).astype(o_ref.dtype)


def _na_context(qc_r, kc_r, p_c, vc_blk):
    lc = qc_r.shape[0]
    hd = NA_HEAD_DIM
    spec = pl.BlockSpec((lc, hd), lambda h: (0, h))
    return pl.pallas_call(
        _ctx_attn_kernel,
        grid=(NA_HEADS,),
        in_specs=[spec, spec, pl.BlockSpec((lc, hd), lambda h: (0, vc_blk + h))],
        out_specs=spec,
        out_shape=jax.ShapeDtypeStruct((lc, NA_WIDTH), BF16),
        compiler_params=_params("arbitrary"),
        name="na_context",
    )(qc_r, kc_r, p_c)


POOL_HALO = BF16_SUBLANE


def _pool_kernel(u_ref, up_ref, un_ref, w_ref, sc_ref, o_ref, ext_ref, *, tm, n_tiles, length):
    i = pl.program_id(0)
    prv, nxt = up_ref[...], un_ref[...]
    ext_ref[0:POOL_HALO, :] = jnp.where(i == 0, jnp.zeros_like(prv), prv).astype(F32)
    ext_ref[POOL_HALO:POOL_HALO + tm, :] = u_ref[...].astype(F32)
    ext_ref[POOL_HALO + tm:POOL_HALO + tm + POOL_HALO, :] = jnp.where(i == n_tiles - 1, jnp.zeros_like(nxt),
                                                                     nxt).astype(F32)
    t = i * tm + lax.broadcasted_iota(jnp.int32, (tm, 1), 0)
    for g, w in enumerate(POOL_WINDOWS):
        cs = slice(g * POOL_GROUP, (g + 1) * POOL_GROUP)
        acc = ext_ref[pl.ds(POOL_HALO - w // 2, tm), cs]
        for k in range(1, w):
            acc = acc + ext_ref[pl.ds(POOL_HALO - w // 2 + k, tm), cs]
        cnt = (jnp.minimum(t + w // 2, length) - jnp.maximum(t - w // 2, 0)).astype(F32)
        pooled = acc / cnt - ext_ref[pl.ds(POOL_HALO, tm), cs]
        y = jnp.dot(pooled.astype(BF16), w_ref[g], preferred_element_type=F32) * sc_ref[:, cs]
        o_ref[:, cs] = y.astype(o_ref.dtype)


def _pool(p, u_blk, pool_w, pool_scale, tm=512):
    length = p.shape[0]
    tm = min(tm, length)
    n_tiles = length // tm
    hb = tm // POOL_HALO
    last_hb = length // POOL_HALO - 1
    kern = functools.partial(_pool_kernel, tm=tm, n_tiles=n_tiles, length=length)
    return pl.pallas_call(
        kern,
        grid=(n_tiles,),
        in_specs=[pl.BlockSpec((tm, POOL_WIDTH), lambda i: (i, u_blk)),
                  pl.BlockSpec((POOL_HALO, POOL_WIDTH), lambda i: (jnp.maximum(i * hb - 1, 0), u_blk)),
                  pl.BlockSpec((POOL_HALO, POOL_WIDTH), lambda i: (jnp.minimum((i + 1) * hb, last_hb), u_blk)),
                  pl.BlockSpec((len(POOL_WINDOWS), POOL_GROUP, POOL_GROUP), lambda i: (0, 0, 0)),
                  pl.BlockSpec((1, POOL_WIDTH), lambda i: (0, 0))],
        out_specs=pl.BlockSpec((tm, POOL_WIDTH), lambda i: (i, 0)),
        out_shape=jax.ShapeDtypeStruct((length, POOL_WIDTH), BF16),
        scratch_shapes=[pltpu.VMEM((tm + 2 * POOL_HALO, POOL_WIDTH), F32)],
        compiler_params=_params("arbitrary"),
        name="pool",
    )(p, p, p, pool_w.astype(BF16), pool_scale.reshape(1, POOL_WIDTH))


SSD_HALO = BF16_SUBLANE
SSD_CONV_BLK = 1024
SSD_PAIRS = SSD_HEADS // 2


def _ssd_conv_kernel(u_ref, up_ref, un_ref, tab_ref, o_ref, ext_ref, *, tm, n_tiles):
    i = pl.program_id(0)
    prv, nxt = up_ref[...], un_ref[...]
    ext_ref[0:SSD_HALO, :] = jnp.where(i == 0, jnp.zeros_like(prv), prv).astype(F32)
    ext_ref[SSD_HALO:SSD_HALO + tm, :] = u_ref[...].astype(F32)
    ext_ref[SSD_HALO + tm:SSD_HALO + tm + SSD_HALO, :] = jnp.where(i == n_tiles - 1, jnp.zeros_like(nxt),
                                                                   nxt).astype(F32)
    tab = tab_ref[...]
    left = (SSD_CONV - 1) // 2
    acc = tab[SSD_CONV:SSD_CONV + 1]
    for k in range(SSD_CONV):
        acc = acc + ext_ref[pl.ds(SSD_HALO - left + k, tm), :] * tab[k:k + 1]
    o_ref[...] = (acc * jax.nn.sigmoid(acc)).astype(o_ref.dtype)


def _ssd_conv(p, blk0, conv_w, conv_b, tm=512):
    length = p.shape[0]
    tm = min(tm, length)
    n_tiles = length // tm
    hb = tm // SSD_HALO
    last_hb = length // SSD_HALO - 1
    cb = SSD_CONV_BLK
    tab = jnp.concatenate([conv_w, conv_b[None], jnp.zeros((8 - SSD_CONV - 1, SSD_XBC), F32)], axis=0)
    kern = functools.partial(_ssd_conv_kernel, tm=tm, n_tiles=n_tiles)
    return pl.pallas_call(
        kern,
        grid=(n_tiles, SSD_XBC // cb),
        in_specs=[pl.BlockSpec((tm, cb), lambda i, j: (i, blk0 + j)),
                  pl.BlockSpec((SSD_HALO, cb), lambda i, j: (jnp.maximum(i * hb - 1, 0), blk0 + j)),
                  pl.BlockSpec((SSD_HALO, cb), lambda i, j: (jnp.minimum((i + 1) * hb, last_hb), blk0 + j)),
                  pl.BlockSpec((8, cb), lambda i, j: (0, j))],
        out_specs=pl.BlockSpec((tm, cb), lambda i, j: (i, j)),
        out_shape=jax.ShapeDtypeStruct((length, SSD_XBC), BF16),
        scratch_shapes=[pltpu.VMEM((tm + 2 * SSD_HALO, cb), F32)],
        compiler_params=_params("arbitrary", "arbitrary"),
        name="ssd_conv",
    )(p, p, p, tab)


def _ssd_dt_kernel(raw_ref, bias_ref, alog_ref, dt_ref, da_ref):
    v = raw_ref[...] + bias_ref[...]
    dt = jnp.maximum(v, 0.0) + jnp.log1p(jnp.exp(-jnp.abs(v)))
    dt_ref[...] = dt
    da_ref[...] = dt * -jnp.exp(alog_ref[...])


def _ssd_dt(raw, dt_bias, a_log, tm=2048):
    length = raw.shape[0]
    tm = min(tm, length)
    pad = LANE - 2 * SSD_HEADS
    spec = pl.BlockSpec((tm, LANE), lambda i: (i, 0))
    vec = pl.BlockSpec((1, LANE), lambda i: (0, 0))
    out = jax.ShapeDtypeStruct((length, LANE), F32)
    return pl.pallas_call(
        _ssd_dt_kernel,
        grid=(length // tm,),
        in_specs=[spec, vec, vec],
        out_specs=[spec, spec],
        out_shape=[out, out],
        compiler_params=_params("arbitrary"),
        name="ssd_dt",
    )(raw, jnp.pad(dt_bias.reshape(1, -1), ((0, 0), (0, pad))), jnp.pad(a_log.reshape(1, -1), ((0, 0), (0, pad))))


def _pieces(a):
    out = []
    for _ in range(3):
        piece = a.astype(BF16)
        out.append(piece)
        a = a - piece.astype(F32)
    return out


def _ssd_scan_kernel(*refs, direction, reverse, finalize, chunks):
    if finalize:
        (x_ref, b_ref, c_ref, dac_ref, dtr_ref, dar_ref, h0_ref, yp_ref, z_ref, dsk_ref, nw_ref,
         y_ref, hn_ref, s_ref, yacc_ref) = refs
    else:
        x_ref, b_ref, c_ref, dac_ref, dtr_ref, dar_ref, h0_ref, y_ref, hn_ref, s_ref = refs
        yacc_ref = y_ref
    step = pl.program_id(0)
    nsteps = pl.num_programs(0)
    t_len = SSD_CHUNK

    @pl.when(step == 0)
    def _():
        s_ref[...] = h0_ref[...]

    row = lax.broadcasted_iota(jnp.int32, (t_len, t_len), 0)
    col = lax.broadcasted_iota(jnp.int32, (t_len, t_len), 1)
    mask = (col >= row) if reverse else (col <= row)
    mask_t = (row >= col) if reverse else (row <= col)
    ones = jnp.ones((t_len, t_len), BF16)
    tri = jnp.where(mask, 1.0, 0.0).astype(BF16)
    tri_t = jnp.where(mask_t, 1.0, 0.0).astype(BF16)
    hsl = slice(direction * SSD_HEADS, (direction + 1) * SSD_HEADS)
    lane = lax.broadcasted_iota(jnp.int32, (t_len, LANE), 1)
    first = lane < SSD_HEAD_DIM
    edge = slice(0, 1) if reverse else slice(t_len - 1, t_len)

    for u in range(chunks):
        q = chunks - 1 - u if reverse else u
        rs = slice(q * t_len, (q + 1) * t_len)
        _ssd_chunk(x_ref, b_ref, c_ref, dac_ref, dtr_ref, dar_ref, s_ref, yacc_ref, rs, hsl, direction,
                   mask, tri, tri_t, ones, first, edge)

    @pl.when(step == nsteps - 1)
    def _():
        hn_ref[...] = s_ref[...]

    if finalize:
        y = yacc_ref[...] + yp_ref[...] + x_ref[...].astype(F32) * dsk_ref[...]
        zz = z_ref[...].astype(F32)
        gated = y * (zz * jax.nn.sigmoid(zz))
        out = gated * lax.rsqrt(jnp.mean(gated * gated, axis=-1, keepdims=True) + RMS_EPS) * nw_ref[...]
        y_ref[...] = out.astype(y_ref.dtype)


def _ssd_chunk(x_ref, b_ref, c_ref, dac_ref, dtr_ref, dar_ref, s_ref, yacc_ref, rs, hsl, direction,
               mask, tri, tri_t, ones, first, edge):
    t_len = SSD_CHUNK
    g_col = sum(jnp.dot(tri, piece, preferred_element_type=F32) for piece in _pieces(dac_ref[rs, :]))
    da_row_p = _pieces(dar_ref[hsl, rs])
    g_row = sum(jnp.dot(piece, tri_t, preferred_element_type=F32) for piece in da_row_p)
    tot_row = sum(jnp.dot(piece, ones, preferred_element_type=F32) for piece in da_row_p)
    dt_row = dtr_ref[hsl, rs]
    wdt_row = jnp.exp(tot_row - g_row) * dt_row

    for grp in range(SSD_GROUPS):
        nsl = slice(grp * SSD_STATE, (grp + 1) * SSD_STATE)
        b_g, c_g = b_ref[rs, nsl], c_ref[rs, nsl]
        cb = lax.dot_general(c_g, b_g, _NT, preferred_element_type=F32)
        bt_g = b_g.astype(F32).T
        for k in range(grp * SSD_HPG // 2, (grp + 1) * SSD_HPG // 2):
            ms, es, bws = [], [], []
            for h in (2 * k, 2 * k + 1):
                j = direction * SSD_HEADS + h
                g_bc = jnp.broadcast_to(g_col[:, j:j + 1], (t_len, t_len))
                decay = jnp.where(mask, jnp.exp(g_bc - g_row[h:h + 1, :]), 0.0)
                ms.append((decay * cb * dt_row[h:h + 1, :]).astype(BF16))
                es.append(jnp.exp(g_bc))
                bws.append((bt_g * wdt_row[h:h + 1, :]).astype(BF16))
            csl = slice(k * LANE, (k + 1) * LANE)
            xp = x_ref[rs, csl]
            xa = jnp.where(first, xp, jnp.zeros_like(xp))
            xb = jnp.where(first, jnp.zeros_like(xp), xp)
            e_pair = jnp.where(first, es[0], es[1])
            s_old = s_ref[k]
            y = (jnp.dot(ms[0], xa, preferred_element_type=F32) + jnp.dot(ms[1], xb, preferred_element_type=F32)
                 + jnp.dot(c_g, s_old.astype(BF16), preferred_element_type=F32) * e_pair)
            s_ref[k] = (s_old * e_pair[edge] + jnp.dot(bws[0], xa, preferred_element_type=F32)
                        + jnp.dot(bws[1], xb, preferred_element_type=F32))
            yacc_ref[rs, csl] = y


def _ssd_scan(xbc, da_col, dt_row, da_row, h0, direction, final=None, chunks=4):
    length = xbc.shape[0]
    t_len = SSD_CHUNK
    n_chunks = length // t_len
    chunks = min(chunks, n_chunks)
    n_steps = n_chunks // chunks
    rows = chunks * t_len
    reverse = direction == 1
    pos = (lambda c: n_steps - 1 - c) if reverse else (lambda c: c)
    di, gn = SSD_D_INNER, SSD_GN
    state = jax.ShapeDtypeStruct((SSD_PAIRS, SSD_STATE, LANE), F32)
    state_spec = pl.BlockSpec((SSD_PAIRS, SSD_STATE, LANE), lambda c: (0, 0, 0))
    in_specs = [pl.BlockSpec((rows, di), lambda c: (pos(c), 0)),
                pl.BlockSpec((rows, gn), lambda c: (pos(c), di // gn)),
                pl.BlockSpec((rows, gn), lambda c: (pos(c), di // gn + 1)),
                pl.BlockSpec((rows, LANE), lambda c: (pos(c), 0)),
                pl.BlockSpec((2 * SSD_HEADS, rows), lambda c: (0, pos(c))),
                pl.BlockSpec((2 * SSD_HEADS, rows), lambda c: (0, pos(c))),
                state_spec]
    args = [xbc, xbc, xbc, da_col, dt_row, da_row, h0]
    scratch = [pltpu.VMEM((SSD_PAIRS, SSD_STATE, LANE), F32)]
    y_spec = pl.BlockSpec((rows, di), lambda c: (pos(c), 0))
    row_spec = pl.BlockSpec((1, di), lambda c: (0, 0))
    if final is not None:
        y_other, p, d_skip_row, norm_w_row = final
        in_specs += [y_spec, y_spec, row_spec, row_spec]
        args += [y_other, p, d_skip_row, norm_w_row]
        scratch.append(pltpu.VMEM((rows, di), F32))
    y_dtype = BF16 if final is not None else F32
    kern = functools.partial(_ssd_scan_kernel, direction=direction, reverse=reverse, finalize=final is not None,
                             chunks=chunks)
    return pl.pallas_call(
        kern,
        grid=(n_steps,),
        in_specs=in_specs,
        out_specs=[y_spec, state_spec],
        out_shape=[jax.ShapeDtypeStruct((length, di), y_dtype), state],
        scratch_shapes=scratch,
        compiler_params=_params("arbitrary"),
        name="ssd_scan",
    )(*args)


def _ssd_inputs(p, dt_raw, conv_w, conv_b, dt_bias, a_log):
    xbc = _ssd_conv(p, P_XBC // SSD_CONV_BLK, conv_w, conv_b)
    dt, da = _ssd_dt(dt_raw, dt_bias, a_log)
    nh = 2 * SSD_HEADS
    return xbc, da, dt[:, :nh].T, da[:, :nh].T


def _ssd_pallas(p_l, dt_raw_l, p_c, dt_raw_c, conv_w, conv_b, dt_bias, a_log, d_skip, norm_w, ctx_out):
    xbc_l, dac_l, dtr_l, dar_l = _ssd_inputs(p_l, dt_raw_l, conv_w, conv_b, dt_bias, a_log)
    xbc_c, dac_c, dtr_c, dar_c = _ssd_inputs(p_c, dt_raw_c, conv_w, conv_b, dt_bias, a_log)
    zero = jnp.zeros((SSD_PAIRS, SSD_STATE, LANE), F32)
    fin = (jnp.repeat(d_skip, SSD_HEAD_DIM).reshape(1, -1), norm_w.reshape(1, -1))
    yc0, h_fwd = _ssd_scan(xbc_c, dac_c, dtr_c, dar_c, zero, 0)
    if ctx_out:
        out_c, h_bwd = _ssd_scan(xbc_c, dac_c, dtr_c, dar_c, zero, 1, final=(yc0, p_c) + fin)
    else:
        out_c = None
        _, h_bwd = _ssd_scan(xbc_c, dac_c, dtr_c, dar_c, zero, 1)
    yl0, _ = _ssd_scan(xbc_l, dac_l, dtr_l, dar_l, h_fwd, 0)
    out_l, _ = _ssd_scan(xbc_l, dac_l, dtr_l, dar_l, h_bwd, 1, final=(yl0, p_l) + fin)
    return out_l, out_c


P_Z, P_Q, P_POOL, P_XBC = OFF_Z, OFF_Q, OFF_POOL, OFF_XBC
P_COLS = OFF_DT
KV_K, KV_V = 0, NA_WIDTH


def _layer(layer, x, ctx, mod_l, mod_c, norm1_w, w_in_all, w_kv_all, w_dt_all, ssd_conv_w, ssd_conv_b, ssd_dt_bias,
           ssd_a_log, ssd_d, ssd_norm_w, na_q_norm_w, na_k_norm_w, na_rpb, pool_w, pool_scale, w_out_all, norm2_w,
           w_up_all, mlp_conv_w, mlp_conv_b, w_down_all, rope_l, rope_c, ctx_out):
    sh1_l, sc1_l, g1_l, sh2_l, sc2_l, g2_l = jnp.split(mod_l, N_MOD)
    sh1_c, sc1_c, g1_c, sh2_c, sc2_c, g2_c = jnp.split(mod_c, N_MOD)
    conv_tab = jnp.concatenate([mlp_conv_w, mlp_conv_b[None], jnp.zeros((4, 2 * D_FF), F32)], axis=0)

    h_l = _norm_mod(x, norm1_w, sc1_l, sh1_l)
    h_c = _norm_mod(ctx, norm1_w, sc1_c, sh1_c)
    p_l = _matmul(h_l, w_in_all, layer, BF16, n_cols=P_COLS)
    kv_l = _matmul(h_l, w_kv_all, layer, BF16)
    dt_l = _matmul(h_l, w_dt_all, layer, F32)
    p_c = _matmul(h_c, w_in_all, layer, BF16, n_cols=P_COLS)
    kv_c = _matmul(h_c, w_kv_all, layer, BF16)
    dt_c = _matmul(h_c, w_dt_all, layer, F32)

    y_ssd_l, y_ssd_c = _ssd_pallas(p_l, dt_l, p_c, dt_c, ssd_conv_w, ssd_conv_b, ssd_dt_bias, ssd_a_log, ssd_d,
                                   ssd_norm_w, ctx_out)
    q_blk, k_blk, v_blk = P_Q // NA_WIDTH, KV_K // NA_WIDTH, KV_V // NA_HEAD_DIM
    q_r, k_r = _qk_prep(p_l, q_blk, kv_l, k_blk, na_q_norm_w, na_k_norm_w, rope_l)
    qc_r, kc_r = _qk_prep(p_c, q_blk, kv_c, k_blk, na_q_norm_w, na_k_norm_w, rope_c)
    y_na_l = _na_latent(q_r, k_r, kv_l, v_blk, kc_r, kv_c, v_blk, _na_bias_table(na_rpb))
    y_pool_l = _pool(p_l, P_POOL // POOL_WIDTH, pool_w, pool_scale)

    x_mid = _outproj([y_ssd_l, y_na_l, y_pool_l], w_out_all, layer, x, g1_l)
    h2_l = _norm_mod(x_mid, norm2_w, sc2_l, sh2_l)
    x_new = _conv_ffn(h2_l, w_up_all, conv_tab, w_down_all, layer, x_mid, g2_l)
    ctx_new = ctx
    if ctx_out:
        y_na_c = _na_context(qc_r, kc_r, kv_c, v_blk)
        y_pool_c = _pool(p_c, P_POOL // POOL_WIDTH, pool_w, pool_scale)
        c_mid = _outproj([y_ssd_c, y_na_c, y_pool_c], w_out_all, layer, ctx, g1_c)
        h2_c = _norm_mod(c_mid, norm2_w, sc2_c, sh2_c)
        ctx_new = _conv_ffn(h2_c, w_up_all, conv_tab, w_down_all, layer, c_mid, g2_c)
    return x_new, ctx_new


def kernel(x, c, ctx, c_ctx, ada_w, ada_b, norm1_w, w_in, ssd_conv_w, ssd_conv_b, ssd_dt_bias, ssd_a_log,
           ssd_d, ssd_norm_w, na_q_norm_w, na_k_norm_w, na_rpb, pool_w, pool_scale, w_out, norm2_w,
           mlp_w_up, mlp_conv_w, mlp_conv_b, mlp_w_down):
    depth = ada_w.shape[0]
    d = x.shape[-1]
    c_rows = jnp.concatenate([c.reshape(1, d), c_ctx.reshape(1, d), jnp.zeros((MOD_ROWS - 2, d), F32)], axis=0)
    mods = _modulation(c_rows, ada_w, ada_b)
    xs, cs = x[0], ctx[0]
    rope_l = _rope_tables(xs.shape[0])
    ctx_tab = (cs.shape[0], NA_HEAD_DIM)
    rope_c = (jnp.ones(ctx_tab, F32), jnp.zeros(ctx_tab, F32), jnp.zeros(ctx_tab, F32))
    w_in_all = w_in.astype(BF16)
    w_kv_all = w_in[:, :, OFF_K:].astype(BF16)
    w_dt_all = jnp.pad(w_in[:, :, OFF_DT:OFF_K], ((0, 0), (0, 0), (0, LANE - 2 * SSD_HEADS))).astype(BF16)
    w_out_all = w_out.astype(BF16)
    w_up_all = mlp_w_up.astype(BF16)
    for i in range(depth):
        xs, cs = _layer(i, xs, cs, mods[i, 0], mods[i, 1], norm1_w[i], w_in_all, w_kv_all, w_dt_all, ssd_conv_w[i],
                        ssd_conv_b[i], ssd_dt_bias[i], ssd_a_log[i], ssd_d[i], ssd_norm_w[i], na_q_norm_w[i],
                        na_k_norm_w[i], na_rpb[i], pool_w[i], pool_scale[i], w_out_all, norm2_w[i], w_up_all,
                        mlp_conv_w[i], mlp_conv_b[i], mlp_w_down, rope_l, rope_c, ctx_out=(i < depth - 1))
    return xs[None]
```

```python
import functools
import math

import jax
import jax.numpy as jnp
from jax import lax
from jax.experimental import pallas as pl
from jax.experimental.pallas import tpu as pltpu

F32 = jnp.float32
BF16 = jnp.bfloat16

D_MODEL = 4096
DEPTH = 2
GRID_W = 64
D_MIX = D_MODEL
SSD_D_INNER = D_MIX // 2
SSD_HEAD_DIM = 64
SSD_HEADS = SSD_D_INNER // SSD_HEAD_DIM
SSD_GROUPS = 4
SSD_HPG = SSD_HEADS // SSD_GROUPS
SSD_STATE = 128
SSD_GN = SSD_GROUPS * SSD_STATE
SSD_XBC = SSD_D_INNER + 2 * SSD_GN
SSD_CONV = 5
SSD_CHUNK = 128
NA_WIDTH = D_MIX // 4
NA_HEAD_DIM = 128
NA_HEADS = NA_WIDTH // NA_HEAD_DIM
NA_WIN_R = 8
NA_WIN_C = 16
ROPE_BASE = 10000.0
POOL_WIDTH = D_MIX - SSD_D_INNER - NA_WIDTH
POOL_WINDOWS = (2, 4, 8, 16)
POOL_GROUP = POOL_WIDTH // len(POOL_WINDOWS)
D_FF = 11008
N_MOD = 6
RMS_EPS = 1e-6
OFF_Z = 0
OFF_Q = OFF_Z + SSD_D_INNER
OFF_POOL = OFF_Q + NA_WIDTH
OFF_XBC = OFF_POOL + POOL_WIDTH
OFF_DT = OFF_XBC + SSD_XBC
OFF_K = OFF_DT + 2 * SSD_HEADS
OFF_V = OFF_K + NA_WIDTH
IN_COLS = OFF_V + NA_WIDTH

VMEM_LIMIT_BYTES = 58 * 1024 * 1024
LANE = 128
BF16_SUBLANE = 16
MOD_ROWS = 8


def _params(*sem):
    return pltpu.CompilerParams(dimension_semantics=sem, vmem_limit_bytes=VMEM_LIMIT_BYTES)


def _mod_kernel(c_ref, w_ref, b_ref, o_ref):
    c = c_ref[...]
    s = (c * jax.nn.sigmoid(c)).astype(BF16)
    o_ref[0] = jnp.dot(s, w_ref[0].astype(BF16), preferred_element_type=F32) + b_ref[0]


def _modulation(c_rows, ada_w, ada_b, tn=1024):
    depth, d, n = ada_w.shape
    return pl.pallas_call(
        _mod_kernel,
        grid=(depth, n // tn),
        in_specs=[pl.BlockSpec((MOD_ROWS, d), lambda l, j: (0, 0)),
                  pl.BlockSpec((1, d, tn), lambda l, j: (l, 0, j)),
                  pl.BlockSpec((1, 1, tn), lambda l, j: (l, 0, j))],
        out_specs=pl.BlockSpec((1, MOD_ROWS, tn), lambda l, j: (l, 0, j)),
        out_shape=jax.ShapeDtypeStruct((depth, MOD_ROWS, n), F32),
        compiler_params=_params("arbitrary", "arbitrary"),
        name="modulation",
    )(c_rows, ada_w, ada_b.reshape(depth, 1, n))


def _norm_kernel(x_ref, w_ref, sc_ref, sh_ref, o_ref):
    x = x_ref[...]
    gain = w_ref[...] * (1.0 + sc_ref[...])
    y = x * lax.rsqrt(jnp.mean(x * x, axis=-1, keepdims=True) + RMS_EPS)
    o_ref[...] = (y * gain + sh_ref[...]).astype(o_ref.dtype)


def _norm_mod(x, w, scale, shift, tm=512):
    m, d = x.shape
    tm = min(tm, m)
    row = pl.BlockSpec((1, d), lambda i: (0, 0))
    return pl.pallas_call(
        _norm_kernel,
        grid=(m // tm,),
        in_specs=[pl.BlockSpec((tm, d), lambda i: (i, 0)), row, row, row],
        out_specs=pl.BlockSpec((tm, d), lambda i: (i, 0)),
        out_shape=jax.ShapeDtypeStruct((m, d), BF16),
        compiler_params=_params("arbitrary"),
        name="norm_mod",
    )(x, w.reshape(1, d), scale.reshape(1, d), shift.reshape(1, d))


def _mm_kernel(a_ref, b_ref, o_ref):
    o_ref[...] = jnp.dot(a_ref[...], b_ref[0], preferred_element_type=F32).astype(o_ref.dtype)


def _matmul(a, b_all, layer, out_dtype, n_cols=None, tm=1024, tn=1024):
    m, k = a.shape
    n = b_all.shape[2] if n_cols is None else n_cols
    tm, tn = min(tm, m), min(tn, n)
    assert n % tn == 0
    return pl.pallas_call(
        _mm_kernel,
        grid=(m // tm, n // tn),
        in_specs=[pl.BlockSpec((tm, k), lambda i, j: (i, 0)),
                  pl.BlockSpec((1, k, tn), lambda i, j: (layer, 0, j))],
        out_specs=pl.BlockSpec((tm, tn), lambda i, j: (i, j)),
        out_shape=jax.ShapeDtypeStruct((m, n), out_dtype),
        compiler_params=_params("arbitrary", "arbitrary"),
        name="matmul",
    )(a, b_all)


def _outproj_kernel(*refs, widths):
    a_refs, (w_ref, x_ref, g_ref, o_ref) = refs[:len(widths)], refs[len(widths):]
    acc, off = None, 0
    for a_ref, k in zip(a_refs, widths):
        part = jnp.dot(a_ref[...], w_ref[0, off:off + k, :], preferred_element_type=F32)
        acc = part if acc is None else acc + part
        off += k
    o_ref[...] = x_ref[...] + g_ref[...] * acc


def _outproj(parts, w_all, layer, x, gate, tm=1024, tn=1024):
    m = x.shape[0]
    _, k, n = w_all.shape
    tm = min(tm, m)
    widths = tuple(a.shape[1] for a in parts)
    assert sum(widths) == k
    return pl.pallas_call(
        functools.partial(_outproj_kernel, widths=widths),
        grid=(m // tm, n // tn),
        in_specs=[pl.BlockSpec((tm, kw), lambda i, j: (i, 0)) for kw in widths] + [
            pl.BlockSpec((1, k, tn), lambda i, j: (layer, 0, j)),
            pl.BlockSpec((tm, tn), lambda i, j: (i, j)),
            pl.BlockSpec((1, tn), lambda i, j: (0, j))],
        out_specs=pl.BlockSpec((tm, tn), lambda i, j: (i, j)),
        out_shape=jax.ShapeDtypeStruct((m, n), F32),
        compiler_params=_params("arbitrary", "arbitrary"),
        name="outproj",
    )(*parts, w_all, x, gate.reshape(1, n))


FFN_HALO = BF16_SUBLANE


def _ffn_kernel(h_ref, hp_ref, hn_ref, wg_ref, wv_ref, tab_ref, wd_ref, x_ref, g_ref,
                o_ref, ext_ref, ua_ref, ub_ref, *, tm, n_mtiles, nf, tn_d):
    i = pl.program_id(0)
    f = pl.program_id(1)
    d = o_ref.shape[1]
    u_refs = (ua_ref, ub_ref)

    @pl.when(f == 0)
    def _():
        ext_ref[0:tm, :] = h_ref[...]
        half = FFN_HALO // 2
        nxt = hn_ref[...].astype(F32)[0:half]
        prv = hp_ref[...].astype(F32)[half:FFN_HALO]
        nxt = jnp.where(i == n_mtiles - 1, jnp.zeros_like(nxt), nxt)
        prv = jnp.where(i == 0, jnp.zeros_like(prv), prv)
        ext_ref[tm:tm + FFN_HALO, :] = jnp.concatenate([nxt, prv], axis=0).astype(BF16)

    def conv(u_ref, br, c):
        um = u_ref[br, 0:tm]
        row = lax.broadcasted_iota(jnp.int32, um.shape, 0)
        before = tm + FFN_HALO - 1
        up = jnp.where(row == 0, u_ref[br, before:before + 1], pltpu.roll(um, 1, 0))
        un = jnp.where(row == tm - 1, u_ref[br, tm:tm + 1], pltpu.roll(um, tm - 1, 0))
        return up * c[0:1] + um * c[1:2] + un * c[2:3] + c[3:4]

    def produce(slot):
        h = ext_ref[...]
        u_refs[slot][0] = jnp.dot(h, wg_ref[0], preferred_element_type=F32)
        u_refs[slot][1] = jnp.dot(h, wv_ref[0], preferred_element_type=F32)

    def consume(slot, first=False, last=False):
        gate = conv(u_refs[slot], 0, tab_ref[f - 1])
        val = conv(u_refs[slot], 1, tab_ref[f - 1 + nf])
        act = (gate * jax.nn.sigmoid(gate) * val).astype(BF16)
        for n in range(d // tn_d):
            sl = slice(n * tn_d, (n + 1) * tn_d)
            part = jnp.dot(act, wd_ref[0, :, sl].astype(BF16), preferred_element_type=F32)
            if first:
                o_ref[:, sl] = part
            elif last:
                o_ref[:, sl] = x_ref[:, sl] + g_ref[:, sl] * (o_ref[:, sl] + part)
            else:
                o_ref[:, sl] += part

    @pl.when(f == 0)
    def _():
        produce(0)

    @pl.when(f == 1)
    def _():
        produce(1)
        consume(0, first=True)

    for parity in (0, 1):
        @pl.when((f >= 2) & (f < nf) & (f % 2 == parity))
        def _():
            produce(parity)
            consume(1 - parity)

    @pl.when(f == nf)
    def _():
        consume((nf - 1) % 2, last=True)


def _conv_ffn(h, w_up_all, conv_tab, w_down_all, layer, x, gate, tm=512, tf=256, tn_d=1024):
    m, d = h.shape
    ff = w_down_all.shape[1]
    tm = min(tm, m)
    tn_d = min(tn_d, d)
    n_mtiles, nf = m // tm, ff // tf
    assert nf >= 2
    hb = tm // FFN_HALO
    last_hb = m // FFN_HALO - 1
    kern = functools.partial(_ffn_kernel, tm=tm, n_mtiles=n_mtiles, nf=nf, tn_d=tn_d)
    tab = conv_tab.reshape(8, 2 * nf, tf).transpose(1, 0, 2)

    def prod(f):
        return jnp.minimum(f, nf - 1)

    def cons(f):
        return jnp.maximum(f - 1, 0)

    return pl.pallas_call(
        kern,
        grid=(n_mtiles, nf + 1),
        in_specs=[
            pl.BlockSpec((tm, d), lambda i, f: (i, 0), pipeline_mode=pl.Buffered(1)),
            pl.BlockSpec((FFN_HALO, d), lambda i, f: (jnp.maximum(i * hb - 1, 0), 0)),
            pl.BlockSpec((FFN_HALO, d), lambda i, f: (jnp.minimum((i + 1) * hb, last_hb), 0)),
            pl.BlockSpec((1, d, tf), lambda i, f: (layer, 0, prod(f))),
            pl.BlockSpec((1, d, tf), lambda i, f: (layer, 0, prod(f) + nf)),
            pl.BlockSpec((2 * nf, 8, tf), lambda i, f: (0, 0, 0)),
            pl.BlockSpec((1, tf, d), lambda i, f: (layer, cons(f), 0)),
            pl.BlockSpec((tm, d), lambda i, f: (i, 0), pipeline_mode=pl.Buffered(1)),
            pl.BlockSpec((1, d), lambda i, f: (0, 0)),
        ],
        out_specs=pl.BlockSpec((tm, d), lambda i, f: (i, 0)),
        out_shape=jax.ShapeDtypeStruct((m, d), F32),
        scratch_shapes=[pltpu.VMEM((tm + FFN_HALO, d), BF16),
                        pltpu.VMEM((2, tm + FFN_HALO, tf), F32),
                        pltpu.VMEM((2, tm + FFN_HALO, tf), F32)],
        compiler_params=_params("arbitrary", "arbitrary"),
        name="conv_ffn",
    )(h, h, h, w_up_all, w_up_all, tab, w_down_all, x, gate.reshape(1, d))


def _rope_tables(length):
    t = jnp.arange(length)
    half = NA_HEAD_DIM // 2
    inv = ROPE_BASE ** (-jnp.arange(0, half, 2, dtype=F32) / half)
    ar = (t // GRID_W).astype(F32)[:, None] * inv
    ac = (t % GRID_W).astype(F32)[:, None] * inv
    ang = jnp.concatenate([ar, ar, ac, ac], axis=-1)
    cos, sin = jnp.cos(ang), jnp.sin(ang)
    first = (jnp.arange(NA_HEAD_DIM) % half) < half // 2
    return cos, jnp.where(first, -sin, 0.0), jnp.where(first, 0.0, sin)


def _qk_prep_kernel(q_ref, k_ref, qw_ref, kw_ref, cos_ref, sa_ref, sb_ref, qo_ref, ko_ref):
    cos, sa, sb = cos_ref[...], sa_ref[...], sb_ref[...]
    quarter = NA_HEAD_DIM // 4
    for h in range(NA_HEADS):
        sl = slice(h * NA_HEAD_DIM, (h + 1) * NA_HEAD_DIM)
        for src, w_ref, dst in ((q_ref, qw_ref, qo_ref), (k_ref, kw_ref, ko_ref)):
            x = src[:, sl].astype(F32)
            y = x * lax.rsqrt(jnp.mean(x * x, axis=-1, keepdims=True) + RMS_EPS) * w_ref[...]
            y = (y * cos + pltpu.roll(y, NA_HEAD_DIM - quarter, 1) * sa + pltpu.roll(y, quarter, 1) * sb)
            dst[:, sl] = y.astype(dst.dtype)


def _qk_prep(p_q, q_blk, p_k, k_blk, qw, kw, tables, tm=512):
    m = p_q.shape[0]
    tm = min(tm, m)
    tab = pl.BlockSpec((tm, NA_HEAD_DIM), lambda i: (i, 0))
    vec = pl.BlockSpec((1, NA_HEAD_DIM), lambda i: (0, 0))
    out = jax.ShapeDtypeStruct((m, NA_WIDTH), BF16)
    return pl.pallas_call(
        _qk_prep_kernel,
        grid=(m // tm,),
        in_specs=[pl.BlockSpec((tm, NA_WIDTH), lambda i: (i, q_blk)),
                  pl.BlockSpec((tm, NA_WIDTH), lambda i: (i, k_blk)), vec, vec, tab, tab, tab],
        out_specs=[pl.BlockSpec((tm, NA_WIDTH), lambda i: (i, 0))] * 2,
        out_shape=[out, out],
        compiler_params=_params("arbitrary"),
        name="qk_prep",
    )(p_q, p_k, qw.reshape(1, -1), kw.reshape(1, -1), *tables)


NA_DR = 2 * NA_WIN_R
NA_BIAS_CHUNK = 1024


def _na_bias_kernel(r0_ref, r1_ref, o_ref):
    def pieces(r):
        out = []
        for _ in range(3):
            piece = r.astype(BF16)
            out.append(piece)
            r = r - piece.astype(F32)
        return out

    p0, p1 = pieces(r0_ref[0]), pieces(r1_ref[0])
    for c in range(GRID_W * LANE // NA_BIAS_CHUNK):
        shape = (LANE, NA_BIAS_CHUNK)
        pos = c * NA_BIAS_CHUNK + lax.broadcasted_iota(jnp.int32, shape, 1)
        j = lax.broadcasted_iota(jnp.int32, shape, 0)
        qc = pos // LANE
        lane = pos % LANE
        kc = lane % GRID_W
        idx = jnp.clip(kc - qc + (NA_WIN_C - 1), 0, 2 * NA_WIN_C - 2)
        hit = idx == j
        second = lane >= GRID_W
        oh0 = jnp.where(hit, jnp.where(second, 0.0, 1.0), 0.0).astype(BF16)
        oh1 = jnp.where(hit, jnp.where(second, 1.0, 0.0), 0.0).astype(BF16)
        acc = jnp.zeros((NA_DR, NA_BIAS_CHUNK), F32)
        for a, b in zip(p0, p1):
            acc = acc + jnp.dot(a, oh0, preferred_element_type=F32) + jnp.dot(b, oh1, preferred_element_type=F32)
        c0 = jnp.clip(qc[0:1] - NA_WIN_C // 2, 0, GRID_W - NA_WIN_C)
        ok = (kc[0:1] >= c0) & (kc[0:1] < c0 + NA_WIN_C)
        o_ref[0, :, c * NA_BIAS_CHUNK:(c + 1) * NA_BIAS_CHUNK] = jnp.where(ok, acc, -jnp.inf)


def _na_bias_table(rpb):
    nh, ndr, ndc = rpb.shape
    r0 = jnp.pad(rpb, ((0, 0), (0, NA_DR - ndr), (0, LANE - ndc)))
    r1 = jnp.pad(rpb[:, 1:], ((0, 0), (0, NA_DR - ndr + 1), (0, LANE - ndc)))
    spec = pl.BlockSpec((1, NA_DR, LANE), lambda h: (h, 0, 0))
    out = pl.pallas_call(
        _na_bias_kernel,
        grid=(nh,),
        in_specs=[spec, spec],
        out_specs=pl.BlockSpec((1, NA_DR, GRID_W * LANE), lambda h: (h, 0, 0)),
        out_shape=jax.ShapeDtypeStruct((nh, NA_DR, GRID_W * LANE), F32),
        compiler_params=_params("arbitrary"),
        name="na_bias",
    )(r0, r1)
    return out.reshape(nh, NA_DR, GRID_W, LANE)


def _softmax_parts(scores):
    m = functools.reduce(jnp.maximum, [jnp.max(s, axis=1, keepdims=True) for s in scores])
    ps = [jnp.exp(s - m) for s in scores]
    den = functools.reduce(jnp.add, [jnp.sum(p, axis=1, keepdims=True) for p in ps])
    return [p.astype(BF16) for p in ps], den


def _pv(parts, values):
    ps, den = parts
    o = functools.reduce(jnp.add, [jnp.dot(p, v, preferred_element_type=F32) for p, v in zip(ps, values)])
    return o / den


def _softmax_pv(scores, values):
    return _pv(_softmax_parts(scores), values)


_NT = (((1,), (1,)), ((), ()))


def _na_kernel(q_ref, k_ref, v_ref, kc_ref, vc_ref, tz_ref, o_ref, *, rows_per_step, n_rows):
    t = pl.program_id(1)
    scale = NA_HEAD_DIM ** -0.5
    band = NA_WIN_R * GRID_W
    kc, vc = kc_ref[...], vc_ref[...]
    starts, scores = [], []
    for i in range(rows_per_step):
        qr = t * rows_per_step + i
        bs = jnp.clip(qr - NA_WIN_R // 2, 0, n_rows - NA_WIN_R)
        off = bs - qr + (NA_WIN_R - 1)
        start = pl.multiple_of(bs * GRID_W, GRID_W)
        q = q_ref[i * GRID_W:(i + 1) * GRID_W, :]
        kb = k_ref[pl.ds(start, band), :]
        bias = jnp.concatenate([tz_ref[0, off + 2 * w] for w in range(NA_WIN_R // 2)], axis=1)
        s_loc = lax.dot_general(q, kb, _NT, preferred_element_type=F32) * scale + bias
        s_ctx = lax.dot_general(q, kc, _NT, preferred_element_type=F32) * scale
        starts.append(start)
        scores.append((s_loc, s_ctx))
    probs = [_softmax_parts(list(s)) for s in scores]
    for i in range(rows_per_step):
        vb = v_ref[pl.ds(starts[i], band), :]
        o = _pv(probs[i], [vb, vc])
        o_ref[i * GRID_W:(i + 1) * GRID_W, :] = o.astype(o_ref.dtype)


def _na_latent(q_r, k_r, p_l, v_blk, kc_r, p_c, vc_blk, tz, rows_per_step=32):
    length = q_r.shape[0]
    lc = kc_r.shape[0]
    n_rows = length // GRID_W
    assert n_rows >= NA_WIN_R and n_rows % rows_per_step == 0
    hd = NA_HEAD_DIM
    kern = functools.partial(_na_kernel, rows_per_step=rows_per_step, n_rows=n_rows)
    return pl.pallas_call(
        kern,
        grid=(NA_HEADS, n_rows // rows_per_step),
        in_specs=[pl.BlockSpec((rows_per_step * GRID_W, hd), lambda h, t: (t, h)),
                  pl.BlockSpec((length, hd), lambda h, t: (0, h)),
                  pl.BlockSpec((length, hd), lambda h, t: (0, v_blk + h)),
                  pl.BlockSpec((lc, hd), lambda h, t: (0, h)),
                  pl.BlockSpec((lc, hd), lambda h, t: (0, vc_blk + h)),
                  pl.BlockSpec((1, NA_DR, GRID_W, LANE), lambda h, t: (h, 0, 0, 0))],
        out_specs=pl.BlockSpec((rows_per_step * GRID_W, hd), lambda h, t: (t, h)),
        out_shape=jax.ShapeDtypeStruct((length, NA_WIDTH), BF16),
        compiler_params=_params("arbitrary", "arbitrary"),
        name="na_latent",
    )(q_r, k_r, p_l, kc_r, p_c, tz)


def _ctx_attn_kernel(q_ref, k_ref, v_ref, o_ref):
    s = lax.dot_general(q_ref[...], k_ref[...], _NT, preferred_element_type=F32) * (NA_HEAD_DIM ** -0.5)
    o_ref[...] = _softmax_pv([s], [v_ref[...]]).astype(o_ref.dtype)


def _na_context(qc_r, kc_r, p_c, vc_blk):
    lc = qc_r.shape[0]
    hd = NA_HEAD_DIM
    spec = pl.BlockSpec((lc, hd), lambda h: (0, h))
    return pl.pallas_call(
        _ctx_attn_kernel,
        grid=(NA_HEADS,),
        in_specs=[spec, spec, pl.BlockSpec((lc, hd), lambda h: (0, vc_blk + h))],
        out_specs=spec,
        out_shape=jax.ShapeDtypeStruct((lc, NA_WIDTH), BF16),
        compiler_params=_params("arbitrary"),
        name="na_context",
    )(qc_r, kc_r, p_c)


POOL_HALO = BF16_SUBLANE


def _pool_kernel(u_ref, up_ref, un_ref, w_ref, sc_ref, o_ref, ext_ref, *, tm, n_tiles, length):
    i = pl.program_id(0)
    prv, nxt = up_ref[...], un_ref[...]
    ext_ref[0:POOL_HALO, :] = jnp.where(i == 0, jnp.zeros_like(prv), prv).astype(F32)
    ext_ref[POOL_HALO:POOL_HALO + tm, :] = u_ref[...].astype(F32)
    ext_ref[POOL_HALO + tm:POOL_HALO + tm + POOL_HALO, :] = jnp.where(i == n_tiles - 1, jnp.zeros_like(nxt),
                                                                     nxt).astype(F32)
    t = i * tm + lax.broadcasted_iota(jnp.int32, (tm, 1), 0)
    for g, w in enumerate(POOL_WINDOWS):
        cs = slice(g * POOL_GROUP, (g + 1) * POOL_GROUP)
        acc = ext_ref[pl.ds(POOL_HALO - w // 2, tm), cs]
        for k in range(1, w):
            acc = acc + ext_ref[pl.ds(POOL_HALO - w // 2 + k, tm), cs]
        cnt = (jnp.minimum(t + w // 2, length) - jnp.maximum(t - w // 2, 0)).astype(F32)
        pooled = acc / cnt - ext_ref[pl.ds(POOL_HALO, tm), cs]
        y = jnp.dot(pooled.astype(BF16), w_ref[g], preferred_element_type=F32) * sc_ref[:, cs]
        o_ref[:, cs] = y.astype(o_ref.dtype)


def _pool(p, u_blk, pool_w, pool_scale, tm=512):
    length = p.shape[0]
    tm = min(tm, length)
    n_tiles = length // tm
    hb = tm // POOL_HALO
    last_hb = length // POOL_HALO - 1
    kern = functools.partial(_pool_kernel, tm=tm, n_tiles=n_tiles, length=length)
    return pl.pallas_call(
        kern,
        grid=(n_tiles,),
        in_specs=[pl.BlockSpec((tm, POOL_WIDTH), lambda i: (i, u_blk)),
                  pl.BlockSpec((POOL_HALO, POOL_WIDTH), lambda i: (jnp.maximum(i * hb - 1, 0), u_blk)),
                  pl.BlockSpec((POOL_HALO, POOL_WIDTH), lambda i: (jnp.minimum((i + 1) * hb, last_hb), u_blk)),
                  pl.BlockSpec((len(POOL_WINDOWS), POOL_GROUP, POOL_GROUP), lambda i: (0, 0, 0)),
                  pl.BlockSpec((1, POOL_WIDTH), lambda i: (0, 0))],
        out_specs=pl.BlockSpec((tm, POOL_WIDTH), lambda i: (i, 0)),
        out_shape=jax.ShapeDtypeStruct((length, POOL_WIDTH), BF16),
        scratch_shapes=[pltpu.VMEM((tm + 2 * POOL_HALO, POOL_WIDTH), F32)],
        compiler_params=_params("arbitrary"),
        name="pool",
    )(p, p, p, pool_w.astype(BF16), pool_scale.reshape(1, POOL_WIDTH))


SSD_HALO = BF16_SUBLANE
SSD_CONV_BLK = 1024
SSD_PAIRS = SSD_HEADS // 2


def _ssd_conv_kernel(u_ref, up_ref, un_ref, tab_ref, o_ref, ext_ref, *, tm, n_tiles):
    i = pl.program_id(0)
    prv, nxt = up_ref[...], un_ref[...]
    ext_ref[0:SSD_HALO, :] = jnp.where(i == 0, jnp.zeros_like(prv), prv).astype(F32)
    ext_ref[SSD_HALO:SSD_HALO + tm, :] = u_ref[...].astype(F32)
    ext_ref[SSD_HALO + tm:SSD_HALO + tm + SSD_HALO, :] = jnp.where(i == n_tiles - 1, jnp.zeros_like(nxt),
                                                                   nxt).astype(F32)
    tab = tab_ref[...]
    left = (SSD_CONV - 1) // 2
    acc = tab[SSD_CONV:SSD_CONV + 1]
    for k in range(SSD_CONV):
        acc = acc + ext_ref[pl.ds(SSD_HALO - left + k, tm), :] * tab[k:k + 1]
    o_ref[...] = (acc * jax.nn.sigmoid(acc)).astype(o_ref.dtype)


def _ssd_conv(p, blk0, conv_w, conv_b, tm=512):
    length = p.shape[0]
    tm = min(tm, length)
    n_tiles = length // tm
    hb = tm // SSD_HALO
    last_hb = length // SSD_HALO - 1
    cb = SSD_CONV_BLK
    tab = jnp.concatenate([conv_w, conv_b[None], jnp.zeros((8 - SSD_CONV - 1, SSD_XBC), F32)], axis=0)
    kern = functools.partial(_ssd_conv_kernel, tm=tm, n_tiles=n_tiles)
    return pl.pallas_call(
        kern,
        grid=(n_tiles, SSD_XBC // cb),
        in_specs=[pl.BlockSpec((tm, cb), lambda i, j: (i, blk0 + j)),
                  pl.BlockSpec((SSD_HALO, cb), lambda i, j: (jnp.maximum(i * hb - 1, 0), blk0 + j)),
                  pl.BlockSpec((SSD_HALO, cb), lambda i, j: (jnp.minimum((i + 1) * hb, last_hb), blk0 + j)),
                  pl.BlockSpec((8, cb), lambda i, j: (0, j))],
        out_specs=pl.BlockSpec((tm, cb), lambda i, j: (i, j)),
        out_shape=jax.ShapeDtypeStruct((length, SSD_XBC), BF16),
        scratch_shapes=[pltpu.VMEM((tm + 2 * SSD_HALO, cb), F32)],
        compiler_params=_params("arbitrary", "arbitrary"),
        name="ssd_conv",
    )(p, p, p, tab)


def _ssd_dt_kernel(raw_ref, bias_ref, alog_ref, dt_ref, da_ref):
    v = raw_ref[...] + bias_ref[...]
    dt = jnp.maximum(v, 0.0) + jnp.log1p(jnp.exp(-jnp.abs(v)))
    dt_ref[...] = dt
    da_ref[...] = dt * -jnp.exp(alog_ref[...])


def _ssd_dt(raw, dt_bias, a_log, tm=2048):
    length = raw.shape[0]
    tm = min(tm, length)
    pad = LANE - 2 * SSD_HEADS
    spec = pl.BlockSpec((tm, LANE), lambda i: (i, 0))
    vec = pl.BlockSpec((1, LANE), lambda i: (0, 0))
    out = jax.ShapeDtypeStruct((length, LANE), F32)
    return pl.pallas_call(
        _ssd_dt_kernel,
        grid=(length // tm,),
        in_specs=[spec, vec, vec],
        out_specs=[spec, spec],
        out_shape=[out, out],
        compiler_params=_params("arbitrary"),
        name="ssd_dt",
    )(raw, jnp.pad(dt_bias.reshape(1, -1), ((0, 0), (0, pad))), jnp.pad(a_log.reshape(1, -1), ((0, 0), (0, pad))))


def _pieces(a):
    out = []
    for _ in range(3):
        piece = a.astype(BF16)
        out.append(piece)
        a = a - piece.astype(F32)
    return out


def _ssd_scan_kernel(*refs, direction, reverse, finalize, chunks):
    if finalize:
        (x_ref, b_ref, c_ref, dac_ref, dtr_ref, dar_ref, h0_ref, yp_ref, z_ref, dsk_ref, nw_ref,
         y_ref, hn_ref, s_ref, yacc_ref) = refs
    else:
        x_ref, b_ref, c_ref, dac_ref, dtr_ref, dar_ref, h0_ref, y_ref, hn_ref, s_ref = refs
        yacc_ref = y_ref
    step = pl.program_id(0)
    nsteps = pl.num_programs(0)
    t_len = SSD_CHUNK

    @pl.when(step == 0)
    def _():
        s_ref[...] = h0_ref[...]

    row = lax.broadcasted_iota(jnp.int32, (t_len, t_len), 0)
    col = lax.broadcasted_iota(jnp.int32, (t_len, t_len), 1)
    mask = (col >= row) if reverse else (col <= row)
    mask_t = (row >= col) if reverse else (row <= col)
    ones = jnp.ones((t_len, t_len), BF16)
    tri = jnp.where(mask, 1.0, 0.0).astype(BF16)
    tri_t = jnp.where(mask_t, 1.0, 0.0).astype(BF16)
    hsl = slice(direction * SSD_HEADS, (direction + 1) * SSD_HEADS)
    lane = lax.broadcasted_iota(jnp.int32, (t_len, LANE), 1)
    first = lane < SSD_HEAD_DIM
    edge = slice(0, 1) if reverse else slice(t_len - 1, t_len)

    for u in range(chunks):
        q = chunks - 1 - u if reverse else u
        rs = slice(q * t_len, (q + 1) * t_len)
        _ssd_chunk(x_ref, b_ref, c_ref, dac_ref, dtr_ref, dar_ref, s_ref, yacc_ref, rs, hsl, direction,
                   mask, tri, tri_t, ones, first, edge)

    @pl.when(step == nsteps - 1)
    def _():
        hn_ref[...] = s_ref[...]

    if finalize:
        y = yacc_ref[...] + yp_ref[...] + x_ref[...].astype(F32) * dsk_ref[...]
        zz = z_ref[...].astype(F32)
        gated = y * (zz * jax.nn.sigmoid(zz))
        out = gated * lax.rsqrt(jnp.mean(gated * gated, axis=-1, keepdims=True) + RMS_EPS) * nw_ref[...]
        y_ref[...] = out.astype(y_ref.dtype)


def _ssd_chunk(x_ref, b_ref, c_ref, dac_ref, dtr_ref, dar_ref, s_ref, yacc_ref, rs, hsl, direction,
               mask, tri, tri_t, ones, first, edge):
    t_len = SSD_CHUNK
    g_col = sum(jnp.dot(tri, piece, preferred_element_type=F32) for piece in _pieces(dac_ref[rs, :]))
    da_row_p = _pieces(dar_ref[hsl, rs])
    g_row = sum(jnp.dot(piece, tri_t, preferred_element_type=F32) for piece in da_row_p)
    tot_row = sum(jnp.dot(piece, ones, preferred_element_type=F32) for piece in da_row_p)
    dt_row = dtr_ref[hsl, rs]
    wdt_row = jnp.exp(tot_row - g_row) * dt_row

    for grp in range(SSD_GROUPS):
        nsl = slice(grp * SSD_STATE, (grp + 1) * SSD_STATE)
        b_g, c_g = b_ref[rs, nsl], c_ref[rs, nsl]
        cb = lax.dot_general(c_g, b_g, _NT, preferred_element_type=F32)
        bt_g = b_g.astype(F32).T
        for k in range(grp * SSD_HPG // 2, (grp + 1) * SSD_HPG // 2):
            ms, es, bws = [], [], []
            for h in (2 * k, 2 * k + 1):
                j = direction * SSD_HEADS + h
                g_bc = jnp.broadcast_to(g_col[:, j:j + 1], (t_len, t_len))
                decay = jnp.where(mask, jnp.exp(g_bc - g_row[h:h + 1, :]), 0.0)
                ms.append((decay * cb * dt_row[h:h + 1, :]).astype(BF16))
                es.append(jnp.exp(g_bc))
                bws.append((bt_g * wdt_row[h:h + 1, :]).astype(BF16))
            csl = slice(k * LANE, (k + 1) * LANE)
            xp = x_ref[rs, csl]
            xa = jnp.where(first, xp, jnp.zeros_like(xp))
            xb = jnp.where(first, jnp.zeros_like(xp), xp)
            e_pair = jnp.where(first, es[0], es[1])
            s_old = s_ref[k]
            y = (jnp.dot(ms[0], xa, preferred_element_type=F32) + jnp.dot(ms[1], xb, preferred_element_type=F32)
                 + jnp.dot(c_g, s_old.astype(BF16), preferred_element_type=F32) * e_pair)
            s_ref[k] = (s_old * e_pair[edge] + jnp.dot(bws[0], xa, preferred_element_type=F32)
                        + jnp.dot(bws[1], xb, preferred_element_type=F32))
            yacc_ref[rs, csl] = y


def _ssd_scan(xbc, da_col, dt_row, da_row, h0, direction, final=None, chunks=4):
    length = xbc.shape[0]
    t_len = SSD_CHUNK
    n_chunks = length // t_len
    chunks = min(chunks, n_chunks)
    n_steps = n_chunks // chunks
    rows = chunks * t_len
    reverse = direction == 1
    pos = (lambda c: n_steps - 1 - c) if reverse else (lambda c: c)
    di, gn = SSD_D_INNER, SSD_GN
    state = jax.ShapeDtypeStruct((SSD_PAIRS, SSD_STATE, LANE), F32)
    state_spec = pl.BlockSpec((SSD_PAIRS, SSD_STATE, LANE), lambda c: (0, 0, 0))
    in_specs = [pl.BlockSpec((rows, di), lambda c: (pos(c), 0)),
                pl.BlockSpec((rows, gn), lambda c: (pos(c), di // gn)),
                pl.BlockSpec((rows, gn), lambda c: (pos(c), di // gn + 1)),
                pl.BlockSpec((rows, LANE), lambda c: (pos(c), 0)),
                pl.BlockSpec((2 * SSD_HEADS, rows), lambda c: (0, pos(c))),
                pl.BlockSpec((2 * SSD_HEADS, rows), lambda c: (0, pos(c))),
                state_spec]
    args = [xbc, xbc, xbc, da_col, dt_row, da_row, h0]
    scratch = [pltpu.VMEM((SSD_PAIRS, SSD_STATE, LANE), F32)]
    y_spec = pl.BlockSpec((rows, di), lambda c: (pos(c), 0))
    row_spec = pl.BlockSpec((1, di), lambda c: (0, 0))
    if final is not None:
        y_other, p, d_skip_row, norm_w_row = final
        in_specs += [y_spec, y_spec, row_spec, row_spec]
        args += [y_other, p, d_skip_row, norm_w_row]
        scratch.append(pltpu.VMEM((rows, di), F32))
    y_dtype = BF16 if final is not None else F32
    kern = functools.partial(_ssd_scan_kernel, direction=direction, reverse=reverse, finalize=final is not None,
                             chunks=chunks)
    return pl.pallas_call(
        kern,
        grid=(n_steps,),
        in_specs=in_specs,
        out_specs=[y_spec, state_spec],
        out_shape=[jax.ShapeDtypeStruct((length, di), y_dtype), state],
        scratch_shapes=scratch,
        compiler_params=_params("arbitrary"),
        name="ssd_scan",
    )(*args)


def _ssd_inputs(p, dt_raw, conv_w, conv_b, dt_bias, a_log):
    xbc = _ssd_conv(p, P_XBC // SSD_CONV_BLK, conv_w, conv_b)
    dt, da = _ssd_dt(dt_raw, dt_bias, a_log)
    nh = 2 * SSD_HEADS
    return xbc, da, dt[:, :nh].T, da[:, :nh].T


def _ssd_pallas(p_l, dt_raw_l, p_c, dt_raw_c, conv_w, conv_b, dt_bias, a_log, d_skip, norm_w, ctx_out):
    xbc_l, dac_l, dtr_l, dar_l = _ssd_inputs(p_l, dt_raw_l, conv_w, conv_b, dt_bias, a_log)
    xbc_c, dac_c, dtr_c, dar_c = _ssd_inputs(p_c, dt_raw_c, conv_w, conv_b, dt_bias, a_log)
    zero = jnp.zeros((SSD_PAIRS, SSD_STATE, LANE), F32)
    fin = (jnp.repeat(d_skip, SSD_HEAD_DIM).reshape(1, -1), norm_w.reshape(1, -1))
    yc0, h_fwd = _ssd_scan(xbc_c, dac_c, dtr_c, dar_c, zero, 0)
    if ctx_out:
        out_c, h_bwd = _ssd_scan(xbc_c, dac_c, dtr_c, dar_c, zero, 1, final=(yc0, p_c) + fin)
    else:
        out_c = None
        _, h_bwd = _ssd_scan(xbc_c, dac_c, dtr_c, dar_c, zero, 1)
    yl0, _ = _ssd_scan(xbc_l, dac_l, dtr_l, dar_l, h_fwd, 0)
    out_l, _ = _ssd_scan(xbc_l, dac_l, dtr_l, dar_l, h_bwd, 1, final=(yl0, p_l) + fin)
    return out_l, out_c


P_Z, P_Q, P_POOL, P_XBC = OFF_Z, OFF_Q, OFF_POOL, OFF_XBC
P_COLS = OFF_DT
KV_K, KV_V = 0, NA_WIDTH


def _layer(layer, x, ctx, mod_l, mod_c, norm1_w, w_in_all, w_kv_all, w_dt_all, ssd_conv_w, ssd_conv_b, ssd_dt_bias,
           ssd_a_log, ssd_d, ssd_norm_w, na_q_norm_w, na_k_norm_w, na_rpb, pool_w, pool_scale, w_out_all, norm2_w,
           w_up_all, mlp_conv_w, mlp_conv_b, w_down_all, rope_l, rope_c, ctx_out):
    sh1_l, sc1_l, g1_l, sh2_l, sc2_l, g2_l = jnp.split(mod_l, N_MOD)
    sh1_c, sc1_c, g1_c, sh2_c, sc2_c, g2_c = jnp.split(mod_c, N_MOD)
    conv_tab = jnp.concatenate([mlp_conv_w, mlp_conv_b[None], jnp.zeros((4, 2 * D_FF), F32)], axis=0)

    h_l = _norm_mod(x, norm1_w, sc1_l, sh1_l)
    h_c = _norm_mod(ctx, norm1_w, sc1_c, sh1_c)
    p_l = _matmul(h_l, w_in_all, layer, BF16, n_cols=P_COLS)
    kv_l = _matmul(h_l, w_kv_all, layer, BF16)
    dt_l = _matmul(h_l, w_dt_all, layer, F32)
    p_c = _matmul(h_c, w_in_all, layer, BF16, n_cols=P_COLS)
    kv_c = _matmul(h_c, w_kv_all, layer, BF16)
    dt_c = _matmul(h_c, w_dt_all, layer, F32)

    y_ssd_l, y_ssd_c = _ssd_pallas(p_l, dt_l, p_c, dt_c, ssd_conv_w, ssd_conv_b, ssd_dt_bias, ssd_a_log, ssd_d,
                                   ssd_norm_w, ctx_out)
    q_blk, k_blk, v_blk = P_Q // NA_WIDTH, KV_K // NA_WIDTH, KV_V // NA_HEAD_DIM
    q_r, k_r = _qk_prep(p_l, q_blk, kv_l, k_blk, na_q_norm_w, na_k_norm_w, rope_l)
    qc_r, kc_r = _qk_prep(p_c, q_blk, kv_c, k_blk, na_q_norm_w, na_k_norm_w, rope_c)
    y_na_l = _na_latent(q_r, k_r, kv_l, v_blk, kc_r, kv_c, v_blk, _na_bias_table(na_rpb))
    y_pool_l = _pool(p_l, P_POOL // POOL_WIDTH, pool_w, pool_scale)

    x_mid = _outproj([y_ssd_l, y_na_l, y_pool_l], w_out_all, layer, x, g1_l)
    h2_l = _norm_mod(x_mid, norm2_w, sc2_l, sh2_l)
    x_new = _conv_ffn(h2_l, w_up_all, conv_tab, w_down_all, layer, x_mid, g2_l)
    ctx_new = ctx
    if ctx_out:
        y_na_c = _na_context(qc_r, kc_r, kv_c, v_blk)
        y_pool_c = _pool(p_c, P_POOL // POOL_WIDTH, pool_w, pool_scale)
        c_mid = _outproj([y_ssd_c, y_na_c, y_pool_c], w_out_all, layer, ctx, g1_c)
        h2_c = _norm_mod(c_mid, norm2_w, sc2_c, sh2_c)
        ctx_new = _conv_ffn(h2_c, w_up_all, conv_tab, w_down_all, layer, c_mid, g2_c)
    return x_new, ctx_new


def kernel(x, c, ctx, c_ctx, ada_w, ada_b, norm1_w, w_in, ssd_conv_w, ssd_conv_b, ssd_dt_bias, ssd_a_log,
           ssd_d, ssd_norm_w, na_q_norm_w, na_k_norm_w, na_rpb, pool_w, pool_scale, w_out, norm2_w,
           mlp_w_up, mlp_conv_w, mlp_conv_b, mlp_w_down):
    depth = ada_w.shape[0]
    d = x.shape[-1]
    c_rows = jnp.concatenate([c.reshape(1, d), c_ctx.reshape(1, d), jnp.zeros((MOD_ROWS - 2, d), F32)], axis=0)
    mods = _modulation(c_rows, ada_w, ada_b)
    xs, cs = x[0], ctx[0]
    rope_l = _rope_tables(xs.shape[0])
    ctx_tab = (cs.shape[0], NA_HEAD_DIM)
    rope_c = (jnp.ones(ctx_tab, F32), jnp.zeros(ctx_tab, F32), jnp.zeros(ctx_tab, F32))
    w_in_all = w_in.astype(BF16)
    w_kv_all = w_in[:, :, OFF_K:].astype(BF16)
    w_dt_all = jnp.pad(w_in[:, :, OFF_DT:OFF_K], ((0, 0), (0, 0), (0, LANE - 2 * SSD_HEADS))).astype(BF16)
    w_out_all = w_out.astype(BF16)
    w_up_all = mlp_w_up.astype(BF16)
    for i in range(depth):
        xs, cs = _layer(i, xs, cs, mods[i, 0], mods[i, 1], norm1_w[i], w_in_all, w_kv_all, w_dt_all, ssd_conv_w[i],
                        ssd_conv_b[i], ssd_dt_bias[i], ssd_a_log[i], ssd_d[i], ssd_norm_w[i], na_q_norm_w[i],
                        na_k_norm_w[i], na_rpb[i], pool_w[i], pool_scale[i], w_out_all, norm2_w[i], w_up_all,
                        mlp_conv_w[i], mlp_conv_b[i], mlp_w_down, rope_l, rope_c, ctx_out=(i < depth - 1))
    return xs[None]
```
